```python
import math
import jax, jax.numpy as jnp
from jax import lax
import numpy as np

D_MODEL = 4096
BATCH = 2
SEQ = 8192
DEPTH = 1
DEC_BATCH = 16
DEC_SEQ = 64
PAST_LEN = 2048

CHUNK = 64
EPS = 1e-6

M_EXPAND = 2
M_D_INNER = M_EXPAND * D_MODEL
M_HEAD_DIM = 64
M_HEADS = M_D_INNER // M_HEAD_DIM
M_GROUPS = 8
M_HPG = M_HEADS // M_GROUPS
M_STATE = 128
M_CONV = 4
M_BC = M_GROUPS * M_STATE
M_CONV_DIM = M_D_INNER + 2 * M_BC

A_HEAD_DIM = 64
A_HEADS = D_MODEL // (2 * A_HEAD_DIM)
A_KV_HEADS = 8
A_REP = A_HEADS // A_KV_HEADS
A_WIDTH = A_HEADS * 2 * A_HEAD_DIM
A_KV_WIDTH = A_KV_HEADS * 2 * A_HEAD_DIM
ROPE_DIM = A_HEAD_DIM // 4
ROPE_THETA = 500000.0
Q_BLOCK = 128

IN_SIZES = (M_D_INNER, M_CONV_DIM, M_HEADS, A_WIDTH, A_KV_WIDTH, A_KV_WIDTH, A_WIDTH, D_MODEL, D_MODEL)
IN_TOTAL = sum(IN_SIZES)

kernel_name = 'hybrid_ssd_diffattn_stream_step'


def lambda_init(layer_idx):
    return 0.8 - 0.6 * math.exp(-0.3 * layer_idx)


def rms_norm(x, w):
    xf = x.astype(jnp.float32)
    y = xf * lax.rsqrt(jnp.mean(xf * xf, axis=-1, keepdims=True) + EPS)
    return (y * w.astype(jnp.float32)).astype(x.dtype)


def partial_rope(t, pos):
    half = ROPE_DIM // 2
    inv = ROPE_THETA ** (-jnp.arange(half, dtype=jnp.float32) / half)
    ang = pos.astype(jnp.float32)[:, None] * inv[None, :]
    shape = (1, t.shape[1]) + (1,) * (t.ndim - 3) + (half,)
    cos = jnp.cos(ang).reshape(shape)
    sin = jnp.sin(ang).reshape(shape)
    r1 = t[..., :half].astype(jnp.float32)
    r2 = t[..., half:ROPE_DIM].astype(jnp.float32)
    rot = jnp.concatenate([r1 * cos - r2 * sin, r2 * cos + r1 * sin], axis=-1).astype(t.dtype)
    return jnp.concatenate([rot, t[..., ROPE_DIM:]], axis=-1)


def ssd_chunk(x, dt, bm, cm, a, state):
    L = x.shape[1]
    acum = jnp.cumsum(dt * a, axis=1)
    seg = acum[:, :, None] - acum[:, None, :]
    causal = jnp.tril(jnp.ones((L, L), dtype=bool))[None, :, :, None, None]
    decay = jnp.exp(jnp.where(causal, seg, -jnp.inf))
    xdt = x.astype(jnp.float32) * dt[..., None]
    bf = bm.astype(jnp.float32)
    cf = cm.astype(jnp.float32)
    sf = state.astype(jnp.float32)
    cb = jnp.einsum('blgn,bsgn->blsg', cf, bf)
    y_diag = jnp.einsum('blsg,blsgj,bsgjp->blgjp', cb, decay, xdt)
    y_off = jnp.einsum('blgn,bgjpn->blgjp', cf, sf) * jnp.exp(acum)[..., None]
    to_end = jnp.exp(acum[:, -1:] - acum)
    new_state = sf * jnp.exp(acum[:, -1])[..., None, None] + jnp.einsum('bsgn,bsgj,bsgjp->bgjpn', bf, to_end, xdt)
    return (y_diag + y_off).astype(x.dtype), new_state.astype(state.dtype)


def mamba_branch(z, xbc, dt_raw, conv_state, ssm_state, conv_w, conv_b, dt_bias, a_log, d_skip, norm_w):
    b, T, _ = xbc.shape
    xp = jnp.concatenate([conv_state, xbc], axis=1)
    new_conv = xp[:, -(M_CONV - 1):]
    conv = conv_b + sum(xp[:, k:k + T] * conv_w[k] for k in range(M_CONV))
    conv = jax.nn.silu(conv)
    xs, bm, cm = jnp.split(conv, [M_D_INNER, M_D_INNER + M_BC], axis=-1)
    xs = xs.reshape(b, T, M_GROUPS, M_HPG, M_HEAD_DIM)
    bm = bm.reshape(b, T, M_GROUPS, M_STATE)
    cm = cm.reshape(b, T, M_GROUPS, M_STATE)
    dt = jax.nn.softplus(dt_raw.astype(jnp.float32) + dt_bias.astype(jnp.float32)).reshape(b, T, M_GROUPS, M_HPG)
    a = -jnp.exp(a_log.astype(jnp.float32)).reshape(M_GROUPS, M_HPG)
    L = min(CHUNK, T)
    nc = T // L

    def to_chunks(t):
        return jnp.moveaxis(t.reshape((b, nc, L) + t.shape[2:]), 1, 0)

    def step(state, inp):
        xc, dtc, bc, cc = inp
        y, state = ssd_chunk(xc, dtc, bc, cc, a, state)
        return state, y

    s0 = ssm_state.reshape(b, M_GROUPS, M_HPG, M_HEAD_DIM, M_STATE)
    s_final, ys = lax.scan(step, s0, (to_chunks(xs), to_chunks(dt), to_chunks(bm), to_chunks(cm)))
    y = jnp.moveaxis(ys, 0, 1).reshape(b, T, M_GROUPS, M_HPG, M_HEAD_DIM)
    y = y + (d_skip.reshape(M_GROUPS, M_HPG)[..., None] * xs).astype(y.dtype)
    y = y.reshape(b, T, M_D_INNER) * jax.nn.silu(z)
    y = rms_norm(y.reshape(b, T, M_GROUPS, M_D_INNER // M_GROUPS), norm_w.reshape(M_GROUPS, -1))
    return y.reshape(b, T, M_D_INNER), new_conv, s_final.reshape(b, M_HEADS, M_HEAD_DIM, M_STATE)


def diff_attn_branch(q, k, v, za, k_past, v_past, q_pos, k_pos, lq1, lk1, lq2, lk2, norm_w, lam_init):
    b, T, _ = q.shape
    q = partial_rope(q.reshape(b, T, A_KV_HEADS, A_REP, 2, A_HEAD_DIM), q_pos)
    k = partial_rope(k.reshape(b, T, A_KV_HEADS, 2, A_HEAD_DIM), q_pos)
    v = v.reshape(b, T, A_KV_HEADS, 2 * A_HEAD_DIM)
    k_all = jnp.concatenate([k_past, k], axis=1)
    v_all = jnp.concatenate([v_past, v], axis=1)
    lam = (jnp.exp(jnp.sum(lq1.astype(jnp.float32) * lk1.astype(jnp.float32)))
           - jnp.exp(jnp.sum(lq2.astype(jnp.float32) * lk2.astype(jnp.float32))) + lam_init)
    qb = min(Q_BLOCK, T)
    nb = T // qb
    q_blocks = jnp.moveaxis(q.reshape((b, nb, qb) + q.shape[2:]), 1, 0)
    pos_blocks = q_pos.reshape(nb, qb)
    k_chunk = k_pos // CHUNK

    def attend(args):
        qi, pi = args
        mask = k_chunk[None, :] <= (pi // CHUNK)[:, None]
        s = jnp.einsum('bqhrcd,bshcd->bhrcqs', qi, k_all, preferred_element_type=jnp.float32) * (A_HEAD_DIM ** -0.5)
        s = jnp.where(mask[None, None, None, None], s, -jnp.inf)
        p = jax.nn.softmax(s, axis=-1)
        w = p[:, :, :, 0] - lam * p[:, :, :, 1]
        return jnp.einsum('bhrqs,bshe->bqhre', w.astype(v_all.dtype), v_all)

    o = lax.map(attend, (q_blocks, pos_blocks))
    o = jnp.moveaxis(o, 0, 1).reshape(b, T, A_KV_HEADS, A_REP, 2 * A_HEAD_DIM)
    o = rms_norm(o, norm_w) * (1.0 - lam_init)
    o = o.reshape(b, T, A_WIDTH) * jax.nn.silu(za)
    return o, k, v


def mixer_layer(x, c, q_pos, k_pos, k_past, v_past, conv_state, ssm_state, p, lam_init):
    mod = jax.nn.silu(c) @ p['w_ada'] + p['b_ada']
    shift, scale, gate = jnp.split(mod, 3, axis=-1)
    h = rms_norm(x, p['norm_w']) * (1.0 + scale[:, None, :]) + shift[:, None, :]
    proj = h @ p['w_in']
    splits = np.cumsum(IN_SIZES)[:-1].tolist()
    z_m, xbc, dt, q, k, v, z_a, g_m, g_a = jnp.split(proj, splits, axis=-1)
    y_m, new_conv, new_ssm = mamba_branch(z_m, xbc, dt, conv_state, ssm_state, p['conv_w'], p['conv_b'],
                                          p['dt_bias'], p['a_log'], p['d_skip'], p['mamba_norm_w'])
    y_a, k_new, v_new = diff_attn_branch(q, k, v, z_a, k_past, v_past, q_pos, k_pos, p['lam_q1'], p['lam_k1'],
                                         p['lam_q2'], p['lam_k2'], p['attn_norm_w'], lam_init)
    merged = jax.nn.sigmoid(g_m) * (y_m @ p['w_proj_m']) + jax.nn.sigmoid(g_a) * (y_a @ p['w_proj_a'])
    x = x + gate[:, None, :] * (merged @ p['w_out'])
    return x, k_new, v_new, new_conv, new_ssm


def setup_inputs(seed: int = 0) -> dict:
    key = jax.random.key(seed)
    ks = jax.random.split(key, 32)

    def nrm(k, shape, s):
        return jax.random.normal(k, shape, jnp.float32) * s

    u = jax.random.uniform(ks[12], (DEPTH, M_HEADS), jnp.float32)
    dt0 = jnp.exp(u * (math.log(0.1) - math.log(0.001)) + math.log(0.001))
    return {
        'x_prompt': nrm(ks[0], (BATCH, SEQ, D_MODEL), 1.0),
        'x_sample': nrm(ks[1], (DEC_BATCH, DEC_SEQ, D_MODEL), 1.0),
        'cache_k': nrm(ks[2], (DEPTH, DEC_BATCH, PAST_LEN, A_KV_HEADS, 2, A_HEAD_DIM), 1.0),
        'cache_v': nrm(ks[3], (DEPTH, DEC_BATCH, PAST_LEN, A_KV_HEADS, 2 * A_HEAD_DIM), 1.0),
        'state_conv': nrm(ks[4], (DEPTH, DEC_BATCH, M_CONV - 1, M_CONV_DIM), 1.0),
        'state_ssm': nrm(ks[5], (DEPTH, DEC_BATCH, M_HEADS, M_HEAD_DIM, M_STATE), 0.5),
        'c_prompt': nrm(ks[6], (BATCH, D_MODEL), 1.0),
        'c_sample': nrm(ks[7], (DEC_BATCH, D_MODEL), 1.0),
        'w_ada': nrm(ks[8], (DEPTH, D_MODEL, 3 * D_MODEL), 0.5 * D_MODEL ** -0.5),
        'b_ada': nrm(ks[9], (DEPTH, 3 * D_MODEL), 0.01),
        'norm_w': 1.0 + nrm(ks[10], (DEPTH, D_MODEL), 0.02),
        'w_in': nrm(ks[11], (DEPTH, D_MODEL, IN_TOTAL), D_MODEL ** -0.5),
        'conv_w': nrm(ks[13], (DEPTH, M_CONV, M_CONV_DIM), M_CONV ** -0.5),
        'conv_b': nrm(ks[14], (DEPTH, M_CONV_DIM), 0.01),
        'dt_bias': dt0 + jnp.log(-jnp.expm1(-dt0)),
        'a_log': jnp.log(jax.random.uniform(ks[15], (DEPTH, M_HEADS), jnp.float32, 1.0, 16.0)),
        'd_skip': 1.0 + nrm(ks[16], (DEPTH, M_HEADS), 0.1),
        'mamba_norm_w': 1.0 + nrm(ks[17], (DEPTH, M_D_INNER), 0.02),
        'lam_q1': nrm(ks[18], (DEPTH, A_HEAD_DIM), 0.1),
        'lam_k1': nrm(ks[19], (DEPTH, A_HEAD_DIM), 0.1),
        'lam_q2': nrm(ks[20], (DEPTH, A_HEAD_DIM), 0.1),
        'lam_k2': nrm(ks[21], (DEPTH, A_HEAD_DIM), 0.1),
        'attn_norm_w': 1.0 + nrm(ks[22], (DEPTH, 2 * A_HEAD_DIM), 0.02),
        'w_proj_m': nrm(ks[23], (DEPTH, M_D_INNER, D_MODEL), M_D_INNER ** -0.5),
        'w_proj_a': nrm(ks[24], (DEPTH, A_WIDTH, D_MODEL), A_WIDTH ** -0.5),
        'w_out': nrm(ks[25], (DEPTH, D_MODEL, D_MODEL), D_MODEL ** -0.5),
        'final_norm_w': 1.0 + nrm(ks[26], (D_MODEL,), 0.02),
    }


def reference(x_prompt, x_sample, cache_k, cache_v, state_conv, state_ssm, c_prompt, c_sample,
              w_ada, b_ada, norm_w, w_in, conv_w, conv_b, dt_bias, a_log, d_skip, mamba_norm_w,
              lam_q1, lam_k1, lam_q2, lam_k2, attn_norm_w, w_proj_m, w_proj_a, w_out, final_norm_w):
    b, T = x_prompt.shape[0], x_prompt.shape[1]
    ts = x_sample.shape[1]
    past = cache_k.shape[2]
    pos_p = jnp.arange(T, dtype=jnp.int32)
    pos_s = past + jnp.arange(ts, dtype=jnp.int32)
    kpos_s = jnp.arange(past + ts, dtype=jnp.int32)
    dt_ = x_prompt.dtype
    empty_k = jnp.zeros((b, 0, A_KV_HEADS, 2, A_HEAD_DIM), dt_)
    empty_v = jnp.zeros((b, 0, A_KV_HEADS, 2 * A_HEAD_DIM), dt_)
    zero_conv = jnp.zeros((b, M_CONV - 1, M_CONV_DIM), dt_)
    zero_ssm = jnp.zeros((b, M_HEADS, M_HEAD_DIM, M_STATE), dt_)
    hp, hs = x_prompt, x_sample
    k_p, v_p, conv_p, ssm_p = [], [], [], []
    k_s, v_s, conv_s, ssm_s = [], [], [], []
    for i in range(DEPTH):
        p = {'w_ada': w_ada[i], 'b_ada': b_ada[i], 'norm_w': norm_w[i], 'w_in': w_in[i],
             'conv_w': conv_w[i], 'conv_b': conv_b[i], 'dt_bias': dt_bias[i], 'a_log': a_log[i],
             'd_skip': d_skip[i], 'mamba_norm_w': mamba_norm_w[i], 'lam_q1': lam_q1[i], 'lam_k1': lam_k1[i],
             'lam_q2': lam_q2[i], 'lam_k2': lam_k2[i], 'attn_norm_w': attn_norm_w[i],
             'w_proj_m': w_proj_m[i], 'w_proj_a': w_proj_a[i], 'w_out': w_out[i]}
        li = lambda_init(i)
        hp, kp, vp, cp, sp = mixer_layer(hp, c_prompt, pos_p, pos_p, empty_k, empty_v, zero_conv, zero_ssm, p, li)
        hs, kq, vq, cq, sq = mixer_layer(hs, c_sample, pos_s, kpos_s, cache_k[i], cache_v[i],
                                         state_conv[i], state_ssm[i], p, li)
        k_p.append(kp); v_p.append(vp); conv_p.append(cp); ssm_p.append(sp)
        k_s.append(kq); v_s.append(vq); conv_s.append(cq); ssm_s.append(sq)
    y_prompt = rms_norm(hp, final_norm_w)
    y_sample = rms_norm(hs, final_norm_w)
    return (y_prompt, y_sample, jnp.stack(k_p), jnp.stack(v_p), jnp.stack(conv_p), jnp.stack(ssm_p),
            jnp.stack(k_s), jnp.stack(v_s), jnp.stack(conv_s), jnp.stack(ssm_s))
```

```python
import functools
import math

import jax
import jax.numpy as jnp
from jax import lax
from jax.experimental import pallas as pl
from jax.experimental.pallas import tpu as pltpu

F32 = jnp.float32
BF16 = jnp.bfloat16

CHUNK = 64
EPS = 1e-6
M_HEAD_DIM = 64
M_HPG = 16
M_STATE = 128
M_CONV = 4
GROUP_W = M_HPG * M_HEAD_DIM
A_HEAD_DIM = 64
A_REP = 4
KV_W = 2 * A_HEAD_DIM
QH_W = A_REP * KV_W
ROPE_DIM = 16
ROPE_THETA = 500000.0
LANES = 128
VMEM_LIMIT_BYTES = 56 * 1024 * 1024


def _pick(n, target, mult=8):
    if n <= target:
        return n
    for t in range(target, 0, -1):
        if n % t == 0 and t % mult == 0:
            return t
    return n


def _params(*sem):
    return pltpu.CompilerParams(dimension_semantics=sem, vmem_limit_bytes=VMEM_LIMIT_BYTES)


def _silu(x):
    return x * jax.nn.sigmoid(x)


def _split_hi_lo(x):
    hi = x.astype(BF16)
    lo = (x - hi.astype(F32)).astype(BF16)
    return hi, lo


def _dot(a, b):
    return jnp.dot(a, b, preferred_element_type=F32)


def _dot_nt(a, b):
    return lax.dot_general(a, b, (((1,), (1,)), ((), ())), preferred_element_type=F32)


def _dot_tn(a, b):
    return lax.dot_general(a, b, (((0,), (0,)), ((), ())), preferred_element_type=F32)


def _ada_kernel(c_ref, w_ref, b_ref, o_ref):
    a = _silu(c_ref[...]).astype(BF16)
    o_ref[...] = _dot(a, w_ref[...].astype(BF16)) + b_ref[...]


def _ada_mod(c, w_ada, b_ada):
    m, d = c.shape
    n = w_ada.shape[1]
    tn = _pick(n, 512, LANES)
    return pl.pallas_call(
        _ada_kernel,
        grid=(n // tn,),
        in_specs=[pl.BlockSpec((m, d), lambda j: (0, 0)),
                  pl.BlockSpec((d, tn), lambda j: (0, j)),
                  pl.BlockSpec((1, tn), lambda j: (0, j))],
        out_specs=pl.BlockSpec((m, tn), lambda j: (0, j)),
        out_shape=jax.ShapeDtypeStruct((m, n), F32),
        compiler_params=_params("arbitrary"),
        name="ada_mod",
    )(c, w_ada, b_ada.reshape(1, n))


def _prenorm_kernel(x_ref, nw_ref, sc_ref, sh_ref, o_ref):
    x = x_ref[0]
    ms = jnp.mean(x * x, axis=-1, keepdims=True)
    y = x * lax.rsqrt(ms + EPS) * nw_ref[...]
    o_ref[0] = (y * (1.0 + sc_ref[0]) + sh_ref[0]).astype(o_ref.dtype)


def _prenorm(x, norm_w, scale, shift):
    b, t, d = x.shape
    tr = _pick(t, 256)
    return pl.pallas_call(
        _prenorm_kernel,
        grid=(b, t // tr),
        in_specs=[pl.BlockSpec((1, tr, d), lambda i, j: (i, j, 0)),
                  pl.BlockSpec((1, d), lambda i, j: (0, 0)),
                  pl.BlockSpec((1, 1, d), lambda i, j: (i, 0, 0)),
                  pl.BlockSpec((1, 1, d), lambda i, j: (i, 0, 0))],
        out_specs=pl.BlockSpec((1, tr, d), lambda i, j: (i, j, 0)),
        out_shape=jax.ShapeDtypeStruct((b, t, d), BF16),
        compiler_params=_params("arbitrary", "arbitrary"),
        name="prenorm",
    )(x, norm_w.reshape(1, d), scale.reshape(b, 1, d), shift.reshape(b, 1, d))


def _mm_kernel(a_ref, w_ref, o_ref):
    o_ref[...] = _dot(a_ref[...], w_ref[...]).astype(o_ref.dtype)


def _matmul(a, w, name):
    m, k = a.shape
    n = w.shape[1]
    tm = _pick(m, 512)
    tn = _pick(n, 1024, LANES)
    return pl.pallas_call(
        _mm_kernel,
        grid=(n // tn, m // tm),
        in_specs=[pl.BlockSpec((tm, k), lambda j, i: (i, 0)),
                  pl.BlockSpec((k, tn), lambda j, i: (0, j))],
        out_specs=pl.BlockSpec((tm, tn), lambda j, i: (i, j)),
        out_shape=jax.ShapeDtypeStruct((m, n), F32),
        compiler_params=_params("arbitrary", "arbitrary"),
        name=name,
    )(a, w)


def _dt_kernel(a_ref, w_ref, b_ref, o_ref):
    x = _dot(a_ref[...], w_ref[...]) + b_ref[...]
    o_ref[...] = jnp.maximum(x, 0.0) + jnp.log1p(jnp.exp(-jnp.abs(x)))


def _dt_proj(a, w_dt, dt_bias):
    m, k = a.shape
    n = w_dt.shape[1]
    tm = _pick(m, 512)
    return pl.pallas_call(
        _dt_kernel,
        grid=(m // tm,),
        in_specs=[pl.BlockSpec((tm, k), lambda i: (i, 0)),
                  pl.BlockSpec((k, n), lambda i: (0, 0)),
                  pl.BlockSpec((1, n), lambda i: (0, 0))],
        out_specs=pl.BlockSpec((tm, n), lambda i: (i, 0)),
        out_shape=jax.ShapeDtypeStruct((m, n), F32),
        compiler_params=_params("arbitrary"),
        name="dt_proj",
    )(a, w_dt, dt_bias.reshape(1, n))


def _rope_slab(x, cos, sin_up, sin_dn):
    return x * cos + pltpu.roll(x, LANES - ROPE_DIM // 2, 1) * sin_up + pltpu.roll(x, ROPE_DIM // 2, 1) * sin_dn


def _rope_kernel(q_ref, k_ref, v_ref, cos_ref, sup_ref, sdn_ref, qo_ref, ko_ref, vo_ref, *, q_scale):
    cos, sup, sdn = cos_ref[...], sup_ref[...], sdn_ref[...]
    for s in range(q_ref.shape[2] // LANES):
        sl = slice(s * LANES, (s + 1) * LANES)
        qo_ref[0, :, sl] = (_rope_slab(q_ref[0, :, sl], cos, sup, sdn) * q_scale).astype(qo_ref.dtype)
    for s in range(k_ref.shape[2] // LANES):
        sl = slice(s * LANES, (s + 1) * LANES)
        ko_ref[0, :, sl] = _rope_slab(k_ref[0, :, sl], cos, sup, sdn)
    vo_ref[...] = v_ref[...]


def _rope_tables(pos):
    half = ROPE_DIM // 2
    inv = ROPE_THETA ** (-jnp.arange(half, dtype=F32) / half)
    ang = pos.astype(F32)[:, None] * inv[None, :]
    cos, sin = jnp.cos(ang), jnp.sin(ang)
    t = pos.shape[0]
    ones = jnp.ones((t, A_HEAD_DIM - ROPE_DIM), F32)
    zeros = jnp.zeros((t, A_HEAD_DIM - ROPE_DIM), F32)
    zh = jnp.zeros((t, half), F32)
    cos64 = jnp.concatenate([cos, cos, ones], axis=1)
    sup64 = jnp.concatenate([-sin, zh, zeros], axis=1)
    sdn64 = jnp.concatenate([zh, sin, zeros], axis=1)
    rep = lambda a: jnp.concatenate([a, a], axis=1)
    return rep(cos64), rep(sup64), rep(sdn64)


def _rope_kv(proj, pos, d, kvw, k_blk, v_blk):
    b, t, _ = proj.shape
    tr = _pick(t, 256)
    cos, sup, sdn = _rope_tables(pos)
    tab = pl.BlockSpec((tr, LANES), lambda i, j: (j, 0))
    return pl.pallas_call(
        functools.partial(_rope_kernel, q_scale=A_HEAD_DIM ** -0.5),
        grid=(b, t // tr),
        in_specs=[pl.BlockSpec((1, tr, d), lambda i, j: (i, j, 0)),
                  pl.BlockSpec((1, tr, kvw), lambda i, j: (i, j, k_blk)),
                  pl.BlockSpec((1, tr, kvw), lambda i, j: (i, j, v_blk)),
                  tab, tab, tab],
        out_specs=[pl.BlockSpec((1, tr, d), lambda i, j: (i, j, 0)),
                   pl.BlockSpec((1, tr, kvw), lambda i, j: (i, j, 0)),
                   pl.BlockSpec((1, tr, kvw), lambda i, j: (i, j, 0))],
        out_shape=[jax.ShapeDtypeStruct((b, t, d), BF16),
                   jax.ShapeDtypeStruct((b, t, kvw), F32),
                   jax.ShapeDtypeStruct((b, t, kvw), F32)],
        compiler_params=_params("arbitrary", "arbitrary"),
        name="rope_kv",
    )(proj, proj, proj, cos, sup, sdn)


def _ssd_kernel(*refs, tl, has_state):
    (alog_ref, alr_ref, dsk_ref, mnw_ref, cwx_ref, cwb_ref, cwc_ref, cbx_ref, cbb_ref, cbc_ref,
     z_ref, x_ref, bm_ref, cm_ref, dt_ref, dtr_ref) = refs[:16]
    if has_state:
        csx_ref, csb_ref, csc_ref, s0_ref = refs[16:20]
        rest = refs[20:]
    else:
        rest = refs[16:]
    y_ref, sout_ref, xbuf, bbuf, cbuf, st_scr = rest
    g = pl.program_id(1)
    c = pl.program_id(2)
    L = CHUNK
    nsub = tl // L
    halo = M_CONV - 1
    base = 8

    @pl.when(c == 0)
    def _():
        if has_state:
            xbuf[base - halo:base, :] = csx_ref[0]
            bbuf[base - halo:base, :] = csb_ref[0]
            cbuf[base - halo:base, :] = csc_ref[0]
            st_scr[...] = s0_ref[0].reshape(GROUP_W, M_STATE).T
        else:
            xbuf[0:base, :] = jnp.zeros((base, GROUP_W), F32)
            bbuf[0:base, :] = jnp.zeros((base, M_STATE), F32)
            cbuf[0:base, :] = jnp.zeros((base, M_STATE), F32)
            st_scr[...] = jnp.zeros_like(st_scr)

    xbuf[base:base + tl, :] = x_ref[0]
    bbuf[base:base + tl, :] = bm_ref[0]
    cbuf[base:base + tl, :] = cm_ref[0]

    r64 = lax.broadcasted_iota(jnp.int32, (L, L), 0)
    c64 = lax.broadcasted_iota(jnp.int32, (L, L), 1)
    tril = (c64 <= r64).astype(BF16)
    heads = dt_ref.shape[2]
    hrow = lax.broadcasted_iota(jnp.int32, (heads, GROUP_W), 0)
    hcol = lax.broadcasted_iota(jnp.int32, (heads, GROUP_W), 1)
    expand = (hrow == g * M_HPG + hcol // M_HEAD_DIM).astype(BF16)
    br = lax.broadcasted_iota(jnp.int32, (LANES, LANES), 0)
    bc = lax.broadcasted_iota(jnp.int32, (LANES, LANES), 1)
    same_half = (br // L) == (bc // L)
    triu2 = (same_half & ((br % L) <= (bc % L))).astype(BF16)
    pr = lax.broadcasted_iota(jnp.int32, (L, LANES), 0)
    pc = lax.broadcasted_iota(jnp.int32, (L, LANES), 1)
    causal2 = (pc % L) <= pr

    a_col = -jnp.exp(alog_ref[...])
    a_row = -jnp.exp(alr_ref[0])
    n_rows = dtr_ref.shape[2]
    da_r = dtr_ref[0, 0] * jnp.concatenate([a_row] * (n_rows // 8), axis=0)
    hi, lo = _split_hi_lo(da_r)
    acum_r_all = _dot(hi, triu2) + _dot(lo, triu2)

    def conv(buf, w_ref, b_ref, i):
        acc = b_ref[...] + w_ref[M_CONV - 1:M_CONV, :] * buf[base + i * L:base + (i + 1) * L, :]
        for k in range(M_CONV - 1):
            off = base - halo + k + i * L
            acc = acc + w_ref[k:k + 1, :] * buf[off:off + L, :]
        return _silu(acc)

    for i in range(nsub):
        rows = slice(i * L, (i + 1) * L)
        x = conv(xbuf, cwx_ref, cbx_ref, i)
        bm = conv(bbuf, cwb_ref, cbb_ref, i).astype(BF16)
        cm = conv(cbuf, cwc_ref, cbc_ref, i).astype(BF16)
        dt = dt_ref[0, rows, :]
        da = dt * a_col
        hi, lo = _split_hi_lo(da)
        acum = _dot(tril, hi) + _dot(tril, lo)
        hi, lo = _split_hi_lo(acum)
        acum_x = _dot(hi, expand) + _dot(lo, expand)
        hi, lo = _split_hi_lo(dt)
        dt_x = _dot(hi, expand) + _dot(lo, expand)
        xdt = x * dt_x
        xdt_b = xdt.astype(BF16)
        alast_x = acum_x[L - 1:L, :]
        cb2 = _dot_nt(cm, jnp.concatenate([bm, bm], axis=0))
        st_b = st_scr[...].astype(BF16)
        y_off = _dot(cm, st_b) * jnp.exp(acum_x)
        y_parts = []
        for jj in range(GROUP_W // LANES):
            ls = slice(jj * LANES, (jj + 1) * LANES)
            seg = acum_x[:, ls] - acum_r_all[i * 8 + jj:i * 8 + jj + 1, :]
            m2 = (jnp.where(causal2, jnp.exp(seg), 0.0) * cb2).astype(BF16)
            x2 = xdt_b[:, ls]
            rhs = jnp.where(same_half, jnp.concatenate([x2, x2], axis=0), jnp.zeros((), BF16))
            y_parts.append(_dot(m2, rhs))
        y = jnp.concatenate(y_parts, axis=1) + y_off + dsk_ref[0] * x
        y = y * _silu(z_ref[0, rows, :])
        ms = jnp.mean(y * y, axis=-1, keepdims=True)
        y_ref[0, rows, :] = (y * lax.rsqrt(ms + EPS) * mnw_ref[0]).astype(y_ref.dtype)
        xw = (xdt * jnp.exp(alast_x - acum_x)).astype(BF16)
        st_scr[...] = st_scr[...] * jnp.exp(alast_x) + _dot_tn(bm, xw)

    tx = xbuf[base + tl - halo:base + tl, :]
    tb = bbuf[base + tl - halo:base + tl, :]
    tc = cbuf[base + tl - halo:base + tl, :]
    xbuf[base - halo:base, :] = tx
    bbuf[base - halo:base, :] = tb
    cbuf[base - halo:base, :] = tc

    @pl.when(c == pl.num_programs(2) - 1)
    def _():
        sout_ref[0] = st_scr[...].T.reshape(M_HPG, M_HEAD_DIM, M_STATE)


def _ssd(proj, dt, conv_state, ssm_state, conv_w, conv_b, a_log, d_skip, mnorm_w, z_off, xbc_off, d_inner):
    b, t, _ = proj.shape
    groups = d_inner // GROUP_W
    heads = groups * M_HPG
    bcw = groups * M_STATE
    has_state = conv_state is not None
    tl = _pick(t, 256, CHUNK)
    nsub = tl // CHUNK
    nc = t // tl
    dtr = dt.reshape(b, t // CHUNK, CHUNK, groups, M_HPG // 2, 2).transpose(0, 3, 1, 4, 5, 2)
    dtr = dtr.reshape(b, groups, (t // CHUNK) * 8, LANES)
    n_rows = nsub * 8
    if n_rows < 16:
        dtr = jnp.concatenate([dtr, jnp.zeros_like(dtr)], axis=2)
        n_rows = 16
    alr = jnp.repeat(a_log.reshape(groups, M_HPG // 2, 2), CHUNK, axis=2)
    dsk = jnp.repeat(d_skip.reshape(groups, 1, M_HPG), M_HEAD_DIM, axis=2)
    mnw = mnorm_w.reshape(groups, 1, GROUP_W)
    xb, bb, cb = xbc_off // GROUP_W, (xbc_off + d_inner) // M_STATE, (xbc_off + d_inner + bcw) // M_STATE
    cxb, cbb, ccb = 0, d_inner // M_STATE, (d_inner + bcw) // M_STATE
    zb = z_off // GROUP_W
    conv_b2 = conv_b.reshape(1, -1)
    gmap = lambda blk: (lambda i, g, c: (0, blk + g))
    in_specs = [
        pl.BlockSpec((1, heads), lambda i, g, c: (0, 0)),
        pl.BlockSpec((1, 8, LANES), lambda i, g, c: (g, 0, 0)),
        pl.BlockSpec((1, 1, GROUP_W), lambda i, g, c: (g, 0, 0)),
        pl.BlockSpec((1, 1, GROUP_W), lambda i, g, c: (g, 0, 0)),
        pl.BlockSpec((M_CONV, GROUP_W), gmap(cxb)),
        pl.BlockSpec((M_CONV, M_STATE), gmap(cbb)),
        pl.BlockSpec((M_CONV, M_STATE), gmap(ccb)),
        pl.BlockSpec((1, GROUP_W), gmap(cxb)),
        pl.BlockSpec((1, M_STATE), gmap(cbb)),
        pl.BlockSpec((1, M_STATE), gmap(ccb)),
        pl.BlockSpec((1, tl, GROUP_W), lambda i, g, c: (i, c, zb + g)),
        pl.BlockSpec((1, tl, GROUP_W), lambda i, g, c: (i, c, xb + g)),
        pl.BlockSpec((1, tl, M_STATE), lambda i, g, c: (i, c, bb + g)),
        pl.BlockSpec((1, tl, M_STATE), lambda i, g, c: (i, c, cb + g)),
        pl.BlockSpec((1, tl, heads), lambda i, g, c: (i, c, 0)),
        pl.BlockSpec((1, 1, n_rows, LANES), lambda i, g, c: (i, g, c, 0)),
    ]
    args = [a_log.reshape(1, heads), alr, dsk, mnw, conv_w, conv_w, conv_w, conv_b2, conv_b2, conv_b2,
            proj, proj, proj, proj, dt.reshape(b, t, heads), dtr]
    if has_state:
        halo = M_CONV - 1
        in_specs += [
            pl.BlockSpec((1, halo, GROUP_W), lambda i, g, c: (i, 0, cxb + g)),
            pl.BlockSpec((1, halo, M_STATE), lambda i, g, c: (i, 0, cbb + g)),
            pl.BlockSpec((1, halo, M_STATE), lambda i, g, c: (i, 0, ccb + g)),
            pl.BlockSpec((1, M_HPG, M_HEAD_DIM, M_STATE), lambda i, g, c: (i, g, 0, 0)),
        ]
        args += [conv_state, conv_state, conv_state, ssm_state]
    return pl.pallas_call(
        functools.partial(_ssd_kernel, tl=tl, has_state=has_state),
        grid=(b, groups, nc),
        in_specs=in_specs,
        out_specs=[pl.BlockSpec((1, tl, GROUP_W), lambda i, g, c: (i, c, g)),
                   pl.BlockSpec((1, M_HPG, M_HEAD_DIM, M_STATE), lambda i, g, c: (i, g, 0, 0))],
        out_shape=[jax.ShapeDtypeStruct((b, t, d_inner), BF16),
                   jax.ShapeDtypeStruct((b, heads, M_HEAD_DIM, M_STATE), F32)],
        scratch_shapes=[pltpu.VMEM((tl + 8, GROUP_W), F32),
                        pltpu.VMEM((tl + 8, M_STATE), F32),
                        pltpu.VMEM((tl + 8, M_STATE), F32),
                        pltpu.VMEM((M_STATE, GROUP_W), F32)],
        compiler_params=_params("arbitrary", "arbitrary", "arbitrary"),
        name="ssd",
    )(*args)


def _attn_kernel(*refs, tq, tk, t_new, past, tkp, pos0, lam_init):
    lq1_ref, lk1_ref, lq2_ref, lk2_ref, nw_ref, q_ref, k_ref, v_ref, za_ref = refs[:9]
    if past:
        kp_ref, vp_ref = refs[9:11]
        rest = refs[11:]
    else:
        rest = refs[9:]
    o_ref, qz_scr, kb_scr, vb_scr, m_scr, l_scr, acc_scr = rest
    qi = pl.program_id(2)
    rows = A_REP * tq

    @pl.when(qi == 0)
    def _():
        step = _pick(t_new, 512)

        def body(i, carry):
            sl = pl.ds(pl.multiple_of(i * step, step), step)
            kb_scr[sl, :] = k_ref[0, sl, :].astype(BF16)
            vb_scr[sl, :] = v_ref[0, sl, :].astype(BF16)
            return carry

        lax.fori_loop(0, t_new // step, body, 0)

    lane = lax.broadcasted_iota(jnp.int32, (tq, KV_W), 1)
    for r in range(A_REP):
        q = q_ref[0, :, r * KV_W:(r + 1) * KV_W]
        zero = jnp.zeros((), BF16)
        qz_scr[0, r * tq:(r + 1) * tq, :] = jnp.where(lane < A_HEAD_DIM, q, zero)
        qz_scr[1, r * tq:(r + 1) * tq, :] = jnp.where(lane >= A_HEAD_DIM, q, zero)
    m_scr[...] = jnp.full(m_scr.shape, -jnp.inf, F32)
    l_scr[...] = jnp.zeros(l_scr.shape, F32)
    acc_scr[...] = jnp.zeros(acc_scr.shape, F32)

    def tile(k, v, mask):
        for comp in range(2):
            s = _dot_nt(qz_scr[comp], k)
            if mask is not None:
                s = jnp.where(mask, s, -jnp.inf)
            m_old = m_scr[comp]
            m_new = jnp.maximum(m_old, jnp.max(s, axis=-1, keepdims=True))
            p = jnp.exp(s - m_new)
            alpha = jnp.exp(m_old - m_new)
            l_scr[comp] = alpha * l_scr[comp] + jnp.sum(p, axis=-1, keepdims=True)
            acc_scr[comp] = alpha * acc_scr[comp] + _dot(p.astype(BF16), v)
            m_scr[comp] = m_new

    def chunk_mask(nkeys, q_pos0, k_pos0):
        qpos = q_pos0 + lax.broadcasted_iota(jnp.int32, (rows, nkeys), 0) % tq
        kpos = k_pos0 + lax.broadcasted_iota(jnp.int32, (rows, nkeys), 1)
        return (kpos // CHUNK) <= (qpos // CHUNK)

    if past:
        def pbody(j, carry):
            sl = pl.ds(pl.multiple_of(j * tkp, tkp), tkp)
            tile(kp_ref[0, sl, :].astype(BF16), vp_ref[0, sl, :].astype(BF16), None)
            return carry

        lax.fori_loop(0, past // tkp, pbody, 0)

    def nbody(j, carry):
        sl = pl.ds(pl.multiple_of(j * tk, tk), tk)
        tile(kb_scr[sl, :], vb_scr[sl, :], None)
        return carry

    n_full = (qi * tq) // tk
    lax.fori_loop(0, n_full, nbody, 0)
    n_diag = tq // tk
    for d in range(n_diag):
        k0 = qi * tq + d * tk
        sl = pl.ds(pl.multiple_of(k0, tk), tk)
        tile(kb_scr[sl, :], vb_scr[sl, :], chunk_mask(tk, pos0 + qi * tq, pos0 + k0))

    lam = (jnp.exp(jnp.sum(lq1_ref[...] * lk1_ref[...], axis=-1, keepdims=True))
           - jnp.exp(jnp.sum(lq2_ref[...] * lk2_ref[...], axis=-1, keepdims=True)) + lam_init)
    o = acc_scr[0] / l_scr[0] - lam * (acc_scr[1] / l_scr[1])
    ms = jnp.mean(o * o, axis=-1, keepdims=True)
    on = o * lax.rsqrt(ms + EPS) * nw_ref[...] * (1.0 - lam_init)
    for r in range(A_REP):
        sl = slice(r * KV_W, (r + 1) * KV_W)
        o_ref[0, :, sl] = (on[r * tq:(r + 1) * tq, :] * _silu(za_ref[0, :, sl])).astype(o_ref.dtype)


def _attention(qr, k_new, v_new, proj, za_off, k_past, v_past, lam_vecs, norm_w, lam_init, pos0):
    b, t, d = qr.shape
    kvh = d // QH_W
    past = 0 if k_past is None else k_past.shape[1]
    tq = _pick(t, 256, CHUNK)
    tk = tq
    tkp = _pick(past, 512, CHUNK) if past else 0
    zb = za_off // QH_W
    rows = A_REP * tq
    vec = pl.BlockSpec((1, A_HEAD_DIM), lambda i, h, j: (0, 0))
    in_specs = [vec, vec, vec, vec,
                pl.BlockSpec((1, KV_W), lambda i, h, j: (0, 0)),
                pl.BlockSpec((1, tq, QH_W), lambda i, h, j: (i, j, h)),
                pl.BlockSpec((1, t, KV_W), lambda i, h, j: (i, 0, h)),
                pl.BlockSpec((1, t, KV_W), lambda i, h, j: (i, 0, h)),
                pl.BlockSpec((1, tq, QH_W), lambda i, h, j: (i, j, zb + h))]
    args = [v.reshape(1, A_HEAD_DIM) for v in lam_vecs] + [norm_w.reshape(1, KV_W), qr, k_new, v_new, proj]
    if past:
        in_specs += [pl.BlockSpec((1, past, KV_W), lambda i, h, j: (i, 0, h)),
                     pl.BlockSpec((1, past, KV_W), lambda i, h, j: (i, 0, h))]
        args += [k_past, v_past]
    return pl.pallas_call(
        functools.partial(_attn_kernel, tq=tq, tk=tk, t_new=t, past=past, tkp=tkp, pos0=pos0, lam_init=lam_init),
        grid=(b, kvh, t // tq),
        in_specs=in_specs,
        out_specs=pl.BlockSpec((1, tq, QH_W), lambda i, h, j: (i, j, h)),
        out_shape=jax.ShapeDtypeStruct((b, t, d), BF16),
        scratch_shapes=[pltpu.VMEM((2, rows, KV_W), BF16),
                        pltpu.VMEM((t, KV_W), BF16),
                        pltpu.VMEM((t, KV_W), BF16),
                        pltpu.VMEM((2, rows, 1), F32),
                        pltpu.VMEM((2, rows, 1), F32),
                        pltpu.VMEM((2, rows, KV_W), F32)],
        compiler_params=_params("arbitrary", "arbitrary", "arbitrary"),
        name="diff_attn",
    )(*args)


def _merge_kernel(ym_ref, ya_ref, wm_ref, wa_ref, gm_ref, ga_ref, o_ref):
    pm = _dot(ym_ref[...], wm_ref[...])
    pa = _dot(ya_ref[...], wa_ref[...])
    o_ref[...] = (jax.nn.sigmoid(gm_ref[...]) * pm + jax.nn.sigmoid(ga_ref[...]) * pa).astype(o_ref.dtype)


def _merge(y_m, y_a, w_pm, w_pa, proj2d, gm_off, ga_off):
    m, km = y_m.shape
    ka = y_a.shape[1]
    d = w_pm.shape[1]
    tm = _pick(m, 256)
    tn = _pick(d, 512, LANES)
    gmb, gab = gm_off // tn, ga_off // tn
    return pl.pallas_call(
        _merge_kernel,
        grid=(d // tn, m // tm),
        in_specs=[pl.BlockSpec((tm, km), lambda j, i: (i, 0)),
                  pl.BlockSpec((tm, ka), lambda j, i: (i, 0)),
                  pl.BlockSpec((km, tn), lambda j, i: (0, j)),
                  pl.BlockSpec((ka, tn), lambda j, i: (0, j)),
                  pl.BlockSpec((tm, tn), lambda j, i: (i, gmb + j)),
                  pl.BlockSpec((tm, tn), lambda j, i: (i, gab + j))],
        out_specs=pl.BlockSpec((tm, tn), lambda j, i: (i, j)),
        out_shape=jax.ShapeDtypeStruct((m, d), BF16),
        compiler_params=_params("arbitrary", "arbitrary"),
        name="merge",
    )(y_m, y_a, w_pm, w_pa, proj2d, proj2d)


def _out_kernel(mg_ref, w_ref, x_ref, gate_ref, fw_ref, o_ref, *, tn):
    j = pl.program_id(2)
    cols = pl.ds(pl.multiple_of(j * tn, tn), tn)
    o_ref[0, :, cols] = x_ref[0] + gate_ref[0] * _dot(mg_ref[0], w_ref[...])

    @pl.when(j == pl.num_programs(2) - 1)
    def _():
        r = o_ref[0]
        ms = jnp.mean(r * r, axis=-1, keepdims=True)
        o_ref[0] = r * lax.rsqrt(ms + EPS) * fw_ref[...]


def _out_proj(merged, w_out, x, gate, final_w):
    b, t, d = x.shape
    tm = _pick(t, 512)
    tn = _pick(d, 512, LANES)
    return pl.pallas_call(
        functools.partial(_out_kernel, tn=tn),
        grid=(b, t // tm, d // tn),
        in_specs=[pl.BlockSpec((1, tm, d), lambda i, r, j: (i, r, 0)),
                  pl.BlockSpec((d, tn), lambda i, r, j: (0, j)),
                  pl.BlockSpec((1, tm, tn), lambda i, r, j: (i, r, j)),
                  pl.BlockSpec((1, 1, tn), lambda i, r, j: (i, 0, j)),
                  pl.BlockSpec((1, d), lambda i, r, j: (0, 0))],
        out_specs=pl.BlockSpec((1, tm, d), lambda i, r, j: (i, r, 0)),
        out_shape=jax.ShapeDtypeStruct((b, t, d), F32),
        compiler_params=_params("arbitrary", "arbitrary", "arbitrary"),
        name="out_proj",
    )(merged.reshape(b, t, d), w_out, x, gate.reshape(b, 1, d), final_w.reshape(1, d))


def _layer_path(x, mod, k_past, v_past, conv_state, ssm_state, pos0, w, lam_init, final_w):
    b, t, d = x.shape
    d_inner = 2 * d
    groups = d_inner // GROUP_W
    bcw = groups * M_STATE
    conv_dim = d_inner + 2 * bcw
    kvw = (d // QH_W) * KV_W
    za_off, gm_off, ga_off, z_off, xbc_off = d, 2 * d, 3 * d, 4 * d, 6 * d
    k_off = xbc_off + conv_dim
    v_off = k_off + kvw
    np_ = v_off + kvw
    shift, scale, gate = mod[:, :d], mod[:, d:2 * d], mod[:, 2 * d:]

    h = _prenorm(x, w['norm_w'], scale, shift).reshape(b * t, d)
    proj = _matmul(h, w['w_main'], "in_proj").reshape(b, t, np_)
    dt = _dt_proj(h, w['w_dt'], w['dt_bias']).reshape(b, t, -1)

    pos = pos0 + jnp.arange(t, dtype=jnp.int32)
    qr, k_new, v_new = _rope_kv(proj, pos, d, kvw, k_off // kvw, v_off // kvw)

    y_m, ssm_new = _ssd(proj, dt, conv_state, ssm_state, w['conv_w'], w['conv_b'], w['a_log'], w['d_skip'],
                        w['mamba_norm_w'], z_off, xbc_off, d_inner)
    halo = M_CONV - 1
    conv_new = proj[:, t - halo:, xbc_off:xbc_off + conv_dim]

    y_a = _attention(qr, k_new, v_new, proj, za_off, k_past, v_past,
                     (w['lam_q1'], w['lam_k1'], w['lam_q2'], w['lam_k2']), w['attn_norm_w'], lam_init, pos0)

    merged = _merge(y_m.reshape(b * t, d_inner), y_a.reshape(b * t, d), w['w_proj_m'], w['w_proj_a'],
                    proj.reshape(b * t, np_), gm_off, ga_off)
    y = _out_proj(merged, w['w_out'], x, gate, final_w)
    kvh = d // QH_W
    return (y, k_new.reshape(b, t, kvh, 2, A_HEAD_DIM), v_new.reshape(b, t, kvh, KV_W), conv_new, ssm_new)


def kernel(x_prompt, x_sample, cache_k, cache_v, state_conv, state_ssm, c_prompt, c_sample,
           w_ada, b_ada, norm_w, w_in, conv_w, conv_b, dt_bias, a_log, d_skip, mamba_norm_w,
           lam_q1, lam_k1, lam_q2, lam_k2, attn_norm_w, w_proj_m, w_proj_a, w_out, final_norm_w):
    depth = w_in.shape[0]
    assert depth == 1, "the final norm is fused into the single layer's output projection"
    bp, d = c_prompt.shape
    bs = c_sample.shape[0]
    past = cache_k.shape[2]
    d_inner = 2 * d
    groups = d_inner // GROUP_W
    heads = groups * M_HPG
    conv_dim = d_inner + 2 * groups * M_STATE
    kvw = (d // QH_W) * KV_W
    sizes = (d_inner, conv_dim, heads, d, kvw, kvw, d, d, d)
    offs = [0]
    for s in sizes:
        offs.append(offs[-1] + s)
    seg = lambda wi, n: wi[:, offs[n]:offs[n + 1]]

    i = 0
    wi = w_in[i]
    w = {
        'w_main': jnp.concatenate([seg(wi, 3), seg(wi, 6), seg(wi, 7), seg(wi, 8), seg(wi, 0), seg(wi, 1),
                                   seg(wi, 4), seg(wi, 5)], axis=1).astype(BF16),
        'w_dt': seg(wi, 2).astype(BF16),
        'norm_w': norm_w[i], 'conv_w': conv_w[i], 'conv_b': conv_b[i], 'dt_bias': dt_bias[i], 'a_log': a_log[i],
        'd_skip': d_skip[i], 'mamba_norm_w': mamba_norm_w[i], 'lam_q1': lam_q1[i], 'lam_k1': lam_k1[i],
        'lam_q2': lam_q2[i], 'lam_k2': lam_k2[i], 'attn_norm_w': attn_norm_w[i],
        'w_proj_m': w_proj_m[i].astype(BF16), 'w_proj_a': w_proj_a[i].astype(BF16), 'w_out': w_out[i].astype(BF16),
    }
    lam_init = 0.8 - 0.6 * math.exp(-0.3 * i)
    pad_rows = -(bp + bs) % 16
    c_all = jnp.concatenate([c_prompt, c_sample, jnp.zeros((pad_rows, d), F32)], axis=0)
    mod = _ada_mod(c_all, w_ada[i], b_ada[i])

    yp, kp, vp, cp, sp = _layer_path(x_prompt, mod[:bp], None, None, None, None, 0, w, lam_init, final_norm_w)
    ck = cache_k[i].reshape(bs, past, kvw)
    cv = cache_v[i].reshape(bs, past, kvw)
    ys, kq, vq, cq, sq = _layer_path(x_sample, mod[bp:bp + bs], ck, cv, state_conv[i], state_ssm[i], past, w, lam_init,
                                     final_norm_w)
    st = lambda a: a[None]
    return (yp, ys, st(kp), st(vp), st(cp), st(sp), st(kq), st(vq), st(cq), st(sq))
```

```python
import functools
import math

import jax
import jax.numpy as jnp
from jax import lax
from jax.experimental import pallas as pl
from jax.experimental.pallas import tpu as pltpu

F32 = jnp.float32
BF16 = jnp.bfloat16

CHUNK = 64
EPS = 1e-6
M_HEAD_DIM = 64
M_HPG = 16
M_STATE = 128
M_CONV = 4
GROUP_W = M_HPG * M_HEAD_DIM
A_HEAD_DIM = 64
A_REP = 4
KV_W = 2 * A_HEAD_DIM
QH_W = A_REP * KV_W
ROPE_DIM = 16
ROPE_THETA = 500000.0
LANES = 128
VMEM_LIMIT_BYTES = 56 * 1024 * 1024


def _pick(n, target, mult=8):
    if n <= target:
        return n
    for t in range(target, 0, -1):
        if n % t == 0 and t % mult == 0:
            return t
    return n


def _params(*sem):
    return pltpu.CompilerParams(dimension_semantics=sem, vmem_limit_bytes=VMEM_LIMIT_BYTES)


def _silu(x):
    return x * jax.nn.sigmoid(x)


def _split_hi_lo(x):
    hi = x.astype(BF16)
    lo = (x - hi.astype(F32)).astype(BF16)
    return hi, lo


def _dot(a, b):
    return jnp.dot(a, b, preferred_element_type=F32)


def _dot_nt(a, b):
    return lax.dot_general(a, b, (((1,), (1,)), ((), ())), preferred_element_type=F32)


def _dot_tn(a, b):
    return lax.dot_general(a, b, (((0,), (0,)), ((), ())), preferred_element_type=F32)


def _ada_kernel(c_ref, w_ref, b_ref, o_ref):
    a = _silu(c_ref[...]).astype(BF16)
    o_ref[...] = _dot(a, w_ref[...].astype(BF16)) + b_ref[...]


def _ada_mod(c, w_ada, b_ada):
    m, d = c.shape
    n = w_ada.shape[1]
    tn = _pick(n, 512, LANES)
    return pl.pallas_call(
        _ada_kernel,
        grid=(n // tn,),
        in_specs=[pl.BlockSpec((m, d), lambda j: (0, 0)),
                  pl.BlockSpec((d, tn), lambda j: (0, j)),
                  pl.BlockSpec((1, tn), lambda j: (0, j))],
        out_specs=pl.BlockSpec((m, tn), lambda j: (0, j)),
        out_shape=jax.ShapeDtypeStruct((m, n), F32),
        compiler_params=_params("arbitrary"),
        name="ada_mod",
    )(c, w_ada, b_ada.reshape(1, n))


def _prenorm_kernel(x_ref, nw_ref, sc_ref, sh_ref, o_ref):
    x = x_ref[0]
    ms = jnp.mean(x * x, axis=-1, keepdims=True)
    y = x * lax.rsqrt(ms + EPS) * nw_ref[...]
    o_ref[0] = (y * (1.0 + sc_ref[0]) + sh_ref[0]).astype(o_ref.dtype)


def _prenorm(x, norm_w, scale, shift):
    b, t, d = x.shape
    tr = _pick(t, 256)
    return pl.pallas_call(
        _prenorm_kernel,
        grid=(b, t // tr),
        in_specs=[pl.BlockSpec((1, tr, d), lambda i, j: (i, j, 0)),
                  pl.BlockSpec((1, d), lambda i, j: (0, 0)),
                  pl.BlockSpec((1, 1, d), lambda i, j: (i, 0, 0)),
                  pl.BlockSpec((1, 1, d), lambda i, j: (i, 0, 0))],
        out_specs=pl.BlockSpec((1, tr, d), lambda i, j: (i, j, 0)),
        out_shape=jax.ShapeDtypeStruct((b, t, d), BF16),
        compiler_params=_params("arbitrary", "arbitrary"),
        name="prenorm",
    )(x, norm_w.reshape(1, d), scale.reshape(b, 1, d), shift.reshape(b, 1, d))


def _mm_kernel(a_ref, w_ref, o_ref):
    o_ref[...] = _dot(a_ref[...], w_ref[...]).astype(o_ref.dtype)


def _matmul(a, w, name):
    m, k = a.shape
    n = w.shape[1]
    tm = _pick(m, 512)
    tn = _pick(n, 1024, LANES)
    return pl.pallas_call(
        _mm_kernel,
        grid=(n // tn, m // tm),
        in_specs=[pl.BlockSpec((tm, k), lambda j, i: (i, 0)),
                  pl.BlockSpec((k, tn), lambda j, i: (0, j))],
        out_specs=pl.BlockSpec((tm, tn), lambda j, i: (i, j)),
        out_shape=jax.ShapeDtypeStruct((m, n), F32),
        compiler_params=_params("arbitrary", "arbitrary"),
        name=name,
    )(a, w)


def _dt_kernel(a_ref, w_ref, b_ref, o_ref):
    x = _dot(a_ref[...], w_ref[...]) + b_ref[...]
    o_ref[...] = jnp.maximum(x, 0.0) + jnp.log1p(jnp.exp(-jnp.abs(x)))


def _dt_proj(a, w_dt, dt_bias):
    m, k = a.shape
    n = w_dt.shape[1]
    tm = _pick(m, 512)
    return pl.pallas_call(
        _dt_kernel,
        grid=(m // tm,),
        in_specs=[pl.BlockSpec((tm, k), lambda i: (i, 0)),
                  pl.BlockSpec((k, n), lambda i: (0, 0)),
                  pl.BlockSpec((1, n), lambda i: (0, 0))],
        out_specs=pl.BlockSpec((tm, n), lambda i: (i, 0)),
        out_shape=jax.ShapeDtypeStruct((m, n), F32),
        compiler_params=_params("arbitrary"),
        name="dt_proj",
    )(a, w_dt, dt_bias.reshape(1, n))


def _rope_slab(x, cos, sin_up, sin_dn):
    return x * cos + pltpu.roll(x, LANES - ROPE_DIM // 2, 1) * sin_up + pltpu.roll(x, ROPE_DIM // 2, 1) * sin_dn


def _rope_kernel(q_ref, k_ref, v_ref, cos_ref, sup_ref, sdn_ref, qo_ref, ko_ref, vo_ref, kb_ref, vb_ref, *, q_scale):
    cos, sup, sdn = cos_ref[...], sup_ref[...], sdn_ref[...]
    for s in range(q_ref.shape[2] // LANES):
        sl = slice(s * LANES, (s + 1) * LANES)
        qo_ref[0, :, sl] = (_rope_slab(q_ref[0, :, sl], cos, sup, sdn) * q_scale).astype(qo_ref.dtype)
    for s in range(k_ref.shape[2] // LANES):
        sl = slice(s * LANES, (s + 1) * LANES)
        k = _rope_slab(k_ref[0, :, sl], cos, sup, sdn)
        ko_ref[0, :, sl] = k
        kb_ref[0, :, sl] = k.astype(kb_ref.dtype)
    v = v_ref[...]
    vo_ref[...] = v
    vb_ref[...] = v.astype(vb_ref.dtype)


def _rope_tables(pos):
    half = ROPE_DIM // 2
    inv = ROPE_THETA ** (-jnp.arange(half, dtype=F32) / half)
    ang = pos.astype(F32)[:, None] * inv[None, :]
    cos, sin = jnp.cos(ang), jnp.sin(ang)
    t = pos.shape[0]
    ones = jnp.ones((t, A_HEAD_DIM - ROPE_DIM), F32)
    zeros = jnp.zeros((t, A_HEAD_DIM - ROPE_DIM), F32)
    zh = jnp.zeros((t, half), F32)
    cos64 = jnp.concatenate([cos, cos, ones], axis=1)
    sup64 = jnp.concatenate([-sin, zh, zeros], axis=1)
    sdn64 = jnp.concatenate([zh, sin, zeros], axis=1)
    rep = lambda a: jnp.concatenate([a, a], axis=1)
    return rep(cos64), rep(sup64), rep(sdn64)


def _rope_kv(proj, pos, d, kvw, k_blk, v_blk):
    b, t, _ = proj.shape
    tr = _pick(t, 256)
    cos, sup, sdn = _rope_tables(pos)
    tab = pl.BlockSpec((tr, LANES), lambda i, j: (j, 0))
    kv_out = pl.BlockSpec((1, tr, kvw), lambda i, j: (i, j, 0))
    return pl.pallas_call(
        functools.partial(_rope_kernel, q_scale=A_HEAD_DIM ** -0.5 * math.log2(math.e)),
        grid=(b, t // tr),
        in_specs=[pl.BlockSpec((1, tr, d), lambda i, j: (i, j, 0)),
                  pl.BlockSpec((1, tr, kvw), lambda i, j: (i, j, k_blk)),
                  pl.BlockSpec((1, tr, kvw), lambda i, j: (i, j, v_blk)),
                  tab, tab, tab],
        out_specs=[pl.BlockSpec((1, tr, d), lambda i, j: (i, j, 0))] + [kv_out] * 4,
        out_shape=[jax.ShapeDtypeStruct((b, t, d), BF16),
                   jax.ShapeDtypeStruct((b, t, kvw), F32),
                   jax.ShapeDtypeStruct((b, t, kvw), F32),
                   jax.ShapeDtypeStruct((b, t, kvw), BF16),
                   jax.ShapeDtypeStruct((b, t, kvw), BF16)],
        compiler_params=_params("arbitrary", "arbitrary"),
        name="rope_kv",
    )(proj, proj, proj, cos, sup, sdn)


def _ssd_kernel(*refs, tl, has_state):
    (alog_ref, alr_ref, dsk_ref, mnw_ref, cwx_ref, cwb_ref, cwc_ref, cbx_ref, cbb_ref, cbc_ref,
     z_ref, x_ref, bm_ref, cm_ref, dt_ref, dtr_ref) = refs[:16]
    if has_state:
        csx_ref, csb_ref, csc_ref, s0_ref = refs[16:20]
        rest = refs[20:]
    else:
        rest = refs[16:]
    y_ref, sout_ref, xbuf, bbuf, cbuf, st_scr = rest
    g = pl.program_id(1)
    c = pl.program_id(2)
    L = CHUNK
    nsub = tl // L
    halo = M_CONV - 1
    base = 8

    @pl.when(c == 0)
    def _():
        if has_state:
            xbuf[base - halo:base, :] = csx_ref[0]
            bbuf[base - halo:base, :] = csb_ref[0]
            cbuf[base - halo:base, :] = csc_ref[0]
            st_scr[...] = s0_ref[0].reshape(GROUP_W, M_STATE).T
        else:
            xbuf[0:base, :] = jnp.zeros((base, GROUP_W), F32)
            bbuf[0:base, :] = jnp.zeros((base, M_STATE), F32)
            cbuf[0:base, :] = jnp.zeros((base, M_STATE), F32)
            st_scr[...] = jnp.zeros_like(st_scr)

    xbuf[base:base + tl, :] = x_ref[0]
    bbuf[base:base + tl, :] = bm_ref[0]
    cbuf[base:base + tl, :] = cm_ref[0]

    r64 = lax.broadcasted_iota(jnp.int32, (L, L), 0)
    c64 = lax.broadcasted_iota(jnp.int32, (L, L), 1)
    tril = (c64 <= r64).astype(BF16)
    heads = dt_ref.shape[2]
    hrow = lax.broadcasted_iota(jnp.int32, (heads, GROUP_W), 0)
    hcol = lax.broadcasted_iota(jnp.int32, (heads, GROUP_W), 1)
    expand = (hrow == g * M_HPG + hcol // M_HEAD_DIM).astype(BF16)
    br = lax.broadcasted_iota(jnp.int32, (LANES, LANES), 0)
    bc = lax.broadcasted_iota(jnp.int32, (LANES, LANES), 1)
    same_half = (br // L) == (bc // L)
    triu2 = (same_half & ((br % L) <= (bc % L))).astype(BF16)
    pr = lax.broadcasted_iota(jnp.int32, (L, LANES), 0)
    pc = lax.broadcasted_iota(jnp.int32, (L, LANES), 1)
    causal2 = (pc % L) <= pr

    a_col = -jnp.exp(alog_ref[...])
    a_row = -jnp.exp(alr_ref[0])
    n_rows = dtr_ref.shape[2]
    da_r = dtr_ref[0, 0] * jnp.concatenate([a_row] * (n_rows // 8), axis=0)
    hi, lo = _split_hi_lo(da_r)
    acum_r_all = _dot(hi, triu2) + _dot(lo, triu2)

    def conv(buf, w_ref, b_ref, i):
        acc = b_ref[...] + w_ref[M_CONV - 1:M_CONV, :] * buf[base + i * L:base + (i + 1) * L, :]
        for k in range(M_CONV - 1):
            off = base - halo + k + i * L
            acc = acc + w_ref[k:k + 1, :] * buf[off:off + L, :]
        return _silu(acc)

    for i in range(nsub):
        rows = slice(i * L, (i + 1) * L)
        x = conv(xbuf, cwx_ref, cbx_ref, i)
        bm = conv(bbuf, cwb_ref, cbb_ref, i).astype(BF16)
        cm = conv(cbuf, cwc_ref, cbc_ref, i).astype(BF16)
        dt = dt_ref[0, rows, :]
        da = dt * a_col
        hi, lo = _split_hi_lo(da)
        acum = _dot(tril, hi) + _dot(tril, lo)
        hi, lo = _split_hi_lo(acum)
        acum_x = _dot(hi, expand) + _dot(lo, expand)
        hi, lo = _split_hi_lo(dt)
        dt_x = _dot(hi, expand) + _dot(lo, expand)
        xdt = x * dt_x
        xdt_b = xdt.astype(BF16)
        alast_x = acum_x[L - 1:L, :]
        cb2 = _dot_nt(cm, jnp.concatenate([bm, bm], axis=0))
        st_b = st_scr[...].astype(BF16)
        y_off = _dot(cm, st_b) * jnp.exp(acum_x)
        y_parts = []
        for jj in range(GROUP_W // LANES):
            ls = slice(jj * LANES, (jj + 1) * LANES)
            seg = acum_x[:, ls] - acum_r_all[i * 8 + jj:i * 8 + jj + 1, :]
            m2 = (jnp.where(causal2, jnp.exp(seg), 0.0) * cb2).astype(BF16)
            x2 = xdt_b[:, ls]
            rhs = jnp.where(same_half, jnp.concatenate([x2, x2], axis=0), jnp.zeros((), BF16))
            y_parts.append(_dot(m2, rhs))
        y = jnp.concatenate(y_parts, axis=1) + y_off + dsk_ref[0] * x
        y = y * _silu(z_ref[0, rows, :])
        ms = jnp.mean(y * y, axis=-1, keepdims=True)
        y_ref[0, rows, :] = (y * lax.rsqrt(ms + EPS) * mnw_ref[0]).astype(y_ref.dtype)
        xw = (xdt * jnp.exp(alast_x - acum_x)).astype(BF16)
        st_scr[...] = st_scr[...] * jnp.exp(alast_x) + _dot_tn(bm, xw)

    tx = xbuf[base + tl - halo:base + tl, :]
    tb = bbuf[base + tl - halo:base + tl, :]
    tc = cbuf[base + tl - halo:base + tl, :]
    xbuf[base - halo:base, :] = tx
    bbuf[base - halo:base, :] = tb
    cbuf[base - halo:base, :] = tc

    @pl.when(c == pl.num_programs(2) - 1)
    def _():
        sout_ref[0] = st_scr[...].T.reshape(M_HPG, M_HEAD_DIM, M_STATE)


def _ssd(proj, dt, conv_state, ssm_state, conv_w, conv_b, a_log, d_skip, mnorm_w, z_off, xbc_off, d_inner):
    b, t, _ = proj.shape
    groups = d_inner // GROUP_W
    heads = groups * M_HPG
    bcw = groups * M_STATE
    has_state = conv_state is not None
    tl = _pick(t, 256, CHUNK)
    nsub = tl // CHUNK
    nc = t // tl
    dtr = dt.reshape(b, t // CHUNK, CHUNK, groups, M_HPG // 2, 2).transpose(0, 3, 1, 4, 5, 2)
    dtr = dtr.reshape(b, groups, (t // CHUNK) * 8, LANES)
    n_rows = nsub * 8
    if n_rows < 16:
        dtr = jnp.concatenate([dtr, jnp.zeros_like(dtr)], axis=2)
        n_rows = 16
    alr = jnp.repeat(a_log.reshape(groups, M_HPG // 2, 2), CHUNK, axis=2)
    dsk = jnp.repeat(d_skip.reshape(groups, 1, M_HPG), M_HEAD_DIM, axis=2)
    mnw = mnorm_w.reshape(groups, 1, GROUP_W)
    xb, bb, cb = xbc_off // GROUP_W, (xbc_off + d_inner) // M_STATE, (xbc_off + d_inner + bcw) // M_STATE
    cxb, cbb, ccb = 0, d_inner // M_STATE, (d_inner + bcw) // M_STATE
    zb = z_off // GROUP_W
    conv_b2 = conv_b.reshape(1, -1)
    gmap = lambda blk: (lambda i, g, c: (0, blk + g))
    in_specs = [
        pl.BlockSpec((1, heads), lambda i, g, c: (0, 0)),
        pl.BlockSpec((1, 8, LANES), lambda i, g, c: (g, 0, 0)),
        pl.BlockSpec((1, 1, GROUP_W), lambda i, g, c: (g, 0, 0)),
        pl.BlockSpec((1, 1, GROUP_W), lambda i, g, c: (g, 0, 0)),
        pl.BlockSpec((M_CONV, GROUP_W), gmap(cxb)),
        pl.BlockSpec((M_CONV, M_STATE), gmap(cbb)),
        pl.BlockSpec((M_CONV, M_STATE), gmap(ccb)),
        pl.BlockSpec((1, GROUP_W), gmap(cxb)),
        pl.BlockSpec((1, M_STATE), gmap(cbb)),
        pl.BlockSpec((1, M_STATE), gmap(ccb)),
        pl.BlockSpec((1, tl, GROUP_W), lambda i, g, c: (i, c, zb + g)),
        pl.BlockSpec((1, tl, GROUP_W), lambda i, g, c: (i, c, xb + g)),
        pl.BlockSpec((1, tl, M_STATE), lambda i, g, c: (i, c, bb + g)),
        pl.BlockSpec((1, tl, M_STATE), lambda i, g, c: (i, c, cb + g)),
        pl.BlockSpec((1, tl, heads), lambda i, g, c: (i, c, 0)),
        pl.BlockSpec((1, 1, n_rows, LANES), lambda i, g, c: (i, g, c, 0)),
    ]
    args = [a_log.reshape(1, heads), alr, dsk, mnw, conv_w, conv_w, conv_w, conv_b2, conv_b2, conv_b2,
            proj, proj, proj, proj, dt.reshape(b, t, heads), dtr]
    if has_state:
        halo = M_CONV - 1
        in_specs += [
            pl.BlockSpec((1, halo, GROUP_W), lambda i, g, c: (i, 0, cxb + g)),
            pl.BlockSpec((1, halo, M_STATE), lambda i, g, c: (i, 0, cbb + g)),
            pl.BlockSpec((1, halo, M_STATE), lambda i, g, c: (i, 0, ccb + g)),
            pl.BlockSpec((1, M_HPG, M_HEAD_DIM, M_STATE), lambda i, g, c: (i, g, 0, 0)),
        ]
        args += [conv_state, conv_state, conv_state, ssm_state]
    return pl.pallas_call(
        functools.partial(_ssd_kernel, tl=tl, has_state=has_state),
        grid=(b, groups, nc),
        in_specs=in_specs,
        out_specs=[pl.BlockSpec((1, tl, GROUP_W), lambda i, g, c: (i, c, g)),
                   pl.BlockSpec((1, M_HPG, M_HEAD_DIM, M_STATE), lambda i, g, c: (i, g, 0, 0))],
        out_shape=[jax.ShapeDtypeStruct((b, t, d_inner), BF16),
                   jax.ShapeDtypeStruct((b, heads, M_HEAD_DIM, M_STATE), F32)],
        scratch_shapes=[pltpu.VMEM((tl + 8, GROUP_W), F32),
                        pltpu.VMEM((tl + 8, M_STATE), F32),
                        pltpu.VMEM((tl + 8, M_STATE), F32),
                        pltpu.VMEM((M_STATE, GROUP_W), F32)],
        compiler_params=_params("arbitrary", "arbitrary", "arbitrary"),
        name="ssd",
    )(*args)


def _attn_kernel(*refs, tq, tk, rc, n_qt, past, tkp, pos0, lam_init):
    lq1_ref, lk1_ref, lq2_ref, lk2_ref, nw_ref, q_ref, k_ref, v_ref, za_ref = refs[:9]
    if past:
        kp_ref, vp_ref = refs[9:11]
        rest = refs[11:]
    else:
        rest = refs[9:]
    o_ref, qz_scr, s_scr, p_scr, m_scr, l_scr, acc_scr = rest
    qi = pl.program_id(2)
    rows = A_REP * tq

    lane = lax.broadcasted_iota(jnp.int32, (tq, KV_W), 1)
    zero = jnp.zeros((), BF16)
    for r in range(A_REP):
        q = q_ref[0, :, r * KV_W:(r + 1) * KV_W]
        qz_scr[0, r * tq:(r + 1) * tq, :] = jnp.where(lane < A_HEAD_DIM, q, zero)
        qz_scr[1, r * tq:(r + 1) * tq, :] = jnp.where(lane >= A_HEAD_DIM, q, zero)
    m_scr[...] = jnp.full(m_scr.shape, -jnp.inf, F32)
    l_scr[...] = jnp.zeros(l_scr.shape, F32)
    acc_scr[...] = jnp.zeros(acc_scr.shape, F32)

    def tile(k, v, nkeys, mask_fn):
        reps = nkeys // LANES
        for comp in range(2):
            s_scr[comp, :, :nkeys] = _dot_nt(qz_scr[comp], k)
        for comp in range(2):
            for i in range(rows // rc):
                rs = slice(i * rc, (i + 1) * rc)
                s = s_scr[comp, rs, :nkeys]
                if mask_fn is not None:
                    s = jnp.where(mask_fn(i), s, -jnp.inf)
                m_old = m_scr[comp, rs, :]
                m_new = jnp.maximum(m_old, jnp.max(s, axis=-1, keepdims=True))
                p = jnp.exp2(s - jnp.concatenate([m_new] * reps, axis=1))
                alpha = jnp.exp2(m_old - m_new)
                psum = p[:, :LANES]
                for u in range(1, reps):
                    psum = psum + p[:, u * LANES:(u + 1) * LANES]
                l_scr[comp, rs, :] = alpha * l_scr[comp, rs, :] + psum
                acc_scr[comp, rs, :] = alpha * acc_scr[comp, rs, :]
                m_scr[comp, rs, :] = m_new
                p_scr[comp, rs, :nkeys] = p.astype(BF16)
            acc_scr[comp] = acc_scr[comp] + _dot(p_scr[comp, :, :nkeys], v)

    def diag_mask(i, nkeys, n_valid, k_pos0):
        t0 = (i * rc) % tq
        qpos = pos0 + qi * tq + t0 + lax.broadcasted_iota(jnp.int32, (rc, nkeys), 0)
        kidx = lax.broadcasted_iota(jnp.int32, (rc, nkeys), 1)
        ok = ((k_pos0 + kidx) // CHUNK) <= (qpos // CHUNK)
        if n_valid < nkeys:
            ok = ok & (kidx < n_valid)
        return ok

    if past:
        def pbody(j, carry):
            sl = pl.ds(pl.multiple_of(j * tkp, tkp), tkp)
            tile(kp_ref[0, sl, :].astype(BF16), vp_ref[0, sl, :].astype(BF16), tkp, None)
            return carry

        lax.fori_loop(0, past // tkp, pbody, 0)

    def nbody(j, carry):
        sl = pl.ds(pl.multiple_of(j * tk, tk), tk)
        tile(k_ref[0, sl, :], v_ref[0, sl, :], tk, None)
        return carry

    if n_qt > 1:
        lax.fori_loop(0, (qi * tq) // tk, nbody, 0)
    tkd = max(tk, LANES)
    for d in range(tq // tk):
        k0 = qi * tq + d * tk
        sl = pl.ds(pl.multiple_of(k0, tk), tk)
        k, v = k_ref[0, sl, :], v_ref[0, sl, :]
        if tkd > tk:
            pad = jnp.zeros((tkd - tk, KV_W), BF16)
            k, v = jnp.concatenate([k, pad], axis=0), jnp.concatenate([v, pad], axis=0)
        tile(k, v, tkd, functools.partial(diag_mask, nkeys=tkd, n_valid=tk, k_pos0=pos0 + k0))

    lam = (jnp.exp(jnp.sum(lq1_ref[...] * lk1_ref[...], axis=-1, keepdims=True))
           - jnp.exp(jnp.sum(lq2_ref[...] * lk2_ref[...], axis=-1, keepdims=True)) + lam_init)
    l0 = jnp.sum(l_scr[0], axis=-1, keepdims=True)
    l1 = jnp.sum(l_scr[1], axis=-1, keepdims=True)
    o = acc_scr[0] / l0 - lam * (acc_scr[1] / l1)
    ms = jnp.mean(o * o, axis=-1, keepdims=True)
    on = o * lax.rsqrt(ms + EPS) * nw_ref[...] * (1.0 - lam_init)
    for r in range(A_REP):
        sl = slice(r * KV_W, (r + 1) * KV_W)
        o_ref[0, :, sl] = (on[r * tq:(r + 1) * tq, :] * _silu(za_ref[0, :, sl])).astype(o_ref.dtype)


def _attention(qr, kb, vb, proj, za_off, k_past, v_past, lam_vecs, norm_w, lam_init, pos0):
    b, t, d = qr.shape
    kvh = d // QH_W
    past = 0 if k_past is None else k_past.shape[1]
    tq = _pick(t, 512, CHUNK)
    tk = tq
    tkp = _pick(past, 512, LANES) if past else 0
    rc = min(64, tq)
    zb = za_off // QH_W
    rows = A_REP * tq
    smax = max(tk, LANES, tkp)
    vec = pl.BlockSpec((1, A_HEAD_DIM), lambda i, h, j: (0, 0))
    in_specs = [vec, vec, vec, vec,
                pl.BlockSpec((1, KV_W), lambda i, h, j: (0, 0)),
                pl.BlockSpec((1, tq, QH_W), lambda i, h, j: (i, j, h)),
                pl.BlockSpec((1, t, KV_W), lambda i, h, j: (i, 0, h)),
                pl.BlockSpec((1, t, KV_W), lambda i, h, j: (i, 0, h)),
                pl.BlockSpec((1, tq, QH_W), lambda i, h, j: (i, j, zb + h))]
    args = [v.reshape(1, A_HEAD_DIM) for v in lam_vecs] + [norm_w.reshape(1, KV_W), qr, kb, vb, proj]
    if past:
        in_specs += [pl.BlockSpec((1, past, KV_W), lambda i, h, j: (i, 0, h)),
                     pl.BlockSpec((1, past, KV_W), lambda i, h, j: (i, 0, h))]
        args += [k_past, v_past]
    return pl.pallas_call(
        functools.partial(_attn_kernel, tq=tq, tk=tk, rc=rc, n_qt=t // tq, past=past, tkp=tkp, pos0=pos0,
                          lam_init=lam_init),
        grid=(b, kvh, t // tq),
        in_specs=in_specs,
        out_specs=pl.BlockSpec((1, tq, QH_W), lambda i, h, j: (i, j, h)),
        out_shape=jax.ShapeDtypeStruct((b, t, d), BF16),
        scratch_shapes=[pltpu.VMEM((2, rows, KV_W), BF16),
                        pltpu.VMEM((2, rows, smax), F32),
                        pltpu.VMEM((2, rows, smax), BF16),
                        pltpu.VMEM((2, rows, LANES), F32),
                        pltpu.VMEM((2, rows, LANES), F32),
                        pltpu.VMEM((2, rows, KV_W), F32)],
        compiler_params=_params("arbitrary", "arbitrary", "arbitrary"),
        name="diff_attn",
    )(*args)


def _merge_kernel(ym_ref, ya_ref, wm_ref, wa_ref, gm_ref, ga_ref, o_ref):
    pm = _dot(ym_ref[...], wm_ref[...])
    pa = _dot(ya_ref[...], wa_ref[...])
    o_ref[...] = (jax.nn.sigmoid(gm_ref[...]) * pm + jax.nn.sigmoid(ga_ref[...]) * pa).astype(o_ref.dtype)


def _merge(y_m, y_a, w_pm, w_pa, proj2d, gm_off, ga_off):
    m, km = y_m.shape
    ka = y_a.shape[1]
    d = w_pm.shape[1]
    tm = _pick(m, 256)
    tn = _pick(d, 512, LANES)
    gmb, gab = gm_off // tn, ga_off // tn
    return pl.pallas_call(
        _merge_kernel,
        grid=(d // tn, m // tm),
        in_specs=[pl.BlockSpec((tm, km), lambda j, i: (i, 0)),
                  pl.BlockSpec((tm, ka), lambda j, i: (i, 0)),
                  pl.BlockSpec((km, tn), lambda j, i: (0, j)),
                  pl.BlockSpec((ka, tn), lambda j, i: (0, j)),
                  pl.BlockSpec((tm, tn), lambda j, i: (i, gmb + j)),
                  pl.BlockSpec((tm, tn), lambda j, i: (i, gab + j))],
        out_specs=pl.BlockSpec((tm, tn), lambda j, i: (i, j)),
        out_shape=jax.ShapeDtypeStruct((m, d), BF16),
        compiler_params=_params("arbitrary", "arbitrary"),
        name="merge",
    )(y_m, y_a, w_pm, w_pa, proj2d, proj2d)


def _out_kernel(mg_ref, w_ref, x_ref, gate_ref, fw_ref, o_ref, *, tn):
    j = pl.program_id(2)
    cols = pl.ds(pl.multiple_of(j * tn, tn), tn)
    o_ref[0, :, cols] = x_ref[0] + gate_ref[0] * _dot(mg_ref[0], w_ref[...])

    @pl.when(j == pl.num_programs(2) - 1)
    def _():
        r = o_ref[0]
        ms = jnp.mean(r * r, axis=-1, keepdims=True)
        o_ref[0] = r * lax.rsqrt(ms + EPS) * fw_ref[...]


def _out_proj(merged, w_out, x, gate, final_w):
    b, t, d = x.shape
    tm = _pick(t, 512)
    tn = _pick(d, 512, LANES)
    return pl.pallas_call(
        functools.partial(_out_kernel, tn=tn),
        grid=(b, t // tm, d // tn),
        in_specs=[pl.BlockSpec((1, tm, d), lambda i, r, j: (i, r, 0)),
                  pl.BlockSpec((d, tn), lambda i, r, j: (0, j)),
                  pl.BlockSpec((1, tm, tn), lambda i, r, j: (i, r, j)),
                  pl.BlockSpec((1, 1, tn), lambda i, r, j: (i, 0, j)),
                  pl.BlockSpec((1, d), lambda i, r, j: (0, 0))],
        out_specs=pl.BlockSpec((1, tm, d), lambda i, r, j: (i, r, 0)),
        out_shape=jax.ShapeDtypeStruct((b, t, d), F32),
        compiler_params=_params("arbitrary", "arbitrary", "arbitrary"),
        name="out_proj",
    )(merged.reshape(b, t, d), w_out, x, gate.reshape(b, 1, d), final_w.reshape(1, d))


def _layer_path(x, mod, k_past, v_past, conv_state, ssm_state, pos0, w, lam_init, final_w):
    b, t, d = x.shape
    d_inner = 2 * d
    groups = d_inner // GROUP_W
    bcw = groups * M_STATE
    conv_dim = d_inner + 2 * bcw
    kvw = (d // QH_W) * KV_W
    za_off, gm_off, ga_off, z_off, xbc_off = d, 2 * d, 3 * d, 4 * d, 6 * d
    k_off = xbc_off + conv_dim
    v_off = k_off + kvw
    np_ = v_off + kvw
    shift, scale, gate = mod[:, :d], mod[:, d:2 * d], mod[:, 2 * d:]

    h = _prenorm(x, w['norm_w'], scale, shift).reshape(b * t, d)
    proj = _matmul(h, w['w_main'], "in_proj").reshape(b, t, np_)
    dt = _dt_proj(h, w['w_dt'], w['dt_bias']).reshape(b, t, -1)

    pos = pos0 + jnp.arange(t, dtype=jnp.int32)
    qr, k_new, v_new, kb, vb = _rope_kv(proj, pos, d, kvw, k_off // kvw, v_off // kvw)

    y_m, ssm_new = _ssd(proj, dt, conv_state, ssm_state, w['conv_w'], w['conv_b'], w['a_log'], w['d_skip'],
                        w['mamba_norm_w'], z_off, xbc_off, d_inner)
    halo = M_CONV - 1
    conv_new = proj[:, t - halo:, xbc_off:xbc_off + conv_dim]

    y_a = _attention(qr, kb, vb, proj, za_off, k_past, v_past,
                     (w['lam_q1'], w['lam_k1'], w['lam_q2'], w['lam_k2']), w['attn_norm_w'], lam_init, pos0)

    merged = _merge(y_m.reshape(b * t, d_inner), y_a.reshape(b * t, d), w['w_proj_m'], w['w_proj_a'],
                    proj.reshape(b * t, np_), gm_off, ga_off)
    y = _out_proj(merged, w['w_out'], x, gate, final_w)
    kvh = d // QH_W
    return (y, k_new.reshape(b, t, kvh, 2, A_HEAD_DIM), v_new.reshape(b, t, kvh, KV_W), conv_new, ssm_new)


def kernel(x_prompt, x_sample, cache_k, cache_v, state_conv, state_ssm, c_prompt, c_sample,
           w_ada, b_ada, norm_w, w_in, conv_w, conv_b, dt_bias, a_log, d_skip, mamba_norm_w,
           lam_q1, lam_k1, lam_q2, lam_k2, attn_norm_w, w_proj_m, w_proj_a, w_out, final_norm_w):
    depth = w_in.shape[0]
    assert depth == 1, "the final norm is fused into the single layer's output projection"
    bp, d = c_prompt.shape
    bs = c_sample.shape[0]
    past = cache_k.shape[2]
    d_inner = 2 * d
    groups = d_inner // GROUP_W
    heads = groups * M_HPG
    conv_dim = d_inner + 2 * groups * M_STATE
    kvw = (d // QH_W) * KV_W
    sizes = (d_inner, conv_dim, heads, d, kvw, kvw, d, d, d)
    offs = [0]
    for s in sizes:
        offs.append(offs[-1] + s)
    seg = lambda wi, n: wi[:, offs[n]:offs[n + 1]]

    i = 0
    wi = w_in[i]
    w = {
        'w_main': jnp.concatenate([seg(wi, 3), seg(wi, 6), seg(wi, 7), seg(wi, 8), seg(wi, 0), seg(wi, 1),
                                   seg(wi, 4), seg(wi, 5)], axis=1).astype(BF16),
        'w_dt': seg(wi, 2).astype(BF16),
        'norm_w': norm_w[i], 'conv_w': conv_w[i], 'conv_b': conv_b[i], 'dt_bias': dt_bias[i], 'a_log': a_log[i],
        'd_skip': d_skip[i], 'mamba_norm_w': mamba_norm_w[i], 'lam_q1': lam_q1[i], 'lam_k1': lam_k1[i],
        'lam_q2': lam_q2[i], 'lam_k2': lam_k2[i], 'attn_norm_w': attn_norm_w[i],
        'w_proj_m': w_proj_m[i].astype(BF16), 'w_proj_a': w_proj_a[i].astype(BF16), 'w_out': w_out[i].astype(BF16),
    }
    lam_init = 0.8 - 0.6 * math.exp(-0.3 * i)
    pad_rows = -(bp + bs) % 16
    c_all = jnp.concatenate([c_prompt, c_sample, jnp.zeros((pad_rows, d), F32)], axis=0)
    mod = _ada_mod(c_all, w_ada[i], b_ada[i])

    yp, kp, vp, cp, sp = _layer_path(x_prompt, mod[:bp], None, None, None, None, 0, w, lam_init, final_norm_w)
    ck = cache_k[i].reshape(bs, past, kvw)
    cv = cache_v[i].reshape(bs, past, kvw)
    ys, kq, vq, cq, sq = _layer_path(x_sample, mod[bp:bp + bs], ck, cv, state_conv[i], state_ssm[i], past, w, lam_init,
                                     final_norm_w)
    st = lambda a: a[None]
    return (yp, ys, st(kp), st(vp), st(cp), st(sp), st(kq), st(vq), st(cq), st(sq))
```

```python
import functools
import math

import jax
import jax.numpy as jnp
from jax import lax
from jax.experimental import pallas as pl
from jax.experimental.pallas import tpu as pltpu

F32 = jnp.float32
BF16 = jnp.bfloat16

CHUNK = 64
EPS = 1e-6
M_HEAD_DIM = 64
M_HPG = 16
M_STATE = 128
M_CONV = 4
GROUP_W = M_HPG * M_HEAD_DIM
A_HEAD_DIM = 64
A_REP = 4
KV_W = 2 * A_HEAD_DIM
QH_W = A_REP * KV_W
ROPE_DIM = 16
ROPE_THETA = 500000.0
LANES = 128
VMEM_LIMIT_BYTES = 56 * 1024 * 1024


def _pick(n, target, mult=8):
    if n <= target:
        return n
    for t in range(target, 0, -1):
        if n % t == 0 and t % mult == 0:
            return t
    return n


def _params(*sem):
    return pltpu.CompilerParams(dimension_semantics=sem, vmem_limit_bytes=VMEM_LIMIT_BYTES)


def _silu(x):
    return x * jax.nn.sigmoid(x)


def _split_hi_lo(x):
    hi = x.astype(BF16)
    lo = (x - hi.astype(F32)).astype(BF16)
    return hi, lo


def _dot(a, b):
    return jnp.dot(a, b, preferred_element_type=F32)


def _dot_nt(a, b):
    return lax.dot_general(a, b, (((1,), (1,)), ((), ())), preferred_element_type=F32)


def _dot_tn(a, b):
    return lax.dot_general(a, b, (((0,), (0,)), ((), ())), preferred_element_type=F32)


def _ada_kernel(c_ref, w_ref, b_ref, o_ref):
    a = _silu(c_ref[...]).astype(BF16)
    o_ref[...] = _dot(a, w_ref[...].astype(BF16)) + b_ref[...]


def _ada_mod(c, w_ada, b_ada):
    m, d = c.shape
    n = w_ada.shape[1]
    tn = _pick(n, 512, LANES)
    return pl.pallas_call(
        _ada_kernel,
        grid=(n // tn,),
        in_specs=[pl.BlockSpec((m, d), lambda j: (0, 0)),
                  pl.BlockSpec((d, tn), lambda j: (0, j)),
                  pl.BlockSpec((1, tn), lambda j: (0, j))],
        out_specs=pl.BlockSpec((m, tn), lambda j: (0, j)),
        out_shape=jax.ShapeDtypeStruct((m, n), F32),
        compiler_params=_params("arbitrary"),
        name="ada_mod",
    )(c, w_ada, b_ada.reshape(1, n))


def _prenorm_kernel(x_ref, nw_ref, sc_ref, sh_ref, o_ref):
    x = x_ref[0]
    ms = jnp.mean(x * x, axis=-1, keepdims=True)
    y = x * lax.rsqrt(ms + EPS) * nw_ref[...]
    o_ref[0] = (y * (1.0 + sc_ref[0]) + sh_ref[0]).astype(o_ref.dtype)


def _prenorm(x, norm_w, scale, shift):
    b, t, d = x.shape
    tr = _pick(t, 256)
    return pl.pallas_call(
        _prenorm_kernel,
        grid=(b, t // tr),
        in_specs=[pl.BlockSpec((1, tr, d), lambda i, j: (i, j, 0)),
                  pl.BlockSpec((1, d), lambda i, j: (0, 0)),
                  pl.BlockSpec((1, 1, d), lambda i, j: (i, 0, 0)),
                  pl.BlockSpec((1, 1, d), lambda i, j: (i, 0, 0))],
        out_specs=pl.BlockSpec((1, tr, d), lambda i, j: (i, j, 0)),
        out_shape=jax.ShapeDtypeStruct((b, t, d), BF16),
        compiler_params=_params("arbitrary", "arbitrary"),
        name="prenorm",
    )(x, norm_w.reshape(1, d), scale.reshape(b, 1, d), shift.reshape(b, 1, d))


def _mm_kernel(a_ref, w_ref, o_ref):
    o_ref[...] = _dot(a_ref[...], w_ref[...]).astype(o_ref.dtype)


def _matmul(a, w, name):
    m, k = a.shape
    n = w.shape[1]
    tm = _pick(m, 512)
    tn = _pick(n, 1024, LANES)
    return pl.pallas_call(
        _mm_kernel,
        grid=(n // tn, m // tm),
        in_specs=[pl.BlockSpec((tm, k), lambda j, i: (i, 0)),
                  pl.BlockSpec((k, tn), lambda j, i: (0, j))],
        out_specs=pl.BlockSpec((tm, tn), lambda j, i: (i, j)),
        out_shape=jax.ShapeDtypeStruct((m, n), F32),
        compiler_params=_params("arbitrary", "arbitrary"),
        name=name,
    )(a, w)


def _dt_kernel(a_ref, w_ref, b_ref, o_ref):
    x = _dot(a_ref[...], w_ref[...]) + b_ref[...]
    o_ref[...] = jnp.maximum(x, 0.0) + jnp.log1p(jnp.exp(-jnp.abs(x)))


def _dt_proj(a, w_dt, dt_bias):
    m, k = a.shape
    n = w_dt.shape[1]
    tm = _pick(m, 512)
    return pl.pallas_call(
        _dt_kernel,
        grid=(m // tm,),
        in_specs=[pl.BlockSpec((tm, k), lambda i: (i, 0)),
                  pl.BlockSpec((k, n), lambda i: (0, 0)),
                  pl.BlockSpec((1, n), lambda i: (0, 0))],
        out_specs=pl.BlockSpec((tm, n), lambda i: (i, 0)),
        out_shape=jax.ShapeDtypeStruct((m, n), F32),
        compiler_params=_params("arbitrary"),
        name="dt_proj",
    )(a, w_dt, dt_bias.reshape(1, n))


def _rope_slab(x, cos, sin_up, sin_dn):
    return x * cos + pltpu.roll(x, LANES - ROPE_DIM // 2, 1) * sin_up + pltpu.roll(x, ROPE_DIM // 2, 1) * sin_dn


def _rope_kernel(q_ref, k_ref, v_ref, cos_ref, sup_ref, sdn_ref, qo_ref, ko_ref, vo_ref, kb_ref, vb_ref, *, q_scale):
    cos, sup, sdn = cos_ref[...], sup_ref[...], sdn_ref[...]
    for s in range(q_ref.shape[2] // LANES):
        sl = slice(s * LANES, (s + 1) * LANES)
        qo_ref[0, :, sl] = (_rope_slab(q_ref[0, :, sl], cos, sup, sdn) * q_scale).astype(qo_ref.dtype)
    for s in range(k_ref.shape[2] // LANES):
        sl = slice(s * LANES, (s + 1) * LANES)
        k = _rope_slab(k_ref[0, :, sl], cos, sup, sdn)
        ko_ref[0, :, sl] = k
        kb_ref[0, :, sl] = k.astype(kb_ref.dtype)
    v = v_ref[...]
    vo_ref[...] = v
    vb_ref[...] = v.astype(vb_ref.dtype)


def _rope_tables(pos):
    half = ROPE_DIM // 2
    inv = ROPE_THETA ** (-jnp.arange(half, dtype=F32) / half)
    ang = pos.astype(F32)[:, None] * inv[None, :]
    cos, sin = jnp.cos(ang), jnp.sin(ang)
    t = pos.shape[0]
    ones = jnp.ones((t, A_HEAD_DIM - ROPE_DIM), F32)
    zeros = jnp.zeros((t, A_HEAD_DIM - ROPE_DIM), F32)
    zh = jnp.zeros((t, half), F32)
    cos64 = jnp.concatenate([cos, cos, ones], axis=1)
    sup64 = jnp.concatenate([-sin, zh, zeros], axis=1)
    sdn64 = jnp.concatenate([zh, sin, zeros], axis=1)
    rep = lambda a: jnp.concatenate([a, a], axis=1)
    return rep(cos64), rep(sup64), rep(sdn64)


def _rope_kv(proj, pos, d, kvw, k_blk, v_blk):
    b, t, _ = proj.shape
    tr = _pick(t, 256)
    cos, sup, sdn = _rope_tables(pos)
    tab = pl.BlockSpec((tr, LANES), lambda i, j: (j, 0))
    kv_out = pl.BlockSpec((1, tr, kvw), lambda i, j: (i, j, 0))
    return pl.pallas_call(
        functools.partial(_rope_kernel, q_scale=A_HEAD_DIM ** -0.5 * math.log2(math.e)),
        grid=(b, t // tr),
        in_specs=[pl.BlockSpec((1, tr, d), lambda i, j: (i, j, 0)),
                  pl.BlockSpec((1, tr, kvw), lambda i, j: (i, j, k_blk)),
                  pl.BlockSpec((1, tr, kvw), lambda i, j: (i, j, v_blk)),
                  tab, tab, tab],
        out_specs=[pl.BlockSpec((1, tr, d), lambda i, j: (i, j, 0))] + [kv_out] * 4,
        out_shape=[jax.ShapeDtypeStruct((b, t, d), BF16),
                   jax.ShapeDtypeStruct((b, t, kvw), F32),
                   jax.ShapeDtypeStruct((b, t, kvw), F32),
                   jax.ShapeDtypeStruct((b, t, kvw), BF16),
                   jax.ShapeDtypeStruct((b, t, kvw), BF16)],
        compiler_params=_params("arbitrary", "arbitrary"),
        name="rope_kv",
    )(proj, proj, proj, cos, sup, sdn)


def _ssd_kernel(*refs, tl, has_state):
    (alog_ref, alr_ref, dsk_ref, mnw_ref, cwx_ref, cwb_ref, cwc_ref, cbx_ref, cbb_ref, cbc_ref,
     z_ref, x_ref, bm_ref, cm_ref, dt_ref, dtr_ref) = refs[:16]
    if has_state:
        csx_ref, csb_ref, csc_ref, s0_ref = refs[16:20]
        rest = refs[20:]
    else:
        rest = refs[16:]
    y_ref, sout_ref, xbuf, bbuf, cbuf, st_scr = rest
    g = pl.program_id(1)
    c = pl.program_id(2)
    L = CHUNK
    nsub = tl // L
    halo = M_CONV - 1
    base = 8

    @pl.when(c == 0)
    def _():
        if has_state:
            xbuf[base - halo:base, :] = csx_ref[0]
            bbuf[base - halo:base, :] = csb_ref[0]
            cbuf[base - halo:base, :] = csc_ref[0]
            st_scr[...] = s0_ref[0].reshape(GROUP_W, M_STATE).T
        else:
            xbuf[0:base, :] = jnp.zeros((base, GROUP_W), F32)
            bbuf[0:base, :] = jnp.zeros((base, M_STATE), F32)
            cbuf[0:base, :] = jnp.zeros((base, M_STATE), F32)
            st_scr[...] = jnp.zeros_like(st_scr)

    xbuf[base:base + tl, :] = x_ref[0]
    bbuf[base:base + tl, :] = bm_ref[0]
    cbuf[base:base + tl, :] = cm_ref[0]

    r64 = lax.broadcasted_iota(jnp.int32, (L, L), 0)
    c64 = lax.broadcasted_iota(jnp.int32, (L, L), 1)
    tril = (c64 <= r64).astype(BF16)
    heads = dt_ref.shape[2]
    hrow = lax.broadcasted_iota(jnp.int32, (heads, GROUP_W), 0)
    hcol = lax.broadcasted_iota(jnp.int32, (heads, GROUP_W), 1)
    expand = (hrow == g * M_HPG + hcol // M_HEAD_DIM).astype(BF16)
    br = lax.broadcasted_iota(jnp.int32, (LANES, LANES), 0)
    bc = lax.broadcasted_iota(jnp.int32, (LANES, LANES), 1)
    same_half = (br // L) == (bc // L)
    triu2 = (same_half & ((br % L) <= (bc % L))).astype(BF16)
    pr = lax.broadcasted_iota(jnp.int32, (L, LANES), 0)
    pc = lax.broadcasted_iota(jnp.int32, (L, LANES), 1)
    causal2 = (pc % L) <= pr

    a_col = -jnp.exp(alog_ref[...])
    a_row = -jnp.exp(alr_ref[0])
    n_rows = dtr_ref.shape[2]
    da_r = dtr_ref[0, 0] * jnp.concatenate([a_row] * (n_rows // 8), axis=0)
    hi, lo = _split_hi_lo(da_r)
    acum_r_all = _dot(hi, triu2) + _dot(lo, triu2)

    def conv(buf, w_ref, b_ref, i):
        acc = b_ref[...] + w_ref[M_CONV - 1:M_CONV, :] * buf[base + i * L:base + (i + 1) * L, :]
        for k in range(M_CONV - 1):
            off = base - halo + k + i * L
            acc = acc + w_ref[k:k + 1, :] * buf[off:off + L, :]
        return _silu(acc)

    for i in range(nsub):
        rows = slice(i * L, (i + 1) * L)
        x = conv(xbuf, cwx_ref, cbx_ref, i)
        bm = conv(bbuf, cwb_ref, cbb_ref, i).astype(BF16)
        cm = conv(cbuf, cwc_ref, cbc_ref, i).astype(BF16)
        dt = dt_ref[0, rows, :]
        da = dt * a_col
        hi, lo = _split_hi_lo(da)
        acum = _dot(tril, hi) + _dot(tril, lo)
        hi, lo = _split_hi_lo(acum)
        acum_x = _dot(hi, expand) + _dot(lo, expand)
        hi, lo = _split_hi_lo(dt)
        dt_x = _dot(hi, expand) + _dot(lo, expand)
        xdt = x * dt_x
        xdt_b = xdt.astype(BF16)
        alast_x = acum_x[L - 1:L, :]
        cb2 = _dot_nt(cm, jnp.concatenate([bm, bm], axis=0))
        st_b = st_scr[...].astype(BF16)
        y_off = _dot(cm, st_b) * jnp.exp(acum_x)
        y_parts = []
        for jj in range(GROUP_W // LANES):
            ls = slice(jj * LANES, (jj + 1) * LANES)
            seg = acum_x[:, ls] - acum_r_all[i * 8 + jj:i * 8 + jj + 1, :]
            m2 = (jnp.where(causal2, jnp.exp(seg), 0.0) * cb2).astype(BF16)
            x2 = xdt_b[:, ls]
            rhs = jnp.where(same_half, jnp.concatenate([x2, x2], axis=0), jnp.zeros((), BF16))
            y_parts.append(_dot(m2, rhs))
        y = jnp.concatenate(y_parts, axis=1) + y_off + dsk_ref[0] * x
        y = y * _silu(z_ref[0, rows, :])
        ms = jnp.mean(y * y, axis=-1, keepdims=True)
        y_ref[0, rows, :] = (y * lax.rsqrt(ms + EPS) * mnw_ref[0]).astype(y_ref.dtype)
        xw = (xdt * jnp.exp(alast_x - acum_x)).astype(BF16)
        st_scr[...] = st_scr[...] * jnp.exp(alast_x) + _dot_tn(bm, xw)

    tx = xbuf[base + tl - halo:base + tl, :]
    tb = bbuf[base + tl - halo:base + tl, :]
    tc = cbuf[base + tl - halo:base + tl, :]
    xbuf[base - halo:base, :] = tx
    bbuf[base - halo:base, :] = tb
    cbuf[base - halo:base, :] = tc

    @pl.when(c == pl.num_programs(2) - 1)
    def _():
        sout_ref[0] = st_scr[...].T.reshape(M_HPG, M_HEAD_DIM, M_STATE)


def _ssd(proj, dt, conv_state, ssm_state, conv_w, conv_b, a_log, d_skip, mnorm_w, z_off, xbc_off, d_inner):
    b, t, _ = proj.shape
    groups = d_inner // GROUP_W
    heads = groups * M_HPG
    bcw = groups * M_STATE
    has_state = conv_state is not None
    tl = _pick(t, 256, CHUNK)
    nsub = tl // CHUNK
    nc = t // tl
    dtr = dt.reshape(b, t // CHUNK, CHUNK, groups, M_HPG // 2, 2).transpose(0, 3, 1, 4, 5, 2)
    dtr = dtr.reshape(b, groups, (t // CHUNK) * 8, LANES)
    n_rows = nsub * 8
    if n_rows < 16:
        dtr = jnp.concatenate([dtr, jnp.zeros_like(dtr)], axis=2)
        n_rows = 16
    alr = jnp.repeat(a_log.reshape(groups, M_HPG // 2, 2), CHUNK, axis=2)
    dsk = jnp.repeat(d_skip.reshape(groups, 1, M_HPG), M_HEAD_DIM, axis=2)
    mnw = mnorm_w.reshape(groups, 1, GROUP_W)
    xb, bb, cb = xbc_off // GROUP_W, (xbc_off + d_inner) // M_STATE, (xbc_off + d_inner + bcw) // M_STATE
    cxb, cbb, ccb = 0, d_inner // M_STATE, (d_inner + bcw) // M_STATE
    zb = z_off // GROUP_W
    conv_b2 = conv_b.reshape(1, -1)
    gmap = lambda blk: (lambda i, g, c: (0, blk + g))
    in_specs = [
        pl.BlockSpec((1, heads), lambda i, g, c: (0, 0)),
        pl.BlockSpec((1, 8, LANES), lambda i, g, c: (g, 0, 0)),
        pl.BlockSpec((1, 1, GROUP_W), lambda i, g, c: (g, 0, 0)),
        pl.BlockSpec((1, 1, GROUP_W), lambda i, g, c: (g, 0, 0)),
        pl.BlockSpec((M_CONV, GROUP_W), gmap(cxb)),
        pl.BlockSpec((M_CONV, M_STATE), gmap(cbb)),
        pl.BlockSpec((M_CONV, M_STATE), gmap(ccb)),
        pl.BlockSpec((1, GROUP_W), gmap(cxb)),
        pl.BlockSpec((1, M_STATE), gmap(cbb)),
        pl.BlockSpec((1, M_STATE), gmap(ccb)),
        pl.BlockSpec((1, tl, GROUP_W), lambda i, g, c: (i, c, zb + g)),
        pl.BlockSpec((1, tl, GROUP_W), lambda i, g, c: (i, c, xb + g)),
        pl.BlockSpec((1, tl, M_STATE), lambda i, g, c: (i, c, bb + g)),
        pl.BlockSpec((1, tl, M_STATE), lambda i, g, c: (i, c, cb + g)),
        pl.BlockSpec((1, tl, heads), lambda i, g, c: (i, c, 0)),
        pl.BlockSpec((1, 1, n_rows, LANES), lambda i, g, c: (i, g, c, 0)),
    ]
    args = [a_log.reshape(1, heads), alr, dsk, mnw, conv_w, conv_w, conv_w, conv_b2, conv_b2, conv_b2,
            proj, proj, proj, proj, dt.reshape(b, t, heads), dtr]
    if has_state:
        halo = M_CONV - 1
        in_specs += [
            pl.BlockSpec((1, halo, GROUP_W), lambda i, g, c: (i, 0, cxb + g)),
            pl.BlockSpec((1, halo, M_STATE), lambda i, g, c: (i, 0, cbb + g)),
            pl.BlockSpec((1, halo, M_STATE), lambda i, g, c: (i, 0, ccb + g)),
            pl.BlockSpec((1, M_HPG, M_HEAD_DIM, M_STATE), lambda i, g, c: (i, g, 0, 0)),
        ]
        args += [conv_state, conv_state, conv_state, ssm_state]
    return pl.pallas_call(
        functools.partial(_ssd_kernel, tl=tl, has_state=has_state),
        grid=(b, groups, nc),
        in_specs=in_specs,
        out_specs=[pl.BlockSpec((1, tl, GROUP_W), lambda i, g, c: (i, c, g)),
                   pl.BlockSpec((1, M_HPG, M_HEAD_DIM, M_STATE), lambda i, g, c: (i, g, 0, 0))],
        out_shape=[jax.ShapeDtypeStruct((b, t, d_inner), BF16),
                   jax.ShapeDtypeStruct((b, heads, M_HEAD_DIM, M_STATE), F32)],
        scratch_shapes=[pltpu.VMEM((tl + 8, GROUP_W), F32),
                        pltpu.VMEM((tl + 8, M_STATE), F32),
                        pltpu.VMEM((tl + 8, M_STATE), F32),
                        pltpu.VMEM((M_STATE, GROUP_W), F32)],
        compiler_params=_params("arbitrary", "arbitrary", "arbitrary"),
        name="ssd",
    )(*args)


def _attn_kernel(*refs, tq, tk, rc, nsplit, n_qt, past, tkp, pos0, lam_init):
    lq1_ref, lk1_ref, lq2_ref, lk2_ref, nw_ref, q_ref, k_ref, v_ref, za_ref = refs[:9]
    if past:
        kp_ref, vp_ref = refs[9:11]
        rest = refs[11:]
    else:
        rest = refs[9:]
    o_ref, qz_scr, s_scr, p_scr, m_scr, acc_scr = rest
    qi = pl.program_id(2)
    rows = A_REP * tq

    lane = lax.broadcasted_iota(jnp.int32, (tq, KV_W), 1)
    zero = jnp.zeros((), BF16)
    for r in range(A_REP):
        q = q_ref[0, :, r * KV_W:(r + 1) * KV_W]
        qz_scr[0, r * tq:(r + 1) * tq, :] = jnp.where(lane < A_HEAD_DIM, q, zero)
        qz_scr[1, r * tq:(r + 1) * tq, :] = jnp.where(lane >= A_HEAD_DIM, q, zero)
    m_scr[...] = jnp.full(m_scr.shape, -jnp.inf, F32)
    acc_scr[...] = jnp.zeros(acc_scr.shape, F32)

    def with_ones(v):
        return jnp.concatenate([v, jnp.ones(v.shape, BF16)], axis=1)

    def tile(k, v, nkeys, mask_fn, v_prev=None, defer=False):
        reps = nkeys // LANES
        rb = rows // nsplit
        blk = lambda b: slice(b * rb, (b + 1) * rb)
        v1 = with_ones(v)
        v1_prev = None if v_prev is None else with_ones(v_prev)

        def qk(comp, b):
            s_scr[comp, blk(b), :nkeys] = _dot_nt(qz_scr[comp, blk(b), :], k)

        def pv(comp, b, vv, n):
            acc_scr[comp, blk(b), :] = acc_scr[comp, blk(b), :] + _dot(p_scr[comp, blk(b), :n], vv)

        def softmax(comp, b):
            for i in range(b * rb // rc, (b + 1) * rb // rc):
                rs = slice(i * rc, (i + 1) * rc)
                s = s_scr[comp, rs, :nkeys]
                if mask_fn is not None:
                    s = jnp.where(mask_fn(i), s, -jnp.inf)
                m_old = m_scr[comp, rs, :]
                m_new = jnp.maximum(m_old, jnp.max(s, axis=-1, keepdims=True))
                p_scr[comp, rs, :nkeys] = jnp.exp2((s - jnp.concatenate([m_new] * reps, axis=1)).astype(BF16))
                alpha = jnp.exp2(m_old - m_new)
                acc_scr[comp, rs, :] = jnp.concatenate([alpha, alpha], axis=1) * acc_scr[comp, rs, :]
                m_scr[comp, rs, :] = m_new

        for b in range(nsplit):
            qk(0, b)
        if v_prev is not None:
            for b in range(nsplit):
                pv(1, b, v1_prev, tk)
        for b in range(nsplit):
            qk(1, b)
        for b in range(nsplit):
            softmax(0, b)
            pv(0, b, v1, nkeys)
        for b in range(nsplit):
            softmax(1, b)
            if not defer:
                pv(1, b, v1, nkeys)

    def diag_mask(i, nkeys, n_valid, k_pos0):
        t0 = (i * rc) % tq
        qpos = pos0 + qi * tq + t0 + lax.broadcasted_iota(jnp.int32, (rc, nkeys), 0)
        kidx = lax.broadcasted_iota(jnp.int32, (rc, nkeys), 1)
        ok = ((k_pos0 + kidx) // CHUNK) <= (qpos // CHUNK)
        if n_valid < nkeys:
            ok = ok & (kidx < n_valid)
        return ok

    if past:
        def pbody(j, carry):
            sl = pl.ds(pl.multiple_of(j * tkp, tkp), tkp)
            tile(kp_ref[0, sl, :].astype(BF16), vp_ref[0, sl, :].astype(BF16), tkp, None)
            return carry

        lax.fori_loop(0, past // tkp, pbody, 0)

    def v_before(j):
        sl = pl.ds(pl.multiple_of(jnp.maximum(j - 1, 0) * tk, tk), tk)
        return v_ref[0, sl, :]

    def nbody(j, carry):
        sl = pl.ds(pl.multiple_of(j * tk, tk), tk)
        tile(k_ref[0, sl, :], v_ref[0, sl, :], tk, None, v_prev=v_before(j), defer=True)
        return carry

    n_full = (qi * tq) // tk
    if n_qt > 1:
        p_scr[1, :, :tk] = jnp.zeros((rows, tk), BF16)
        lax.fori_loop(0, n_full, nbody, 0)
    tkd = max(tk, LANES)
    for d in range(tq // tk):
        k0 = qi * tq + d * tk
        sl = pl.ds(pl.multiple_of(k0, tk), tk)
        k, v = k_ref[0, sl, :], v_ref[0, sl, :]
        if tkd > tk:
            pad = jnp.zeros((tkd - tk, KV_W), BF16)
            k, v = jnp.concatenate([k, pad], axis=0), jnp.concatenate([v, pad], axis=0)
        tile(k, v, tkd, functools.partial(diag_mask, nkeys=tkd, n_valid=tk, k_pos0=pos0 + k0),
             v_prev=v_before(n_full) if (n_qt > 1 and d == 0) else None)

    lam = (jnp.exp(jnp.sum(lq1_ref[...] * lk1_ref[...], axis=-1, keepdims=True))
           - jnp.exp(jnp.sum(lq2_ref[...] * lk2_ref[...], axis=-1, keepdims=True)) + lam_init)
    o = (acc_scr[0, :, :KV_W] / acc_scr[0, :, KV_W:] - lam * (acc_scr[1, :, :KV_W] / acc_scr[1, :, KV_W:]))
    ms = jnp.mean(o * o, axis=-1, keepdims=True)
    on = o * lax.rsqrt(ms + EPS) * nw_ref[...] * (1.0 - lam_init)
    for r in range(A_REP):
        sl = slice(r * KV_W, (r + 1) * KV_W)
        o_ref[0, :, sl] = (on[r * tq:(r + 1) * tq, :] * _silu(za_ref[0, :, sl])).astype(o_ref.dtype)


def _attention(qr, kb, vb, proj, za_off, k_past, v_past, lam_vecs, norm_w, lam_init, pos0):
    b, t, d = qr.shape
    kvh = d // QH_W
    past = 0 if k_past is None else k_past.shape[1]
    tq = _pick(t, 512, CHUNK)
    tk = tq
    tkp = _pick(past, 512, LANES) if past else 0
    rc = min(64, tq)
    zb = za_off // QH_W
    rows = A_REP * tq
    smax = max(tk, LANES, tkp)
    vec = pl.BlockSpec((1, A_HEAD_DIM), lambda i, h, j: (0, 0))
    in_specs = [vec, vec, vec, vec,
                pl.BlockSpec((1, KV_W), lambda i, h, j: (0, 0)),
                pl.BlockSpec((1, tq, QH_W), lambda i, h, j: (i, j, h)),
                pl.BlockSpec((1, t, KV_W), lambda i, h, j: (i, 0, h)),
                pl.BlockSpec((1, t, KV_W), lambda i, h, j: (i, 0, h)),
                pl.BlockSpec((1, tq, QH_W), lambda i, h, j: (i, j, zb + h))]
    args = [v.reshape(1, A_HEAD_DIM) for v in lam_vecs] + [norm_w.reshape(1, KV_W), qr, kb, vb, proj]
    if past:
        in_specs += [pl.BlockSpec((1, past, KV_W), lambda i, h, j: (i, 0, h)),
                     pl.BlockSpec((1, past, KV_W), lambda i, h, j: (i, 0, h))]
        args += [k_past, v_past]
    return pl.pallas_call(
        functools.partial(_attn_kernel, tq=tq, tk=tk, rc=rc, nsplit=max(1, rows // 512), n_qt=t // tq,
                          past=past, tkp=tkp, pos0=pos0, lam_init=lam_init),
        grid=(b, kvh, t // tq),
        in_specs=in_specs,
        out_specs=pl.BlockSpec((1, tq, QH_W), lambda i, h, j: (i, j, h)),
        out_shape=jax.ShapeDtypeStruct((b, t, d), BF16),
        scratch_shapes=[pltpu.VMEM((2, rows, KV_W), BF16),
                        pltpu.VMEM((2, rows, smax), F32),
                        pltpu.VMEM((2, rows, smax), BF16),
                        pltpu.VMEM((2, rows, LANES), F32),
                        pltpu.VMEM((2, rows, 2 * KV_W), F32)],
        compiler_params=_params("arbitrary", "arbitrary", "arbitrary"),
        name="diff_attn",
    )(*args)


def _merge_kernel(ym_ref, ya_ref, wm_ref, wa_ref, gm_ref, ga_ref, o_ref):
    pm = _dot(ym_ref[...], wm_ref[...])
    pa = _dot(ya_ref[...], wa_ref[...])
    o_ref[...] = (jax.nn.sigmoid(gm_ref[...]) * pm + jax.nn.sigmoid(ga_ref[...]) * pa).astype(o_ref.dtype)


def _merge(y_m, y_a, w_pm, w_pa, proj2d, gm_off, ga_off):
    m, km = y_m.shape
    ka = y_a.shape[1]
    d = w_pm.shape[1]
    tm = _pick(m, 256)
    tn = _pick(d, 512, LANES)
    gmb, gab = gm_off // tn, ga_off // tn
    return pl.pallas_call(
        _merge_kernel,
        grid=(d // tn, m // tm),
        in_specs=[pl.BlockSpec((tm, km), lambda j, i: (i, 0)),
                  pl.BlockSpec((tm, ka), lambda j, i: (i, 0)),
                  pl.BlockSpec((km, tn), lambda j, i: (0, j)),
                  pl.BlockSpec((ka, tn), lambda j, i: (0, j)),
                  pl.BlockSpec((tm, tn), lambda j, i: (i, gmb + j)),
                  pl.BlockSpec((tm, tn), lambda j, i: (i, gab + j))],
        out_specs=pl.BlockSpec((tm, tn), lambda j, i: (i, j)),
        out_shape=jax.ShapeDtypeStruct((m, d), BF16),
        compiler_params=_params("arbitrary", "arbitrary"),
        name="merge",
    )(y_m, y_a, w_pm, w_pa, proj2d, proj2d)


def _out_kernel(mg_ref, w_ref, x_ref, gate_ref, fw_ref, o_ref, *, tn):
    j = pl.program_id(2)
    cols = pl.ds(pl.multiple_of(j * tn, tn), tn)
    o_ref[0, :, cols] = x_ref[0] + gate_ref[0] * _dot(mg_ref[0], w_ref[...])

    @pl.when(j == pl.num_programs(2) - 1)
    def _():
        r = o_ref[0]
        ms = jnp.mean(r * r, axis=-1, keepdims=True)
        o_ref[0] = r * lax.rsqrt(ms + EPS) * fw_ref[...]


def _out_proj(merged, w_out, x, gate, final_w):
    shape = x.shape
    b, t, d = shape
    gate = gate.reshape(b, 1, d)
    per_row_gate = t < 512 and b > 1
    if per_row_gate:
        gate = jnp.broadcast_to(gate, (b, t, d)).reshape(1, b * t, d)
        x = x.reshape(1, b * t, d)
        b, t = 1, b * t
    tm = _pick(t, 512)
    tn = _pick(d, 512, LANES)
    gate_spec = (pl.BlockSpec((1, tm, tn), lambda i, r, j: (i, r, j)) if per_row_gate else
                 pl.BlockSpec((1, 1, tn), lambda i, r, j: (i, 0, j)))
    out = pl.pallas_call(
        functools.partial(_out_kernel, tn=tn),
        grid=(b, t // tm, d // tn),
        in_specs=[pl.BlockSpec((1, tm, d), lambda i, r, j: (i, r, 0)),
                  pl.BlockSpec((d, tn), lambda i, r, j: (0, j)),
                  pl.BlockSpec((1, tm, tn), lambda i, r, j: (i, r, j)),
                  gate_spec,
                  pl.BlockSpec((1, d), lambda i, r, j: (0, 0))],
        out_specs=pl.BlockSpec((1, tm, d), lambda i, r, j: (i, r, 0)),
        out_shape=jax.ShapeDtypeStruct((b, t, d), F32),
        compiler_params=_params("arbitrary", "arbitrary", "arbitrary"),
        name="out_proj",
    )(merged.reshape(b, t, d), w_out, x, gate, final_w.reshape(1, d))
    return out.reshape(shape)


def _layer_path(x, mod, k_past, v_past, conv_state, ssm_state, pos0, w, lam_init, final_w):
    b, t, d = x.shape
    d_inner = 2 * d
    groups = d_inner // GROUP_W
    bcw = groups * M_STATE
    conv_dim = d_inner + 2 * bcw
    kvw = (d // QH_W) * KV_W
    z_off, xbc_off = 0, d_inner
    k_off, v_off, za_off, gm_off, ga_off = d, d + kvw, d + 2 * kvw, 2 * d + 2 * kvw, 3 * d + 2 * kvw
    assert k_off % kvw == 0 and za_off % QH_W == 0, "consumer column blocks must be block-aligned"
    shift, scale, gate = mod[:, :d], mod[:, d:2 * d], mod[:, 2 * d:]

    h = _prenorm(x, w['norm_w'], scale, shift).reshape(b * t, d)
    proj_m = _matmul(h, w['w_in_m'], "in_proj_m").reshape(b, t, -1)
    proj_a = _matmul(h, w['w_in_a'], "in_proj_a").reshape(b, t, -1)
    dt = _dt_proj(h, w['w_dt'], w['dt_bias']).reshape(b, t, -1)

    pos = pos0 + jnp.arange(t, dtype=jnp.int32)
    qr, k_new, v_new, kb, vb = _rope_kv(proj_a, pos, d, kvw, k_off // kvw, v_off // kvw)

    y_m, ssm_new = _ssd(proj_m, dt, conv_state, ssm_state, w['conv_w'], w['conv_b'], w['a_log'], w['d_skip'],
                        w['mamba_norm_w'], z_off, xbc_off, d_inner)
    halo = M_CONV - 1
    conv_new = proj_m[:, t - halo:, xbc_off:xbc_off + conv_dim]

    y_a = _attention(qr, kb, vb, proj_a, za_off, k_past, v_past,
                     (w['lam_q1'], w['lam_k1'], w['lam_q2'], w['lam_k2']), w['attn_norm_w'], lam_init, pos0)

    merged = _merge(y_m.reshape(b * t, d_inner), y_a.reshape(b * t, d), w['w_proj_m'], w['w_proj_a'],
                    proj_a.reshape(b * t, -1), gm_off, ga_off)
    y = _out_proj(merged, w['w_out'], x, gate, final_w)
    kvh = d // QH_W
    return (y, k_new.reshape(b, t, kvh, 2, A_HEAD_DIM), v_new.reshape(b, t, kvh, KV_W), conv_new, ssm_new)


def kernel(x_prompt, x_sample, cache_k, cache_v, state_conv, state_ssm, c_prompt, c_sample,
           w_ada, b_ada, norm_w, w_in, conv_w, conv_b, dt_bias, a_log, d_skip, mamba_norm_w,
           lam_q1, lam_k1, lam_q2, lam_k2, attn_norm_w, w_proj_m, w_proj_a, w_out, final_norm_w):
    depth = w_in.shape[0]
    assert depth == 1, "the final norm is fused into the single layer's output projection"
    bp, d = c_prompt.shape
    bs = c_sample.shape[0]
    past = cache_k.shape[2]
    d_inner = 2 * d
    groups = d_inner // GROUP_W
    heads = groups * M_HPG
    conv_dim = d_inner + 2 * groups * M_STATE
    kvw = (d // QH_W) * KV_W
    sizes = (d_inner, conv_dim, heads, d, kvw, kvw, d, d, d)
    offs = [0]
    for s in sizes:
        offs.append(offs[-1] + s)

    i = 0
    wi = w_in[i]
    w = {
        'w_in_m': wi[:, :offs[2]].astype(BF16),
        'w_dt': wi[:, offs[2]:offs[3]].astype(BF16),
        'w_in_a': wi[:, offs[3]:].astype(BF16),
        'norm_w': norm_w[i], 'conv_w': conv_w[i], 'conv_b': conv_b[i], 'dt_bias': dt_bias[i], 'a_log': a_log[i],
        'd_skip': d_skip[i], 'mamba_norm_w': mamba_norm_w[i], 'lam_q1': lam_q1[i], 'lam_k1': lam_k1[i],
        'lam_q2': lam_q2[i], 'lam_k2': lam_k2[i], 'attn_norm_w': attn_norm_w[i],
        'w_proj_m': w_proj_m[i].astype(BF16), 'w_proj_a': w_proj_a[i].astype(BF16), 'w_out': w_out[i].astype(BF16),
    }
    lam_init = 0.8 - 0.6 * math.exp(-0.3 * i)
    pad_rows = -(bp + bs) % 16
    c_all = jnp.concatenate([c_prompt, c_sample, jnp.zeros((pad_rows, d), F32)], axis=0)
    mod = _ada_mod(c_all, w_ada[i], b_ada[i])

    yp, kp, vp, cp, sp = _layer_path(x_prompt, mod[:bp], None, None, None, None, 0, w, lam_init, final_norm_w)
    ck = cache_k[i].reshape(bs, past, kvw)
    cv = cache_v[i].reshape(bs, past, kvw)
    ys, kq, vq, cq, sq = _layer_path(x_sample, mod[bp:bp + bs], ck, cv, state_conv[i], state_ssm[i], past, w, lam_init,
                                     final_norm_w)
    st = lambda a: a[None]
    return (yp, ys, st(kp), st(vp), st(cp), st(sp), st(kq), st(vq), st(cq), st(sq))
```

```python
import functools
import math

import jax
import jax.numpy as jnp
from jax import lax
from jax.experimental import pallas as pl
from jax.experimental.pallas import tpu as pltpu

F32 = jnp.float32
BF16 = jnp.bfloat16

CHUNK = 64
EPS = 1e-6
M_HEAD_DIM = 64
M_HPG = 16
M_STATE = 128
M_CONV = 4
GROUP_W = M_HPG * M_HEAD_DIM
A_HEAD_DIM = 64
A_REP = 4
KV_W = 2 * A_HEAD_DIM
QH_W = A_REP * KV_W
ROPE_DIM = 16
ROPE_THETA = 500000.0
LANES = 128
VMEM_LIMIT_BYTES = 56 * 1024 * 1024


def _pick(n, target, mult=8):
    if n <= target:
        return n
    for t in range(target, 0, -1):
        if n % t == 0 and t % mult == 0:
            return t
    return n


def _params(*sem):
    return pltpu.CompilerParams(dimension_semantics=sem, vmem_limit_bytes=VMEM_LIMIT_BYTES)


def _silu(x):
    return x * jax.nn.sigmoid(x)


def _split_hi_lo(x):
    hi = x.astype(BF16)
    lo = (x - hi.astype(F32)).astype(BF16)
    return hi, lo


def _dot(a, b):
    return jnp.dot(a, b, preferred_element_type=F32)


def _dot_nt(a, b):
    return lax.dot_general(a, b, (((1,), (1,)), ((), ())), preferred_element_type=F32)


def _dot_tn(a, b):
    return lax.dot_general(a, b, (((0,), (0,)), ((), ())), preferred_element_type=F32)


def _ada_kernel(c_ref, w_ref, b_ref, o_ref):
    a = _silu(c_ref[...]).astype(BF16)
    o_ref[...] = _dot(a, w_ref[...].astype(BF16)) + b_ref[...]


def _ada_mod(c, w_ada, b_ada):
    m, d = c.shape
    n = w_ada.shape[1]
    tn = _pick(n, 512, LANES)
    return pl.pallas_call(
        _ada_kernel,
        grid=(n // tn,),
        in_specs=[pl.BlockSpec((m, d), lambda j: (0, 0)),
                  pl.BlockSpec((d, tn), lambda j: (0, j)),
                  pl.BlockSpec((1, tn), lambda j: (0, j))],
        out_specs=pl.BlockSpec((m, tn), lambda j: (0, j)),
        out_shape=jax.ShapeDtypeStruct((m, n), F32),
        compiler_params=_params("arbitrary"),
        name="ada_mod",
    )(c, w_ada, b_ada.reshape(1, n))


def _prenorm_kernel(x_ref, nw_ref, sc_ref, sh_ref, o_ref):
    x = x_ref[0]
    ms = jnp.mean(x * x, axis=-1, keepdims=True)
    y = x * lax.rsqrt(ms + EPS) * nw_ref[...]
    o_ref[0] = (y * (1.0 + sc_ref[0]) + sh_ref[0]).astype(o_ref.dtype)


def _prenorm(x, norm_w, scale, shift):
    b, t, d = x.shape
    tr = _pick(t, 256)
    return pl.pallas_call(
        _prenorm_kernel,
        grid=(b, t // tr),
        in_specs=[pl.BlockSpec((1, tr, d), lambda i, j: (i, j, 0)),
                  pl.BlockSpec((1, d), lambda i, j: (0, 0)),
                  pl.BlockSpec((1, 1, d), lambda i, j: (i, 0, 0)),
                  pl.BlockSpec((1, 1, d), lambda i, j: (i, 0, 0))],
        out_specs=pl.BlockSpec((1, tr, d), lambda i, j: (i, j, 0)),
        out_shape=jax.ShapeDtypeStruct((b, t, d), BF16),
        compiler_params=_params("arbitrary", "arbitrary"),
        name="prenorm",
    )(x, norm_w.reshape(1, d), scale.reshape(b, 1, d), shift.reshape(b, 1, d))


def _mm_kernel(a_ref, w_ref, o_ref):
    o_ref[...] = _dot(a_ref[...], w_ref[...]).astype(o_ref.dtype)


def _matmul(a, w, name):
    m, k = a.shape
    n = w.shape[1]
    tm = _pick(m, 512)
    tn = _pick(n, 1024, LANES)
    return pl.pallas_call(
        _mm_kernel,
        grid=(n // tn, m // tm),
        in_specs=[pl.BlockSpec((tm, k), lambda j, i: (i, 0)),
                  pl.BlockSpec((k, tn), lambda j, i: (0, j))],
        out_specs=pl.BlockSpec((tm, tn), lambda j, i: (i, j)),
        out_shape=jax.ShapeDtypeStruct((m, n), F32),
        compiler_params=_params("arbitrary", "arbitrary"),
        name=name,
    )(a, w)


def _dt_kernel(a_ref, w_ref, b_ref, o_ref):
    x = _dot(a_ref[...], w_ref[...]) + b_ref[...]
    o_ref[...] = jnp.maximum(x, 0.0) + jnp.log1p(jnp.exp(-jnp.abs(x)))


def _dt_proj(a, w_dt, dt_bias):
    m, k = a.shape
    n = w_dt.shape[1]
    tm = _pick(m, 512)
    return pl.pallas_call(
        _dt_kernel,
        grid=(m // tm,),
        in_specs=[pl.BlockSpec((tm, k), lambda i: (i, 0)),
                  pl.BlockSpec((k, n), lambda i: (0, 0)),
                  pl.BlockSpec((1, n), lambda i: (0, 0))],
        out_specs=pl.BlockSpec((tm, n), lambda i: (i, 0)),
        out_shape=jax.ShapeDtypeStruct((m, n), F32),
        compiler_params=_params("arbitrary"),
        name="dt_proj",
    )(a, w_dt, dt_bias.reshape(1, n))


def _rope_slab(x, cos, sin_up, sin_dn):
    return x * cos + pltpu.roll(x, LANES - ROPE_DIM // 2, 1) * sin_up + pltpu.roll(x, ROPE_DIM // 2, 1) * sin_dn


def _rope_kernel(q_ref, k_ref, v_ref, cos_ref, sup_ref, sdn_ref, qo_ref, ko_ref, vo_ref, kb_ref, vb_ref, *, q_scale):
    cos, sup, sdn = cos_ref[...], sup_ref[...], sdn_ref[...]
    for s in range(q_ref.shape[2] // LANES):
        sl = slice(s * LANES, (s + 1) * LANES)
        qo_ref[0, :, sl] = (_rope_slab(q_ref[0, :, sl], cos, sup, sdn) * q_scale).astype(qo_ref.dtype)
    for s in range(k_ref.shape[2] // LANES):
        sl = slice(s * LANES, (s + 1) * LANES)
        k = _rope_slab(k_ref[0, :, sl], cos, sup, sdn)
        ko_ref[0, :, sl] = k
        kb_ref[0, :, sl] = k.astype(kb_ref.dtype)
    v = v_ref[...]
    vo_ref[...] = v
    vb_ref[...] = v.astype(vb_ref.dtype)


def _rope_tables(pos):
    half = ROPE_DIM // 2
    inv = ROPE_THETA ** (-jnp.arange(half, dtype=F32) / half)
    ang = pos.astype(F32)[:, None] * inv[None, :]
    cos, sin = jnp.cos(ang), jnp.sin(ang)
    t = pos.shape[0]
    ones = jnp.ones((t, A_HEAD_DIM - ROPE_DIM), F32)
    zeros = jnp.zeros((t, A_HEAD_DIM - ROPE_DIM), F32)
    zh = jnp.zeros((t, half), F32)
    cos64 = jnp.concatenate([cos, cos, ones], axis=1)
    sup64 = jnp.concatenate([-sin, zh, zeros], axis=1)
    sdn64 = jnp.concatenate([zh, sin, zeros], axis=1)
    rep = lambda a: jnp.concatenate([a, a], axis=1)
    return rep(cos64), rep(sup64), rep(sdn64)


def _rope_kv(proj, pos, d, kvw, k_blk, v_blk):
    b, t, _ = proj.shape
    tr = _pick(t, 256)
    cos, sup, sdn = _rope_tables(pos)
    tab = pl.BlockSpec((tr, LANES), lambda i, j: (j, 0))
    kv_out = pl.BlockSpec((1, tr, kvw), lambda i, j: (i, j, 0))
    return pl.pallas_call(
        functools.partial(_rope_kernel, q_scale=A_HEAD_DIM ** -0.5 * math.log2(math.e)),
        grid=(b, t // tr),
        in_specs=[pl.BlockSpec((1, tr, d), lambda i, j: (i, j, 0)),
                  pl.BlockSpec((1, tr, kvw), lambda i, j: (i, j, k_blk)),
                  pl.BlockSpec((1, tr, kvw), lambda i, j: (i, j, v_blk)),
                  tab, tab, tab],
        out_specs=[pl.BlockSpec((1, tr, d), lambda i, j: (i, j, 0))] + [kv_out] * 4,
        out_shape=[jax.ShapeDtypeStruct((b, t, d), BF16),
                   jax.ShapeDtypeStruct((b, t, kvw), F32),
                   jax.ShapeDtypeStruct((b, t, kvw), F32),
                   jax.ShapeDtypeStruct((b, t, kvw), BF16),
                   jax.ShapeDtypeStruct((b, t, kvw), BF16)],
        compiler_params=_params("arbitrary", "arbitrary"),
        name="rope_kv",
    )(proj, proj, proj, cos, sup, sdn)


def _ssd_kernel(*refs, tl, has_state):
    (alog_ref, alr_ref, dsk_ref, mnw_ref, cwx_ref, cwb_ref, cwc_ref, cbx_ref, cbb_ref, cbc_ref,
     z_ref, x_ref, bm_ref, cm_ref, dt_ref, dtr_ref) = refs[:16]
    if has_state:
        csx_ref, csb_ref, csc_ref, s0_ref = refs[16:20]
        rest = refs[20:]
    else:
        rest = refs[16:]
    y_ref, sout_ref, xbuf, bbuf, cbuf, st_scr = rest
    g = pl.program_id(1)
    c = pl.program_id(2)
    L = CHUNK
    nsub = tl // L
    halo = M_CONV - 1
    base = 8

    @pl.when(c == 0)
    def _():
        if has_state:
            xbuf[base - halo:base, :] = csx_ref[0]
            bbuf[base - halo:base, :] = csb_ref[0]
            cbuf[base - halo:base, :] = csc_ref[0]
            st_scr[...] = s0_ref[0].reshape(GROUP_W, M_STATE).T
        else:
            xbuf[0:base, :] = jnp.zeros((base, GROUP_W), F32)
            bbuf[0:base, :] = jnp.zeros((base, M_STATE), F32)
            cbuf[0:base, :] = jnp.zeros((base, M_STATE), F32)
            st_scr[...] = jnp.zeros_like(st_scr)

    xbuf[base:base + tl, :] = x_ref[0]
    bbuf[base:base + tl, :] = bm_ref[0]
    cbuf[base:base + tl, :] = cm_ref[0]

    r64 = lax.broadcasted_iota(jnp.int32, (L, L), 0)
    c64 = lax.broadcasted_iota(jnp.int32, (L, L), 1)
    tril = (c64 <= r64).astype(BF16)
    heads = dt_ref.shape[2]
    hrow = lax.broadcasted_iota(jnp.int32, (heads, GROUP_W), 0)
    hcol = lax.broadcasted_iota(jnp.int32, (heads, GROUP_W), 1)
    expand = (hrow == g * M_HPG + hcol // M_HEAD_DIM).astype(BF16)
    br = lax.broadcasted_iota(jnp.int32, (LANES, LANES), 0)
    bc = lax.broadcasted_iota(jnp.int32, (LANES, LANES), 1)
    same_half = (br // L) == (bc // L)
    triu2 = (same_half & ((br % L) <= (bc % L))).astype(BF16)
    pr = lax.broadcasted_iota(jnp.int32, (L, LANES), 0)
    pc = lax.broadcasted_iota(jnp.int32, (L, LANES), 1)
    causal2 = (pc % L) <= pr

    a_col = -jnp.exp(alog_ref[...])
    a_row = -jnp.exp(alr_ref[0])
    n_rows = dtr_ref.shape[2]
    da_r = dtr_ref[0, 0] * jnp.concatenate([a_row] * (n_rows // 8), axis=0)
    hi, lo = _split_hi_lo(da_r)
    acum_r_all = _dot(hi, triu2) + _dot(lo, triu2)

    def conv(buf, w_ref, b_ref, i):
        acc = b_ref[...] + w_ref[M_CONV - 1:M_CONV, :] * buf[base + i * L:base + (i + 1) * L, :]
        for k in range(M_CONV - 1):
            off = base - halo + k + i * L
            acc = acc + w_ref[k:k + 1, :] * buf[off:off + L, :]
        return _silu(acc)

    for i in range(nsub):
        rows = slice(i * L, (i + 1) * L)
        x = conv(xbuf, cwx_ref, cbx_ref, i)
        bm = conv(bbuf, cwb_ref, cbb_ref, i).astype(BF16)
        cm = conv(cbuf, cwc_ref, cbc_ref, i).astype(BF16)
        dt = dt_ref[0, rows, :]
        da = dt * a_col
        hi, lo = _split_hi_lo(da)
        acum = _dot(tril, hi) + _dot(tril, lo)
        hi, lo = _split_hi_lo(acum)
        acum_x = _dot(hi, expand) + _dot(lo, expand)
        hi, lo = _split_hi_lo(dt)
        dt_x = _dot(hi, expand) + _dot(lo, expand)
        xdt = x * dt_x
        xdt_b = xdt.astype(BF16)
        alast_x = acum_x[L - 1:L, :]
        cb2 = _dot_nt(cm, jnp.concatenate([bm, bm], axis=0))
        st_b = st_scr[...].astype(BF16)
        y_off = _dot(cm, st_b) * jnp.exp(acum_x)
        y_parts = []
        for jj in range(GROUP_W // LANES):
            ls = slice(jj * LANES, (jj + 1) * LANES)
            seg = acum_x[:, ls] - acum_r_all[i * 8 + jj:i * 8 + jj + 1, :]
            m2 = (jnp.where(causal2, jnp.exp(seg), 0.0) * cb2).astype(BF16)
            x2 = xdt_b[:, ls]
            rhs = jnp.where(same_half, jnp.concatenate([x2, x2], axis=0), jnp.zeros((), BF16))
            y_parts.append(_dot(m2, rhs))
        y = jnp.concatenate(y_parts, axis=1) + y_off + dsk_ref[0] * x
        y = y * _silu(z_ref[0, rows, :])
        ms = jnp.mean(y * y, axis=-1, keepdims=True)
        y_ref[0, rows, :] = (y * lax.rsqrt(ms + EPS) * mnw_ref[0]).astype(y_ref.dtype)
        xw = (xdt * jnp.exp(alast_x - acum_x)).astype(BF16)
        st_scr[...] = st_scr[...] * jnp.exp(alast_x) + _dot_tn(bm, xw)

    tx = xbuf[base + tl - halo:base + tl, :]
    tb = bbuf[base + tl - halo:base + tl, :]
    tc = cbuf[base + tl - halo:base + tl, :]
    xbuf[base - halo:base, :] = tx
    bbuf[base - halo:base, :] = tb
    cbuf[base - halo:base, :] = tc

    @pl.when(c == pl.num_programs(2) - 1)
    def _():
        sout_ref[0] = st_scr[...].T.reshape(M_HPG, M_HEAD_DIM, M_STATE)


def _ssd(proj, dt, conv_state, ssm_state, conv_w, conv_b, a_log, d_skip, mnorm_w, z_off, xbc_off, d_inner):
    b, t, _ = proj.shape
    groups = d_inner // GROUP_W
    heads = groups * M_HPG
    bcw = groups * M_STATE
    has_state = conv_state is not None
    tl = _pick(t, 256, CHUNK)
    nsub = tl // CHUNK
    nc = t // tl
    dtr = dt.reshape(b, t // CHUNK, CHUNK, groups, M_HPG // 2, 2).transpose(0, 3, 1, 4, 5, 2)
    dtr = dtr.reshape(b, groups, (t // CHUNK) * 8, LANES)
    n_rows = nsub * 8
    if n_rows < 16:
        dtr = jnp.concatenate([dtr, jnp.zeros_like(dtr)], axis=2)
        n_rows = 16
    alr = jnp.repeat(a_log.reshape(groups, M_HPG // 2, 2), CHUNK, axis=2)
    dsk = jnp.repeat(d_skip.reshape(groups, 1, M_HPG), M_HEAD_DIM, axis=2)
    mnw = mnorm_w.reshape(groups, 1, GROUP_W)
    xb, bb, cb = xbc_off // GROUP_W, (xbc_off + d_inner) // M_STATE, (xbc_off + d_inner + bcw) // M_STATE
    cxb, cbb, ccb = 0, d_inner // M_STATE, (d_inner + bcw) // M_STATE
    zb = z_off // GROUP_W
    conv_b2 = conv_b.reshape(1, -1)
    gmap = lambda blk: (lambda i, g, c: (0, blk + g))
    in_specs = [
        pl.BlockSpec((1, heads), lambda i, g, c: (0, 0)),
        pl.BlockSpec((1, 8, LANES), lambda i, g, c: (g, 0, 0)),
        pl.BlockSpec((1, 1, GROUP_W), lambda i, g, c: (g, 0, 0)),
        pl.BlockSpec((1, 1, GROUP_W), lambda i, g, c: (g, 0, 0)),
        pl.BlockSpec((M_CONV, GROUP_W), gmap(cxb)),
        pl.BlockSpec((M_CONV, M_STATE), gmap(cbb)),
        pl.BlockSpec((M_CONV, M_STATE), gmap(ccb)),
        pl.BlockSpec((1, GROUP_W), gmap(cxb)),
        pl.BlockSpec((1, M_STATE), gmap(cbb)),
        pl.BlockSpec((1, M_STATE), gmap(ccb)),
        pl.BlockSpec((1, tl, GROUP_W), lambda i, g, c: (i, c, zb + g)),
        pl.BlockSpec((1, tl, GROUP_W), lambda i, g, c: (i, c, xb + g)),
        pl.BlockSpec((1, tl, M_STATE), lambda i, g, c: (i, c, bb + g)),
        pl.BlockSpec((1, tl, M_STATE), lambda i, g, c: (i, c, cb + g)),
        pl.BlockSpec((1, tl, heads), lambda i, g, c: (i, c, 0)),
        pl.BlockSpec((1, 1, n_rows, LANES), lambda i, g, c: (i, g, c, 0)),
    ]
    args = [a_log.reshape(1, heads), alr, dsk, mnw, conv_w, conv_w, conv_w, conv_b2, conv_b2, conv_b2,
            proj, proj, proj, proj, dt.reshape(b, t, heads), dtr]
    if has_state:
        halo = M_CONV - 1
        in_specs += [
            pl.BlockSpec((1, halo, GROUP_W), lambda i, g, c: (i, 0, cxb + g)),
            pl.BlockSpec((1, halo, M_STATE), lambda i, g, c: (i, 0, cbb + g)),
            pl.BlockSpec((1, halo, M_STATE), lambda i, g, c: (i, 0, ccb + g)),
            pl.BlockSpec((1, M_HPG, M_HEAD_DIM, M_STATE), lambda i, g, c: (i, g, 0, 0)),
        ]
        args += [conv_state, conv_state, conv_state, ssm_state]
    return pl.pallas_call(
        functools.partial(_ssd_kernel, tl=tl, has_state=has_state),
        grid=(b, groups, nc),
        in_specs=in_specs,
        out_specs=[pl.BlockSpec((1, tl, GROUP_W), lambda i, g, c: (i, c, g)),
                   pl.BlockSpec((1, M_HPG, M_HEAD_DIM, M_STATE), lambda i, g, c: (i, g, 0, 0))],
        out_shape=[jax.ShapeDtypeStruct((b, t, d_inner), BF16),
                   jax.ShapeDtypeStruct((b, heads, M_HEAD_DIM, M_STATE), F32)],
        scratch_shapes=[pltpu.VMEM((tl + 8, GROUP_W), F32),
                        pltpu.VMEM((tl + 8, M_STATE), F32),
                        pltpu.VMEM((tl + 8, M_STATE), F32),
                        pltpu.VMEM((M_STATE, GROUP_W), F32)],
        compiler_params=_params("arbitrary", "arbitrary", "arbitrary"),
        name="ssd",
    )(*args)


def _attn_kernel(*refs, tq, tk, rc, n_qt, past, tkp, pos0, lam_init):
    lq1_ref, lk1_ref, lq2_ref, lk2_ref, nw_ref, q_ref, k_ref, v_ref, za_ref = refs[:9]
    if past:
        kp_ref, vp_ref = refs[9:11]
        rest = refs[11:]
    else:
        rest = refs[9:]
    o_ref, qz_scr, s_scr, p_scr, m_scr, l_scr, acc_scr = rest
    qi = pl.program_id(2)
    rows = A_REP * tq

    lane = lax.broadcasted_iota(jnp.int32, (tq, KV_W), 1)
    zero = jnp.zeros((), BF16)
    for r in range(A_REP):
        q = q_ref[0, :, r * KV_W:(r + 1) * KV_W]
        qz_scr[0, r * tq:(r + 1) * tq, :] = jnp.where(lane < A_HEAD_DIM, q, zero)
        qz_scr[1, r * tq:(r + 1) * tq, :] = jnp.where(lane >= A_HEAD_DIM, q, zero)
    m_scr[...] = jnp.full(m_scr.shape, -jnp.inf, F32)
    l_scr[...] = jnp.zeros(l_scr.shape, F32)
    acc_scr[...] = jnp.zeros(acc_scr.shape, F32)

    def tile(k, v, nkeys, mask_fn):
        reps = nkeys // LANES
        for comp in range(2):
            s_scr[comp, :, :nkeys] = _dot_nt(qz_scr[comp], k)
        for comp in range(2):
            for i in range(rows // rc):
                rs = slice(i * rc, (i + 1) * rc)
                s = s_scr[comp, rs, :nkeys]
                if mask_fn is not None:
                    s = jnp.where(mask_fn(i), s, -jnp.inf)
                m_old = m_scr[comp, rs, :]
                m_new = jnp.maximum(m_old, jnp.max(s, axis=-1, keepdims=True))
                p = jnp.exp2((s - jnp.concatenate([m_new] * reps, axis=1)).astype(BF16))
                alpha = jnp.exp2(m_old - m_new)
                psum = p[:, :LANES]
                for u in range(1, reps):
                    psum = psum + p[:, u * LANES:(u + 1) * LANES]
                l_scr[comp, rs, :] = alpha * l_scr[comp, rs, :] + psum.astype(F32)
                acc_scr[comp, rs, :] = alpha * acc_scr[comp, rs, :]
                m_scr[comp, rs, :] = m_new
                p_scr[comp, rs, :nkeys] = p
            acc_scr[comp] = acc_scr[comp] + _dot(p_scr[comp, :, :nkeys], v)

    def diag_mask(i, nkeys, n_valid, k_pos0):
        t0 = (i * rc) % tq
        qpos = pos0 + qi * tq + t0 + lax.broadcasted_iota(jnp.int32, (rc, nkeys), 0)
        kidx = lax.broadcasted_iota(jnp.int32, (rc, nkeys), 1)
        ok = ((k_pos0 + kidx) // CHUNK) <= (qpos // CHUNK)
        if n_valid < nkeys:
            ok = ok & (kidx < n_valid)
        return ok

    if past:
        def pbody(j, carry):
            sl = pl.ds(pl.multiple_of(j * tkp, tkp), tkp)
            tile(kp_ref[0, sl, :].astype(BF16), vp_ref[0, sl, :].astype(BF16), tkp, None)
            return carry

        lax.fori_loop(0, past // tkp, pbody, 0)

    def nbody(j, carry):
        sl = pl.ds(pl.multiple_of(j * tk, tk), tk)
        tile(k_ref[0, sl, :], v_ref[0, sl, :], tk, None)
        return carry

    if n_qt > 1:
        lax.fori_loop(0, (qi * tq) // tk, nbody, 0)
    tkd = max(tk, LANES)
    for d in range(tq // tk):
        k0 = qi * tq + d * tk
        sl = pl.ds(pl.multiple_of(k0, tk), tk)
        k, v = k_ref[0, sl, :], v_ref[0, sl, :]
        if tkd > tk:
            pad = jnp.zeros((tkd - tk, KV_W), BF16)
            k, v = jnp.concatenate([k, pad], axis=0), jnp.concatenate([v, pad], axis=0)
        tile(k, v, tkd, functools.partial(diag_mask, nkeys=tkd, n_valid=tk, k_pos0=pos0 + k0))

    lam = (jnp.exp(jnp.sum(lq1_ref[...] * lk1_ref[...], axis=-1, keepdims=True))
           - jnp.exp(jnp.sum(lq2_ref[...] * lk2_ref[...], axis=-1, keepdims=True)) + lam_init)
    l0 = jnp.sum(l_scr[0], axis=-1, keepdims=True)
    l1 = jnp.sum(l_scr[1], axis=-1, keepdims=True)
    o = acc_scr[0] / l0 - lam * (acc_scr[1] / l1)
    ms = jnp.mean(o * o, axis=-1, keepdims=True)
    on = o * lax.rsqrt(ms + EPS) * nw_ref[...] * (1.0 - lam_init)
    for r in range(A_REP):
        sl = slice(r * KV_W, (r + 1) * KV_W)
        o_ref[0, :, sl] = (on[r * tq:(r + 1) * tq, :] * _silu(za_ref[0, :, sl])).astype(o_ref.dtype)


def _attention(qr, kb, vb, proj, za_off, k_past, v_past, lam_vecs, norm_w, lam_init, pos0):
    b, t, d = qr.shape
    kvh = d // QH_W
    past = 0 if k_past is None else k_past.shape[1]
    tq = _pick(t, 512, CHUNK)
    tk = tq
    tkp = _pick(past, 512, LANES) if past else 0
    rc = min(64, tq)
    zb = za_off // QH_W
    rows = A_REP * tq
    smax = max(tk, LANES, tkp)
    vec = pl.BlockSpec((1, A_HEAD_DIM), lambda i, h, j: (0, 0))
    in_specs = [vec, vec, vec, vec,
                pl.BlockSpec((1, KV_W), lambda i, h, j: (0, 0)),
                pl.BlockSpec((1, tq, QH_W), lambda i, h, j: (i, j, h)),
                pl.BlockSpec((1, t, KV_W), lambda i, h, j: (i, 0, h)),
                pl.BlockSpec((1, t, KV_W), lambda i, h, j: (i, 0, h)),
                pl.BlockSpec((1, tq, QH_W), lambda i, h, j: (i, j, zb + h))]
    args = [v.reshape(1, A_HEAD_DIM) for v in lam_vecs] + [norm_w.reshape(1, KV_W), qr, kb, vb, proj]
    if past:
        in_specs += [pl.BlockSpec((1, past, KV_W), lambda i, h, j: (i, 0, h)),
                     pl.BlockSpec((1, past, KV_W), lambda i, h, j: (i, 0, h))]
        args += [k_past, v_past]
    return pl.pallas_call(
        functools.partial(_attn_kernel, tq=tq, tk=tk, rc=rc, n_qt=t // tq, past=past, tkp=tkp, pos0=pos0,
                          lam_init=lam_init),
        grid=(b, kvh, t // tq),
        in_specs=in_specs,
        out_specs=pl.BlockSpec((1, tq, QH_W), lambda i, h, j: (i, j, h)),
        out_shape=jax.ShapeDtypeStruct((b, t, d), BF16),
        scratch_shapes=[pltpu.VMEM((2, rows, KV_W), BF16),
                        pltpu.VMEM((2, rows, smax), F32),
                        pltpu.VMEM((2, rows, smax), BF16),
                        pltpu.VMEM((2, rows, LANES), F32),
                        pltpu.VMEM((2, rows, LANES), F32),
                        pltpu.VMEM((2, rows, KV_W), F32)],
        compiler_params=_params("arbitrary", "arbitrary", "arbitrary"),
        name="diff_attn",
    )(*args)


def _merge_kernel(ym_ref, ya_ref, wm_ref, wa_ref, gm_ref, ga_ref, o_ref):
    pm = _dot(ym_ref[...], wm_ref[...])
    pa = _dot(ya_ref[...], wa_ref[...])
    o_ref[...] = (jax.nn.sigmoid(gm_ref[...]) * pm + jax.nn.sigmoid(ga_ref[...]) * pa).astype(o_ref.dtype)


def _merge(y_m, y_a, w_pm, w_pa, proj2d, gm_off, ga_off):
    m, km = y_m.shape
    ka = y_a.shape[1]
    d = w_pm.shape[1]
    tm = _pick(m, 256)
    tn = _pick(d, 512, LANES)
    gmb, gab = gm_off // tn, ga_off // tn
    return pl.pallas_call(
        _merge_kernel,
        grid=(d // tn, m // tm),
        in_specs=[pl.BlockSpec((tm, km), lambda j, i: (i, 0)),
                  pl.BlockSpec((tm, ka), lambda j, i: (i, 0)),
                  pl.BlockSpec((km, tn), lambda j, i: (0, j)),
                  pl.BlockSpec((ka, tn), lambda j, i: (0, j)),
                  pl.BlockSpec((tm, tn), lambda j, i: (i, gmb + j)),
                  pl.BlockSpec((tm, tn), lambda j, i: (i, gab + j))],
        out_specs=pl.BlockSpec((tm, tn), lambda j, i: (i, j)),
        out_shape=jax.ShapeDtypeStruct((m, d), BF16),
        compiler_params=_params("arbitrary", "arbitrary"),
        name="merge",
    )(y_m, y_a, w_pm, w_pa, proj2d, proj2d)


def _out_kernel(mg_ref, w_ref, x_ref, gate_ref, fw_ref, o_ref, *, tn):
    j = pl.program_id(2)
    cols = pl.ds(pl.multiple_of(j * tn, tn), tn)
    o_ref[0, :, cols] = x_ref[0] + gate_ref[0] * _dot(mg_ref[0], w_ref[...])

    @pl.when(j == pl.num_programs(2) - 1)
    def _():
        r = o_ref[0]
        ms = jnp.mean(r * r, axis=-1, keepdims=True)
        o_ref[0] = r * lax.rsqrt(ms + EPS) * fw_ref[...]


def _out_proj(merged, w_out, x, gate, final_w):
    shape = x.shape
    b, t, d = shape
    gate = gate.reshape(b, 1, d)
    per_row_gate = t < 512 and b > 1
    if per_row_gate:
        gate = jnp.broadcast_to(gate, (b, t, d)).reshape(1, b * t, d)
        x = x.reshape(1, b * t, d)
        b, t = 1, b * t
    tm = _pick(t, 512)
    tn = _pick(d, 512, LANES)
    gate_spec = (pl.BlockSpec((1, tm, tn), lambda i, r, j: (i, r, j)) if per_row_gate else
                 pl.BlockSpec((1, 1, tn), lambda i, r, j: (i, 0, j)))
    out = pl.pallas_call(
        functools.partial(_out_kernel, tn=tn),
        grid=(b, t // tm, d // tn),
        in_specs=[pl.BlockSpec((1, tm, d), lambda i, r, j: (i, r, 0)),
                  pl.BlockSpec((d, tn), lambda i, r, j: (0, j)),
                  pl.BlockSpec((1, tm, tn), lambda i, r, j: (i, r, j)),
                  gate_spec,
                  pl.BlockSpec((1, d), lambda i, r, j: (0, 0))],
        out_specs=pl.BlockSpec((1, tm, d), lambda i, r, j: (i, r, 0)),
        out_shape=jax.ShapeDtypeStruct((b, t, d), F32),
        compiler_params=_params("arbitrary", "arbitrary", "arbitrary"),
        name="out_proj",
    )(merged.reshape(b, t, d), w_out, x, gate, final_w.reshape(1, d))
    return out.reshape(shape)


def _layer_path(x, mod, k_past, v_past, conv_state, ssm_state, pos0, w, lam_init, final_w):
    b, t, d = x.shape
    d_inner = 2 * d
    groups = d_inner // GROUP_W
    bcw = groups * M_STATE
    conv_dim = d_inner + 2 * bcw
    kvw = (d // QH_W) * KV_W
    z_off, xbc_off = 0, d_inner
    k_off, v_off, za_off, gm_off, ga_off = d, d + kvw, d + 2 * kvw, 2 * d + 2 * kvw, 3 * d + 2 * kvw
    assert k_off % kvw == 0 and za_off % QH_W == 0, "consumer column blocks must be block-aligned"
    shift, scale, gate = mod[:, :d], mod[:, d:2 * d], mod[:, 2 * d:]

    h = _prenorm(x, w['norm_w'], scale, shift).reshape(b * t, d)
    proj_m = _matmul(h, w['w_in_m'], "in_proj_m").reshape(b, t, -1)
    proj_a = _matmul(h, w['w_in_a'], "in_proj_a").reshape(b, t, -1)
    dt = _dt_proj(h, w['w_dt'], w['dt_bias']).reshape(b, t, -1)

    pos = pos0 + jnp.arange(t, dtype=jnp.int32)
    qr, k_new, v_new, kb, vb = _rope_kv(proj_a, pos, d, kvw, k_off // kvw, v_off // kvw)

    y_m, ssm_new = _ssd(proj_m, dt, conv_state, ssm_state, w['conv_w'], w['conv_b'], w['a_log'], w['d_skip'],
                        w['mamba_norm_w'], z_off, xbc_off, d_inner)
    halo = M_CONV - 1
    conv_new = proj_m[:, t - halo:, xbc_off:xbc_off + conv_dim]

    y_a = _attention(qr, kb, vb, proj_a, za_off, k_past, v_past,
                     (w['lam_q1'], w['lam_k1'], w['lam_q2'], w['lam_k2']), w['attn_norm_w'], lam_init, pos0)

    merged = _merge(y_m.reshape(b * t, d_inner), y_a.reshape(b * t, d), w['w_proj_m'], w['w_proj_a'],
                    proj_a.reshape(b * t, -1), gm_off, ga_off)
    y = _out_proj(merged, w['w_out'], x, gate, final_w)
    kvh = d // QH_W
    return (y, k_new.reshape(b, t, kvh, 2, A_HEAD_DIM), v_new.reshape(b, t, kvh, KV_W), conv_new, ssm_new)


def kernel(x_prompt, x_sample, cache_k, cache_v, state_conv, state_ssm, c_prompt, c_sample,
           w_ada, b_ada, norm_w, w_in, conv_w, conv_b, dt_bias, a_log, d_skip, mamba_norm_w,
           lam_q1, lam_k1, lam_q2, lam_k2, attn_norm_w, w_proj_m, w_proj_a, w_out, final_norm_w):
    depth = w_in.shape[0]
    assert depth == 1, "the final norm is fused into the single layer's output projection"
    bp, d = c_prompt.shape
    bs = c_sample.shape[0]
    past = cache_k.shape[2]
    d_inner = 2 * d
    groups = d_inner // GROUP_W
    heads = groups * M_HPG
    conv_dim = d_inner + 2 * groups * M_STATE
    kvw = (d // QH_W) * KV_W
    sizes = (d_inner, conv_dim, heads, d, kvw, kvw, d, d, d)
    offs = [0]
    for s in sizes:
        offs.append(offs[-1] + s)

    i = 0
    wi = w_in[i]
    w = {
        'w_in_m': wi[:, :offs[2]].astype(BF16),
        'w_dt': wi[:, offs[2]:offs[3]].astype(BF16),
        'w_in_a': wi[:, offs[3]:].astype(BF16),
        'norm_w': norm_w[i], 'conv_w': conv_w[i], 'conv_b': conv_b[i], 'dt_bias': dt_bias[i], 'a_log': a_log[i],
        'd_skip': d_skip[i], 'mamba_norm_w': mamba_norm_w[i], 'lam_q1': lam_q1[i], 'lam_k1': lam_k1[i],
        'lam_q2': lam_q2[i], 'lam_k2': lam_k2[i], 'attn_norm_w': attn_norm_w[i],
        'w_proj_m': w_proj_m[i].astype(BF16), 'w_proj_a': w_proj_a[i].astype(BF16), 'w_out': w_out[i].astype(BF16),
    }
    lam_init = 0.8 - 0.6 * math.exp(-0.3 * i)
    pad_rows = -(bp + bs) % 16
    c_all = jnp.concatenate([c_prompt, c_sample, jnp.zeros((pad_rows, d), F32)], axis=0)
    mod = _ada_mod(c_all, w_ada[i], b_ada[i])

    yp, kp, vp, cp, sp = _layer_path(x_prompt, mod[:bp], None, None, None, None, 0, w, lam_init, final_norm_w)
    ck = cache_k[i].reshape(bs, past, kvw)
    cv = cache_v[i].reshape(bs, past, kvw)
    ys, kq, vq, cq, sq = _layer_path(x_sample, mod[bp:bp + bs], ck, cv, state_conv[i], state_ssm[i], past, w, lam_init,
                                     final_norm_w)
    st = lambda a: a[None]
    return (yp, ys, st(kp), st(vp), st(cp), st(sp), st(kq), st(vq), st(cq), st(sq))
```

```python
import functools
import math

import jax
import jax.numpy as jnp
from jax import lax
from jax.experimental import pallas as pl
from jax.experimental.pallas import tpu as pltpu

F32 = jnp.float32
BF16 = jnp.bfloat16

CHUNK = 64
EPS = 1e-6
M_HEAD_DIM = 64
M_HPG = 16
M_STATE = 128
M_CONV = 4
GROUP_W = M_HPG * M_HEAD_DIM
A_HEAD_DIM = 64
A_REP = 4
KV_W = 2 * A_HEAD_DIM
QH_W = A_REP * KV_W
ROPE_DIM = 16
ROPE_THETA = 500000.0
LANES = 128
VMEM_LIMIT_BYTES = 56 * 1024 * 1024


def _pick(n, target, mult=8):
    if n <= target:
        return n
    for t in range(target, 0, -1):
        if n % t == 0 and t % mult == 0:
            return t
    return n


def _params(*sem):
    return pltpu.CompilerParams(dimension_semantics=sem, vmem_limit_bytes=VMEM_LIMIT_BYTES)


def _silu(x):
    return x * jax.nn.sigmoid(x)


def _split_hi_lo(x):
    hi = x.astype(BF16)
    lo = (x - hi.astype(F32)).astype(BF16)
    return hi, lo


def _dot(a, b):
    return jnp.dot(a, b, preferred_element_type=F32)


def _dot_nt(a, b):
    return lax.dot_general(a, b, (((1,), (1,)), ((), ())), preferred_element_type=F32)


def _dot_tn(a, b):
    return lax.dot_general(a, b, (((0,), (0,)), ((), ())), preferred_element_type=F32)


def _ada_kernel(c_ref, w_ref, b_ref, o_ref):
    a = _silu(c_ref[...]).astype(BF16)
    o_ref[...] = _dot(a, w_ref[...].astype(BF16)) + b_ref[...]


def _ada_mod(c, w_ada, b_ada):
    m, d = c.shape
    n = w_ada.shape[1]
    tn = _pick(n, 512, LANES)
    return pl.pallas_call(
        _ada_kernel,
        grid=(n // tn,),
        in_specs=[pl.BlockSpec((m, d), lambda j: (0, 0)),
                  pl.BlockSpec((d, tn), lambda j: (0, j)),
                  pl.BlockSpec((1, tn), lambda j: (0, j))],
        out_specs=pl.BlockSpec((m, tn), lambda j: (0, j)),
        out_shape=jax.ShapeDtypeStruct((m, n), F32),
        compiler_params=_params("arbitrary"),
        name="ada_mod",
    )(c, w_ada, b_ada.reshape(1, n))


def _prenorm_kernel(x_ref, nw_ref, sc_ref, sh_ref, o_ref):
    x = x_ref[0]
    ms = jnp.mean(x * x, axis=-1, keepdims=True)
    y = x * lax.rsqrt(ms + EPS) * nw_ref[...]
    o_ref[0] = (y * (1.0 + sc_ref[0]) + sh_ref[0]).astype(o_ref.dtype)


def _prenorm(x, norm_w, scale, shift):
    b, t, d = x.shape
    tr = _pick(t, 256)
    return pl.pallas_call(
        _prenorm_kernel,
        grid=(b, t // tr),
        in_specs=[pl.BlockSpec((1, tr, d), lambda i, j: (i, j, 0)),
                  pl.BlockSpec((1, d), lambda i, j: (0, 0)),
                  pl.BlockSpec((1, 1, d), lambda i, j: (i, 0, 0)),
                  pl.BlockSpec((1, 1, d), lambda i, j: (i, 0, 0))],
        out_specs=pl.BlockSpec((1, tr, d), lambda i, j: (i, j, 0)),
        out_shape=jax.ShapeDtypeStruct((b, t, d), BF16),
        compiler_params=_params("arbitrary", "arbitrary"),
        name="prenorm",
    )(x, norm_w.reshape(1, d), scale.reshape(b, 1, d), shift.reshape(b, 1, d))


def _mm_kernel(a_ref, w_ref, o_ref):
    o_ref[...] = _dot(a_ref[...], w_ref[...]).astype(o_ref.dtype)


def _mm_castw_kernel(a_ref, w_ref, o_ref, wb_scr):
    @pl.when(pl.program_id(1) == 0)
    def _():
        wb_scr[...] = w_ref[0].astype(BF16)

    o_ref[...] = _dot(a_ref[...], wb_scr[...])


def _matmul_f32w(a, w3, layer, n, name):
    m, k = a.shape
    tm = _pick(m, 1024)
    tn = _pick(n, 512, LANES)
    return pl.pallas_call(
        _mm_castw_kernel,
        grid=(n // tn, m // tm),
        in_specs=[pl.BlockSpec((tm, k), lambda j, i: (i, 0)),
                  pl.BlockSpec((1, k, tn), lambda j, i: (layer, 0, j))],
        out_specs=pl.BlockSpec((tm, tn), lambda j, i: (i, j)),
        out_shape=jax.ShapeDtypeStruct((m, n), F32),
        scratch_shapes=[pltpu.VMEM((k, tn), BF16)],
        compiler_params=_params("arbitrary", "arbitrary"),
        name=name,
    )(a, w3)


def _matmul(a, w, name):
    m, k = a.shape
    n = w.shape[1]
    tm = _pick(m, 1024)
    tn = _pick(n, 1024, LANES)
    return pl.pallas_call(
        _mm_kernel,
        grid=(n // tn, m // tm),
        in_specs=[pl.BlockSpec((tm, k), lambda j, i: (i, 0)),
                  pl.BlockSpec((k, tn), lambda j, i: (0, j))],
        out_specs=pl.BlockSpec((tm, tn), lambda j, i: (i, j)),
        out_shape=jax.ShapeDtypeStruct((m, n), F32),
        compiler_params=_params("arbitrary", "arbitrary"),
        name=name,
    )(a, w)


def _dt_kernel(a_ref, w_ref, b_ref, o_ref):
    x = _dot(a_ref[...], w_ref[...]) + b_ref[...]
    o_ref[...] = jnp.maximum(x, 0.0) + jnp.log1p(jnp.exp(-jnp.abs(x)))


def _dt_proj(a, w_dt, dt_bias):
    m, k = a.shape
    n = w_dt.shape[1]
    tm = _pick(m, 512)
    return pl.pallas_call(
        _dt_kernel,
        grid=(m // tm,),
        in_specs=[pl.BlockSpec((tm, k), lambda i: (i, 0)),
                  pl.BlockSpec((k, n), lambda i: (0, 0)),
                  pl.BlockSpec((1, n), lambda i: (0, 0))],
        out_specs=pl.BlockSpec((tm, n), lambda i: (i, 0)),
        out_shape=jax.ShapeDtypeStruct((m, n), F32),
        compiler_params=_params("arbitrary"),
        name="dt_proj",
    )(a, w_dt, dt_bias.reshape(1, n))


def _rope_slab(x, cos, sin_up, sin_dn):
    return x * cos + pltpu.roll(x, LANES - ROPE_DIM // 2, 1) * sin_up + pltpu.roll(x, ROPE_DIM // 2, 1) * sin_dn


def _rope_kernel(q_ref, k_ref, v_ref, cos_ref, sup_ref, sdn_ref, qo_ref, ko_ref, vo_ref, kb_ref, vb_ref, *, q_scale):
    cos, sup, sdn = cos_ref[...], sup_ref[...], sdn_ref[...]
    for s in range(q_ref.shape[2] // LANES):
        sl = slice(s * LANES, (s + 1) * LANES)
        qo_ref[0, :, sl] = (_rope_slab(q_ref[0, :, sl], cos, sup, sdn) * q_scale).astype(qo_ref.dtype)
    for s in range(k_ref.shape[2] // LANES):
        sl = slice(s * LANES, (s + 1) * LANES)
        k = _rope_slab(k_ref[0, :, sl], cos, sup, sdn)
        ko_ref[0, :, sl] = k
        kb_ref[0, :, sl] = k.astype(kb_ref.dtype)
    v = v_ref[...]
    vo_ref[...] = v
    vb_ref[...] = v.astype(vb_ref.dtype)


def _rope_tables(pos):
    half = ROPE_DIM // 2
    inv = ROPE_THETA ** (-jnp.arange(half, dtype=F32) / half)
    ang = pos.astype(F32)[:, None] * inv[None, :]
    cos, sin = jnp.cos(ang), jnp.sin(ang)
    t = pos.shape[0]
    ones = jnp.ones((t, A_HEAD_DIM - ROPE_DIM), F32)
    zeros = jnp.zeros((t, A_HEAD_DIM - ROPE_DIM), F32)
    zh = jnp.zeros((t, half), F32)
    cos64 = jnp.concatenate([cos, cos, ones], axis=1)
    sup64 = jnp.concatenate([-sin, zh, zeros], axis=1)
    sdn64 = jnp.concatenate([zh, sin, zeros], axis=1)
    rep = lambda a: jnp.concatenate([a, a], axis=1)
    return rep(cos64), rep(sup64), rep(sdn64)


def _rope_kv(proj, pos, d, kvw, k_blk, v_blk):
    b, t, _ = proj.shape
    tr = _pick(t, 256)
    cos, sup, sdn = _rope_tables(pos)
    tab = pl.BlockSpec((tr, LANES), lambda i, j: (j, 0))
    kv_out = pl.BlockSpec((1, tr, kvw), lambda i, j: (i, j, 0))
    return pl.pallas_call(
        functools.partial(_rope_kernel, q_scale=A_HEAD_DIM ** -0.5 * math.log2(math.e)),
        grid=(b, t // tr),
        in_specs=[pl.BlockSpec((1, tr, d), lambda i, j: (i, j, 0)),
                  pl.BlockSpec((1, tr, kvw), lambda i, j: (i, j, k_blk)),
                  pl.BlockSpec((1, tr, kvw), lambda i, j: (i, j, v_blk)),
                  tab, tab, tab],
        out_specs=[pl.BlockSpec((1, tr, d), lambda i, j: (i, j, 0))] + [kv_out] * 4,
        out_shape=[jax.ShapeDtypeStruct((b, t, d), BF16),
                   jax.ShapeDtypeStruct((b, t, kvw), F32),
                   jax.ShapeDtypeStruct((b, t, kvw), F32),
                   jax.ShapeDtypeStruct((b, t, kvw), BF16),
                   jax.ShapeDtypeStruct((b, t, kvw), BF16)],
        compiler_params=_params("arbitrary", "arbitrary"),
        name="rope_kv",
    )(proj, proj, proj, cos, sup, sdn)


def _ssd_kernel(*refs, tl, has_state):
    (alog_ref, alr_ref, dsk_ref, mnw_ref, cwx_ref, cwb_ref, cwc_ref, cbx_ref, cbb_ref, cbc_ref,
     z_ref, x_ref, bm_ref, cm_ref, dt_ref, dtr_ref) = refs[:16]
    if has_state:
        csx_ref, csb_ref, csc_ref, s0_ref = refs[16:20]
        rest = refs[20:]
    else:
        rest = refs[16:]
    y_ref, sout_ref, xbuf, bbuf, cbuf, st_scr = rest
    g = pl.program_id(1)
    c = pl.program_id(2)
    L = CHUNK
    nsub = tl // L
    halo = M_CONV - 1
    base = 8

    @pl.when(c == 0)
    def _():
        if has_state:
            xbuf[base - halo:base, :] = csx_ref[0]
            bbuf[base - halo:base, :] = csb_ref[0]
            cbuf[base - halo:base, :] = csc_ref[0]
            st_scr[...] = s0_ref[0].reshape(GROUP_W, M_STATE).T
        else:
            xbuf[0:base, :] = jnp.zeros((base, GROUP_W), F32)
            bbuf[0:base, :] = jnp.zeros((base, M_STATE), F32)
            cbuf[0:base, :] = jnp.zeros((base, M_STATE), F32)
            st_scr[...] = jnp.zeros_like(st_scr)

    xbuf[base:base + tl, :] = x_ref[0]
    bbuf[base:base + tl, :] = bm_ref[0]
    cbuf[base:base + tl, :] = cm_ref[0]

    r64 = lax.broadcasted_iota(jnp.int32, (L, L), 0)
    c64 = lax.broadcasted_iota(jnp.int32, (L, L), 1)
    tril = (c64 <= r64).astype(BF16)
    heads = dt_ref.shape[2]
    hrow = lax.broadcasted_iota(jnp.int32, (heads, GROUP_W), 0)
    hcol = lax.broadcasted_iota(jnp.int32, (heads, GROUP_W), 1)
    expand = (hrow == g * M_HPG + hcol // M_HEAD_DIM).astype(BF16)
    br = lax.broadcasted_iota(jnp.int32, (LANES, LANES), 0)
    bc = lax.broadcasted_iota(jnp.int32, (LANES, LANES), 1)
    same_half = (br // L) == (bc // L)
    triu2 = (same_half & ((br % L) <= (bc % L))).astype(BF16)
    pr = lax.broadcasted_iota(jnp.int32, (L, LANES), 0)
    pc = lax.broadcasted_iota(jnp.int32, (L, LANES), 1)
    causal2 = (pc % L) <= pr

    a_col = -jnp.exp(alog_ref[...])
    a_row = -jnp.exp(alr_ref[0])
    n_rows = dtr_ref.shape[2]
    da_r = dtr_ref[0, 0] * jnp.concatenate([a_row] * (n_rows // 8), axis=0)
    hi, lo = _split_hi_lo(da_r)
    acum_r_all = _dot(hi, triu2) + _dot(lo, triu2)

    def conv(buf, w_ref, b_ref, i):
        acc = b_ref[...] + w_ref[M_CONV - 1:M_CONV, :] * buf[base + i * L:base + (i + 1) * L, :]
        for k in range(M_CONV - 1):
            off = base - halo + k + i * L
            acc = acc + w_ref[k:k + 1, :] * buf[off:off + L, :]
        return _silu(acc)

    for i in range(nsub):
        rows = slice(i * L, (i + 1) * L)
        x = conv(xbuf, cwx_ref, cbx_ref, i)
        bm = conv(bbuf, cwb_ref, cbb_ref, i).astype(BF16)
        cm = conv(cbuf, cwc_ref, cbc_ref, i).astype(BF16)
        dt = dt_ref[0, rows, :]
        da = dt * a_col
        hi, lo = _split_hi_lo(da)
        acum = _dot(tril, hi) + _dot(tril, lo)
        hi, lo = _split_hi_lo(acum)
        acum_x = _dot(hi, expand) + _dot(lo, expand)
        hi, lo = _split_hi_lo(dt)
        dt_x = _dot(hi, expand) + _dot(lo, expand)
        xdt = x * dt_x
        xdt_b = xdt.astype(BF16)
        alast_x = acum_x[L - 1:L, :]
        cb2 = _dot_nt(cm, jnp.concatenate([bm, bm], axis=0))
        st_b = st_scr[...].astype(BF16)
        y_off = _dot(cm, st_b) * jnp.exp(acum_x)
        y_parts = []
        for jj in range(GROUP_W // LANES):
            ls = slice(jj * LANES, (jj + 1) * LANES)
            seg = acum_x[:, ls] - acum_r_all[i * 8 + jj:i * 8 + jj + 1, :]
            m2 = (jnp.where(causal2, jnp.exp(seg), 0.0) * cb2).astype(BF16)
            x2 = xdt_b[:, ls]
            rhs = jnp.where(same_half, jnp.concatenate([x2, x2], axis=0), jnp.zeros((), BF16))
            y_parts.append(_dot(m2, rhs))
        y = jnp.concatenate(y_parts, axis=1) + y_off + dsk_ref[0] * x
        y = y * _silu(z_ref[0, rows, :])
        ms = jnp.mean(y * y, axis=-1, keepdims=True)
        y_ref[0, rows, :] = (y * lax.rsqrt(ms + EPS) * mnw_ref[0]).astype(y_ref.dtype)
        xw = (xdt * jnp.exp(alast_x - acum_x)).astype(BF16)
        st_scr[...] = st_scr[...] * jnp.exp(alast_x) + _dot_tn(bm, xw)

    tx = xbuf[base + tl - halo:base + tl, :]
    tb = bbuf[base + tl - halo:base + tl, :]
    tc = cbuf[base + tl - halo:base + tl, :]
    xbuf[base - halo:base, :] = tx
    bbuf[base - halo:base, :] = tb
    cbuf[base - halo:base, :] = tc

    @pl.when(c == pl.num_programs(2) - 1)
    def _():
        sout_ref[0] = st_scr[...].T.reshape(M_HPG, M_HEAD_DIM, M_STATE)


def _ssd(proj, dt, conv_state, ssm_state, conv_w, conv_b, a_log, d_skip, mnorm_w, z_off, xbc_off, d_inner):
    b, t, _ = proj.shape
    groups = d_inner // GROUP_W
    heads = groups * M_HPG
    bcw = groups * M_STATE
    has_state = conv_state is not None
    tl = _pick(t, 256, CHUNK)
    nsub = tl // CHUNK
    nc = t // tl
    dtr = dt.reshape(b, t // CHUNK, CHUNK, groups, M_HPG // 2, 2).transpose(0, 3, 1, 4, 5, 2)
    dtr = dtr.reshape(b, groups, (t // CHUNK) * 8, LANES)
    n_rows = nsub * 8
    if n_rows < 16:
        dtr = jnp.concatenate([dtr, jnp.zeros_like(dtr)], axis=2)
        n_rows = 16
    alr = jnp.repeat(a_log.reshape(groups, M_HPG // 2, 2), CHUNK, axis=2)
    dsk = jnp.repeat(d_skip.reshape(groups, 1, M_HPG), M_HEAD_DIM, axis=2)
    mnw = mnorm_w.reshape(groups, 1, GROUP_W)
    xb, bb, cb = xbc_off // GROUP_W, (xbc_off + d_inner) // M_STATE, (xbc_off + d_inner + bcw) // M_STATE
    cxb, cbb, ccb = 0, d_inner // M_STATE, (d_inner + bcw) // M_STATE
    zb = z_off // GROUP_W
    conv_b2 = conv_b.reshape(1, -1)
    gmap = lambda blk: (lambda i, g, c: (0, blk + g))
    in_specs = [
        pl.BlockSpec((1, heads), lambda i, g, c: (0, 0)),
        pl.BlockSpec((1, 8, LANES), lambda i, g, c: (g, 0, 0)),
        pl.BlockSpec((1, 1, GROUP_W), lambda i, g, c: (g, 0, 0)),
        pl.BlockSpec((1, 1, GROUP_W), lambda i, g, c: (g, 0, 0)),
        pl.BlockSpec((M_CONV, GROUP_W), gmap(cxb)),
        pl.BlockSpec((M_CONV, M_STATE), gmap(cbb)),
        pl.BlockSpec((M_CONV, M_STATE), gmap(ccb)),
        pl.BlockSpec((1, GROUP_W), gmap(cxb)),
        pl.BlockSpec((1, M_STATE), gmap(cbb)),
        pl.BlockSpec((1, M_STATE), gmap(ccb)),
        pl.BlockSpec((1, tl, GROUP_W), lambda i, g, c: (i, c, zb + g)),
        pl.BlockSpec((1, tl, GROUP_W), lambda i, g, c: (i, c, xb + g)),
        pl.BlockSpec((1, tl, M_STATE), lambda i, g, c: (i, c, bb + g)),
        pl.BlockSpec((1, tl, M_STATE), lambda i, g, c: (i, c, cb + g)),
        pl.BlockSpec((1, tl, heads), lambda i, g, c: (i, c, 0)),
        pl.BlockSpec((1, 1, n_rows, LANES), lambda i, g, c: (i, g, c, 0)),
    ]
    args = [a_log.reshape(1, heads), alr, dsk, mnw, conv_w, conv_w, conv_w, conv_b2, conv_b2, conv_b2,
            proj, proj, proj, proj, dt.reshape(b, t, heads), dtr]
    if has_state:
        halo = M_CONV - 1
        in_specs += [
            pl.BlockSpec((1, halo, GROUP_W), lambda i, g, c: (i, 0, cxb + g)),
            pl.BlockSpec((1, halo, M_STATE), lambda i, g, c: (i, 0, cbb + g)),
            pl.BlockSpec((1, halo, M_STATE), lambda i, g, c: (i, 0, ccb + g)),
            pl.BlockSpec((1, M_HPG, M_HEAD_DIM, M_STATE), lambda i, g, c: (i, g, 0, 0)),
        ]
        args += [conv_state, conv_state, conv_state, ssm_state]
    return pl.pallas_call(
        functools.partial(_ssd_kernel, tl=tl, has_state=has_state),
        grid=(b, groups, nc),
        in_specs=in_specs,
        out_specs=[pl.BlockSpec((1, tl, GROUP_W), lambda i, g, c: (i, c, g)),
                   pl.BlockSpec((1, M_HPG, M_HEAD_DIM, M_STATE), lambda i, g, c: (i, g, 0, 0))],
        out_shape=[jax.ShapeDtypeStruct((b, t, d_inner), BF16),
                   jax.ShapeDtypeStruct((b, heads, M_HEAD_DIM, M_STATE), F32)],
        scratch_shapes=[pltpu.VMEM((tl + 8, GROUP_W), F32),
                        pltpu.VMEM((tl + 8, M_STATE), F32),
                        pltpu.VMEM((tl + 8, M_STATE), F32),
                        pltpu.VMEM((M_STATE, GROUP_W), F32)],
        compiler_params=_params("arbitrary", "arbitrary", "arbitrary"),
        name="ssd",
    )(*args)


def _attn_kernel(*refs, tq, tk, rc, n_qt, past, tkp, pos0, lam_init):
    lq1_ref, lk1_ref, lq2_ref, lk2_ref, nw_ref, q_ref, k_ref, v_ref, za_ref = refs[:9]
    if past:
        kp_ref, vp_ref = refs[9:11]
        rest = refs[11:]
    else:
        rest = refs[9:]
    o_ref, qz_scr, s_scr, p_scr, m_scr, l_scr, acc_scr = rest
    qi = pl.program_id(2)
    rows = A_REP * tq

    lane = lax.broadcasted_iota(jnp.int32, (tq, KV_W), 1)
    zero = jnp.zeros((), BF16)
    for r in range(A_REP):
        q = q_ref[0, :, r * KV_W:(r + 1) * KV_W]
        qz_scr[0, r * tq:(r + 1) * tq, :] = jnp.where(lane < A_HEAD_DIM, q, zero)
        qz_scr[1, r * tq:(r + 1) * tq, :] = jnp.where(lane >= A_HEAD_DIM, q, zero)
    m_scr[...] = jnp.full(m_scr.shape, -jnp.inf, F32)
    l_scr[...] = jnp.zeros(l_scr.shape, F32)
    acc_scr[...] = jnp.zeros(acc_scr.shape, F32)

    def tile(k, v, nkeys, mask_fn):
        reps = nkeys // LANES
        for comp in range(2):
            s_scr[comp, :, :nkeys] = _dot_nt(qz_scr[comp], k)
        for comp in range(2):
            for i in range(rows // rc):
                rs = slice(i * rc, (i + 1) * rc)
                s = s_scr[comp, rs, :nkeys]
                if mask_fn is not None:
                    s = jnp.where(mask_fn(i), s, -jnp.inf)
                m_old = m_scr[comp, rs, :]
                m_new = jnp.maximum(m_old, jnp.max(s, axis=-1, keepdims=True))
                p = jnp.exp2(s - jnp.concatenate([m_new] * reps, axis=1))
                alpha = jnp.exp2(m_old - m_new)
                psum = p[:, :LANES]
                for u in range(1, reps):
                    psum = psum + p[:, u * LANES:(u + 1) * LANES]
                l_scr[comp, rs, :] = alpha * l_scr[comp, rs, :] + psum
                acc_scr[comp, rs, :] = alpha * acc_scr[comp, rs, :]
                m_scr[comp, rs, :] = m_new
                p_scr[comp, rs, :nkeys] = p.astype(BF16)
            acc_scr[comp] = acc_scr[comp] + _dot(p_scr[comp, :, :nkeys], v)

    def diag_mask(i, nkeys, n_valid, k_pos0):
        t0 = (i * rc) % tq
        qpos = pos0 + qi * tq + t0 + lax.broadcasted_iota(jnp.int32, (rc, nkeys), 0)
        kidx = lax.broadcasted_iota(jnp.int32, (rc, nkeys), 1)
        ok = ((k_pos0 + kidx) // CHUNK) <= (qpos // CHUNK)
        if n_valid < nkeys:
            ok = ok & (kidx < n_valid)
        return ok

    if past:
        def pbody(j, carry):
            sl = pl.ds(pl.multiple_of(j * tkp, tkp), tkp)
            tile(kp_ref[0, sl, :].astype(BF16), vp_ref[0, sl, :].astype(BF16), tkp, None)
            return carry

        lax.fori_loop(0, past // tkp, pbody, 0)

    def nbody(j, carry):
        sl = pl.ds(pl.multiple_of(j * tk, tk), tk)
        tile(k_ref[0, sl, :], v_ref[0, sl, :], tk, None)
        return carry

    if n_qt > 1:
        lax.fori_loop(0, (qi * tq) // tk, nbody, 0)
    tkd = max(tk, LANES)
    for d in range(tq // tk):
        k0 = qi * tq + d * tk
        sl = pl.ds(pl.multiple_of(k0, tk), tk)
        k, v = k_ref[0, sl, :], v_ref[0, sl, :]
        if tkd > tk:
            pad = jnp.zeros((tkd - tk, KV_W), BF16)
            k, v = jnp.concatenate([k, pad], axis=0), jnp.concatenate([v, pad], axis=0)
        tile(k, v, tkd, functools.partial(diag_mask, nkeys=tkd, n_valid=tk, k_pos0=pos0 + k0))

    lam = (jnp.exp(jnp.sum(lq1_ref[...] * lk1_ref[...], axis=-1, keepdims=True))
           - jnp.exp(jnp.sum(lq2_ref[...] * lk2_ref[...], axis=-1, keepdims=True)) + lam_init)
    l0 = jnp.sum(l_scr[0], axis=-1, keepdims=True)
    l1 = jnp.sum(l_scr[1], axis=-1, keepdims=True)
    o = acc_scr[0] / l0 - lam * (acc_scr[1] / l1)
    ms = jnp.mean(o * o, axis=-1, keepdims=True)
    on = o * lax.rsqrt(ms + EPS) * nw_ref[...] * (1.0 - lam_init)
    for r in range(A_REP):
        sl = slice(r * KV_W, (r + 1) * KV_W)
        o_ref[0, :, sl] = (on[r * tq:(r + 1) * tq, :] * _silu(za_ref[0, :, sl])).astype(o_ref.dtype)


def _attention(qr, kb, vb, proj, za_off, k_past, v_past, lam_vecs, norm_w, lam_init, pos0):
    b, t, d = qr.shape
    kvh = d // QH_W
    past = 0 if k_past is None else k_past.shape[1]
    tq = _pick(t, 512, CHUNK)
    tk = tq
    tkp = _pick(past, 512, LANES) if past else 0
    rc = min(64, tq)
    zb = za_off // QH_W
    rows = A_REP * tq
    smax = max(tk, LANES, tkp)
    vec = pl.BlockSpec((1, A_HEAD_DIM), lambda i, h, j: (0, 0))
    in_specs = [vec, vec, vec, vec,
                pl.BlockSpec((1, KV_W), lambda i, h, j: (0, 0)),
                pl.BlockSpec((1, tq, QH_W), lambda i, h, j: (i, j, h)),
                pl.BlockSpec((1, t, KV_W), lambda i, h, j: (i, 0, h)),
                pl.BlockSpec((1, t, KV_W), lambda i, h, j: (i, 0, h)),
                pl.BlockSpec((1, tq, QH_W), lambda i, h, j: (i, j, zb + h))]
    args = [v.reshape(1, A_HEAD_DIM) for v in lam_vecs] + [norm_w.reshape(1, KV_W), qr, kb, vb, proj]
    if past:
        in_specs += [pl.BlockSpec((1, past, KV_W), lambda i, h, j: (i, 0, h)),
                     pl.BlockSpec((1, past, KV_W), lambda i, h, j: (i, 0, h))]
        args += [k_past, v_past]
    return pl.pallas_call(
        functools.partial(_attn_kernel, tq=tq, tk=tk, rc=rc, n_qt=t // tq, past=past, tkp=tkp, pos0=pos0,
                          lam_init=lam_init),
        grid=(b, kvh, t // tq),
        in_specs=in_specs,
        out_specs=pl.BlockSpec((1, tq, QH_W), lambda i, h, j: (i, j, h)),
        out_shape=jax.ShapeDtypeStruct((b, t, d), BF16),
        scratch_shapes=[pltpu.VMEM((2, rows, KV_W), BF16),
                        pltpu.VMEM((2, rows, smax), F32),
                        pltpu.VMEM((2, rows, smax), BF16),
                        pltpu.VMEM((2, rows, LANES), F32),
                        pltpu.VMEM((2, rows, LANES), F32),
                        pltpu.VMEM((2, rows, KV_W), F32)],
        compiler_params=_params("arbitrary", "arbitrary", "arbitrary"),
        name="diff_attn",
    )(*args)


def _merge_kernel(ym_ref, ya_ref, wm_ref, wa_ref, gm_ref, ga_ref, o_ref):
    pm = _dot(ym_ref[...], wm_ref[...])
    pa = _dot(ya_ref[...], wa_ref[...])
    o_ref[...] = (jax.nn.sigmoid(gm_ref[...]) * pm + jax.nn.sigmoid(ga_ref[...]) * pa).astype(o_ref.dtype)


def _merge(y_m, y_a, w_pm, w_pa, proj2d, gm_off, ga_off):
    m, km = y_m.shape
    ka = y_a.shape[1]
    d = w_pm.shape[1]
    tm = _pick(m, 256)
    tn = _pick(d, 512, LANES)
    gmb, gab = gm_off // tn, ga_off // tn
    return pl.pallas_call(
        _merge_kernel,
        grid=(d // tn, m // tm),
        in_specs=[pl.BlockSpec((tm, km), lambda j, i: (i, 0)),
                  pl.BlockSpec((tm, ka), lambda j, i: (i, 0)),
                  pl.BlockSpec((km, tn), lambda j, i: (0, j)),
                  pl.BlockSpec((ka, tn), lambda j, i: (0, j)),
                  pl.BlockSpec((tm, tn), lambda j, i: (i, gmb + j)),
                  pl.BlockSpec((tm, tn), lambda j, i: (i, gab + j))],
        out_specs=pl.BlockSpec((tm, tn), lambda j, i: (i, j)),
        out_shape=jax.ShapeDtypeStruct((m, d), BF16),
        compiler_params=_params("arbitrary", "arbitrary"),
        name="merge",
    )(y_m, y_a, w_pm, w_pa, proj2d, proj2d)


def _out_kernel(mg_ref, w_ref, x_ref, gate_ref, fw_ref, o_ref, *, tn):
    j = pl.program_id(2)
    cols = pl.ds(pl.multiple_of(j * tn, tn), tn)
    o_ref[0, :, cols] = x_ref[0] + gate_ref[0] * _dot(mg_ref[0], w_ref[...])

    @pl.when(j == pl.num_programs(2) - 1)
    def _():
        r = o_ref[0]
        ms = jnp.mean(r * r, axis=-1, keepdims=True)
        o_ref[0] = r * lax.rsqrt(ms + EPS) * fw_ref[...]


def _out_proj(merged, w_out, x, gate, final_w):
    shape = x.shape
    b, t, d = shape
    gate = gate.reshape(b, 1, d)
    per_row_gate = t < 512 and b > 1
    if per_row_gate:
        gate = jnp.broadcast_to(gate, (b, t, d)).reshape(1, b * t, d)
        x = x.reshape(1, b * t, d)
        b, t = 1, b * t
    tm = _pick(t, 512)
    tn = _pick(d, 512, LANES)
    gate_spec =(pl.BlockSpec((1, tm, tn), lambda i, r, j: (i, r, j)) if per_row_gate else
                 pl.BlockSpec((1, 1, tn), lambda i, r, j: (i, 0, j)))
    out = pl.pallas_call(
        functools.partial(_out_kernel, tn=tn),
        grid=(b, t // tm, d // tn),
        in_specs=[pl.BlockSpec((1, tm, d), lambda i, r, j: (i, r, 0)),
                  pl.BlockSpec((d, tn), lambda i, r, j: (0, j)),
                  pl.BlockSpec((1, tm, tn), lambda i, r, j: (i, r, j)),
                  gate_spec,
                  pl.BlockSpec((1, d), lambda i, r, j: (0, 0))],
        out_specs=pl.BlockSpec((1, tm, d), lambda i, r, j: (i, r, 0)),
        out_shape=jax.ShapeDtypeStruct((b, t, d), F32),
        compiler_params=_params("arbitrary", "arbitrary", "arbitrary"),
        name="out_proj",
    )(merged.reshape(b, t, d), w_out, x, gate, final_w.reshape(1, d))
    return out.reshape(shape)


def _layer_path(x, mod, k_past, v_past, conv_state, ssm_state, pos0, w, lam_init, final_w):
    b, t, d = x.shape
    d_inner = 2 * d
    groups = d_inner // GROUP_W
    bcw = groups * M_STATE
    conv_dim = d_inner + 2 * bcw
    kvw = (d // QH_W) * KV_W
    z_off, xbc_off = 0, d_inner
    k_off, v_off, za_off, gm_off, ga_off = d, d + kvw, d + 2 * kvw, 2 * d + 2 * kvw, 3 * d + 2 * kvw
    assert k_off % kvw == 0 and za_off % QH_W == 0, "consumer column blocks must be block-aligned"
    shift, scale, gate = mod[:, :d], mod[:, d:2 * d], mod[:, 2 * d:]

    h = _prenorm(x, w['norm_w'], scale, shift).reshape(b * t, d)
    proj_m = _matmul_f32w(h, w['w_in'], w['layer'], xbc_off + conv_dim, "in_proj_m").reshape(b, t, -1)
    proj_a = _matmul(h, w['w_in_a'], "in_proj_a").reshape(b, t, -1)
    dt = _dt_proj(h, w['w_dt'], w['dt_bias']).reshape(b, t, -1)

    pos = pos0 + jnp.arange(t, dtype=jnp.int32)
    qr, k_new, v_new, kb, vb = _rope_kv(proj_a, pos, d, kvw, k_off // kvw, v_off // kvw)

    y_m, ssm_new = _ssd(proj_m, dt, conv_state, ssm_state, w['conv_w'], w['conv_b'], w['a_log'], w['d_skip'],
                        w['mamba_norm_w'], z_off, xbc_off, d_inner)
    halo = M_CONV - 1
    conv_new = proj_m[:, t - halo:, xbc_off:xbc_off + conv_dim]

    y_a = _attention(qr, kb, vb, proj_a, za_off, k_past, v_past,
                     (w['lam_q1'], w['lam_k1'], w['lam_q2'], w['lam_k2']), w['attn_norm_w'], lam_init, pos0)

    merged = _merge(y_m.reshape(b * t, d_inner), y_a.reshape(b * t, d), w['w_proj_m'], w['w_proj_a'],
                    proj_a.reshape(b * t, -1), gm_off, ga_off)
    y = _out_proj(merged, w['w_out'], x, gate, final_w)
    kvh = d // QH_W
    return (y, k_new.reshape(b, t, kvh, 2, A_HEAD_DIM), v_new.reshape(b, t, kvh, KV_W), conv_new, ssm_new)


def kernel(x_prompt, x_sample, cache_k, cache_v, state_conv, state_ssm, c_prompt, c_sample,
           w_ada, b_ada, norm_w, w_in, conv_w, conv_b, dt_bias, a_log, d_skip, mamba_norm_w,
           lam_q1, lam_k1, lam_q2, lam_k2, attn_norm_w, w_proj_m, w_proj_a, w_out, final_norm_w):
    depth = w_in.shape[0]
    assert depth == 1, "the final norm is fused into the single layer's output projection"
    bp, d = c_prompt.shape
    bs = c_sample.shape[0]
    past = cache_k.shape[2]
    d_inner = 2 * d
    groups = d_inner // GROUP_W
    heads = groups * M_HPG
    conv_dim = d_inner + 2 * groups * M_STATE
    kvw = (d // QH_W) * KV_W
    sizes = (d_inner, conv_dim, heads, d, kvw, kvw, d, d, d)
    offs = [0]
    for s in sizes:
        offs.append(offs[-1] + s)

    i = 0
    wi = w_in[i]
    w = {
        'w_in': w_in, 'layer': i,
        'w_dt': wi[:, offs[2]:offs[3]].astype(BF16),
        'w_in_a': wi[:, offs[3]:].astype(BF16),
        'norm_w': norm_w[i], 'conv_w': conv_w[i], 'conv_b': conv_b[i], 'dt_bias': dt_bias[i], 'a_log': a_log[i],
        'd_skip': d_skip[i], 'mamba_norm_w': mamba_norm_w[i], 'lam_q1': lam_q1[i], 'lam_k1': lam_k1[i],
        'lam_q2': lam_q2[i], 'lam_k2': lam_k2[i], 'attn_norm_w': attn_norm_w[i],
        'w_proj_m': w_proj_m[i].astype(BF16), 'w_proj_a': w_proj_a[i].astype(BF16), 'w_out': w_out[i].astype(BF16),
    }
    lam_init = 0.8 - 0.6 * math.exp(-0.3 * i)
    pad_rows = -(bp + bs) % 16
    c_all = jnp.concatenate([c_prompt, c_sample, jnp.zeros((pad_rows, d), F32)], axis=0)
    mod = _ada_mod(c_all, w_ada[i], b_ada[i])

    yp, kp, vp, cp, sp = _layer_path(x_prompt, mod[:bp], None, None, None, None, 0, w, lam_init, final_norm_w)
    ck = cache_k[i].reshape(bs, past, kvw)
    cv = cache_v[i].reshape(bs, past, kvw)
    ys, kq, vq, cq, sq = _layer_path(x_sample, mod[bp:bp + bs], ck, cv, state_conv[i], state_ssm[i], past, w, lam_init,
                                     final_norm_w)
    st = lambda a: a[None]
    return (yp, ys, st(kp), st(vp), st(cp), st(sp), st(kq), st(vq), st(cq), st(sq))
```

```python
import functools
import math

import jax
import jax.numpy as jnp
from jax import lax
from jax.experimental import pallas as pl
from jax.experimental.pallas import tpu as pltpu

F32 = jnp.float32
BF16 = jnp.bfloat16

CHUNK = 64
EPS = 1e-6
M_HEAD_DIM = 64
M_HPG = 16
M_STATE = 128
M_CONV = 4
GROUP_W = M_HPG * M_HEAD_DIM
A_HEAD_DIM = 64
A_REP = 4
KV_W = 2 * A_HEAD_DIM
QH_W = A_REP * KV_W
ROPE_DIM = 16
ROPE_THETA = 500000.0
LANES = 128
VMEM_LIMIT_BYTES = 56 * 1024 * 1024


def _pick(n, target, mult=8):
    if n <= target:
        return n
    for t in range(target, 0, -1):
        if n % t == 0 and t % mult == 0:
            return t
    return n


def _params(*sem):
    return pltpu.CompilerParams(dimension_semantics=sem, vmem_limit_bytes=VMEM_LIMIT_BYTES)


def _silu(x):
    return x * jax.nn.sigmoid(x)


def _split_hi_lo(x):
    hi = x.astype(BF16)
    lo = (x - hi.astype(F32)).astype(BF16)
    return hi, lo


def _dot(a, b):
    return jnp.dot(a, b, preferred_element_type=F32)


def _dot_nt(a, b):
    return lax.dot_general(a, b, (((1,), (1,)), ((), ())), preferred_element_type=F32)


def _dot_tn(a, b):
    return lax.dot_general(a, b, (((0,), (0,)), ((), ())), preferred_element_type=F32)


def _ada_kernel(c_ref, w_ref, b_ref, o_ref):
    a = _silu(c_ref[...]).astype(BF16)
    o_ref[...] = _dot(a, w_ref[...].astype(BF16)) + b_ref[...]


def _ada_mod(c, w_ada, b_ada):
    m, d = c.shape
    n = w_ada.shape[1]
    tn = _pick(n, 512, LANES)
    return pl.pallas_call(
        _ada_kernel,
        grid=(n // tn,),
        in_specs=[pl.BlockSpec((m, d), lambda j: (0, 0)),
                  pl.BlockSpec((d, tn), lambda j: (0, j)),
                  pl.BlockSpec((1, tn), lambda j: (0, j))],
        out_specs=pl.BlockSpec((m, tn), lambda j: (0, j)),
        out_shape=jax.ShapeDtypeStruct((m, n), F32),
        compiler_params=_params("arbitrary"),
        name="ada_mod",
    )(c, w_ada, b_ada.reshape(1, n))


def _prenorm_kernel(x_ref, nw_ref, sc_ref, sh_ref, o_ref):
    x = x_ref[0]
    ms = jnp.mean(x * x, axis=-1, keepdims=True)
    y = x * lax.rsqrt(ms + EPS) * nw_ref[...]
    o_ref[0] = (y * (1.0 + sc_ref[0]) + sh_ref[0]).astype(o_ref.dtype)


def _prenorm(x, norm_w, scale, shift):
    b, t, d = x.shape
    tr = _pick(t, 256)
    return pl.pallas_call(
        _prenorm_kernel,
        grid=(b, t // tr),
        in_specs=[pl.BlockSpec((1, tr, d), lambda i, j: (i, j, 0)),
                  pl.BlockSpec((1, d), lambda i, j: (0, 0)),
                  pl.BlockSpec((1, 1, d), lambda i, j: (i, 0, 0)),
                  pl.BlockSpec((1, 1, d), lambda i, j: (i, 0, 0))],
        out_specs=pl.BlockSpec((1, tr, d), lambda i, j: (i, j, 0)),
        out_shape=jax.ShapeDtypeStruct((b, t, d), BF16),
        compiler_params=_params("arbitrary", "arbitrary"),
        name="prenorm",
    )(x, norm_w.reshape(1, d), scale.reshape(b, 1, d), shift.reshape(b, 1, d))


def _mm_kernel(a_ref, w_ref, o_ref):
    o_ref[...] = _dot(a_ref[...], w_ref[...]).astype(o_ref.dtype)


def _mm_castw_kernel(a_ref, w_ref, o_ref, wb_scr):
    @pl.when(pl.program_id(1) == 0)
    def _():
        wb_scr[...] = w_ref[0].astype(BF16)

    o_ref[...] = _dot(a_ref[...], wb_scr[...])


def _matmul_f32w(a, w3, layer, col0, n, name):
    m, k = a.shape
    tm = _pick(m, 1024)
    tn = _pick(n, 512, LANES)
    if col0 % LANES:
        w3, col0 = w3[:, :, col0:col0 + n], 0
    return pl.pallas_call(
        _mm_castw_kernel,
        grid=(n // tn, m // tm),
        in_specs=[pl.BlockSpec((tm, k), lambda j, i: (i, 0)),
                  pl.BlockSpec((pl.Element(1), pl.Element(k), pl.Element(tn)),
                               lambda j, i: (layer, 0, pl.multiple_of(col0 + j * tn, LANES)))],
        out_specs=pl.BlockSpec((tm, tn), lambda j, i: (i, j)),
        out_shape=jax.ShapeDtypeStruct((m, n), F32),
        scratch_shapes=[pltpu.VMEM((k, tn), BF16)],
        compiler_params=_params("arbitrary", "arbitrary"),
        name=name,
    )(a, w3)


def _matmul(a, w, name):
    m, k = a.shape
    n = w.shape[1]
    tm = _pick(m, 1024)
    tn = _pick(n, 1024, LANES)
    return pl.pallas_call(
        _mm_kernel,
        grid=(n // tn, m // tm),
        in_specs=[pl.BlockSpec((tm, k), lambda j, i: (i, 0)),
                  pl.BlockSpec((k, tn), lambda j, i: (0, j))],
        out_specs=pl.BlockSpec((tm, tn), lambda j, i: (i, j)),
        out_shape=jax.ShapeDtypeStruct((m, n), F32),
        compiler_params=_params("arbitrary", "arbitrary"),
        name=name,
    )(a, w)


def _dt_kernel(a_ref, w_ref, b_ref, o_ref, wb_scr):
    @pl.when(pl.program_id(0) == 0)
    def _():
        wb_scr[...] = w_ref[0].astype(BF16)

    x = _dot(a_ref[...], wb_scr[...]) + b_ref[...]
    o_ref[...] = jnp.maximum(x, 0.0) + jnp.log1p(jnp.exp(-jnp.abs(x)))


def _dt_proj(a, w3, layer, col0, dt_bias):
    m, k = a.shape
    n = dt_bias.shape[0]
    tm = _pick(m, 512)
    if col0 % LANES or n % LANES:
        w3, col0 = w3[:, :, col0:col0 + n], 0
    return pl.pallas_call(
        _dt_kernel,
        grid=(m // tm,),
        in_specs=[pl.BlockSpec((tm, k), lambda i: (i, 0)),
                  pl.BlockSpec((pl.Element(1), pl.Element(k), pl.Element(n)),
                               lambda i: (layer, 0, col0)),
                  pl.BlockSpec((1, n), lambda i: (0, 0))],
        out_specs=pl.BlockSpec((tm, n), lambda i: (i, 0)),
        out_shape=jax.ShapeDtypeStruct((m, n), F32),
        scratch_shapes=[pltpu.VMEM((k, n), BF16)],
        compiler_params=_params("arbitrary"),
        name="dt_proj",
    )(a, w3, dt_bias.reshape(1, n))


def _rope_slab(x, cos, sin_up, sin_dn):
    return x * cos + pltpu.roll(x, LANES - ROPE_DIM // 2, 1) * sin_up + pltpu.roll(x, ROPE_DIM // 2, 1) * sin_dn


def _rope_kernel(q_ref, k_ref, v_ref, cos_ref, sup_ref, sdn_ref, qo_ref, ko_ref, vo_ref, kb_ref, vb_ref, *, q_scale):
    cos, sup, sdn = cos_ref[...], sup_ref[...], sdn_ref[...]
    for s in range(q_ref.shape[2] // LANES):
        sl = slice(s * LANES, (s + 1) * LANES)
        qo_ref[0, :, sl] = (_rope_slab(q_ref[0, :, sl], cos, sup, sdn) * q_scale).astype(qo_ref.dtype)
    for s in range(k_ref.shape[2] // LANES):
        sl = slice(s * LANES, (s + 1) * LANES)
        k = _rope_slab(k_ref[0, :, sl], cos, sup, sdn)
        ko_ref[0, :, sl] = k
        kb_ref[0, :, sl] = k.astype(kb_ref.dtype)
    v = v_ref[...]
    vo_ref[...] = v
    vb_ref[...] = v.astype(vb_ref.dtype)


def _rope_tables(pos):
    half = ROPE_DIM // 2
    inv = ROPE_THETA ** (-jnp.arange(half, dtype=F32) / half)
    ang = pos.astype(F32)[:, None] * inv[None, :]
    cos, sin = jnp.cos(ang), jnp.sin(ang)
    t = pos.shape[0]
    ones = jnp.ones((t, A_HEAD_DIM - ROPE_DIM), F32)
    zeros = jnp.zeros((t, A_HEAD_DIM - ROPE_DIM), F32)
    zh = jnp.zeros((t, half), F32)
    cos64 = jnp.concatenate([cos, cos, ones], axis=1)
    sup64 = jnp.concatenate([-sin, zh, zeros], axis=1)
    sdn64 = jnp.concatenate([zh, sin, zeros], axis=1)
    rep = lambda a: jnp.concatenate([a, a], axis=1)
    return rep(cos64), rep(sup64), rep(sdn64)


def _rope_kv(proj, pos, d, kvw, k_blk, v_blk):
    b, t, _ = proj.shape
    tr = _pick(t, 256)
    cos, sup, sdn = _rope_tables(pos)
    tab = pl.BlockSpec((tr, LANES), lambda i, j: (j, 0))
    kv_out = pl.BlockSpec((1, tr, kvw), lambda i, j: (i, j, 0))
    return pl.pallas_call(
        functools.partial(_rope_kernel, q_scale=A_HEAD_DIM ** -0.5 * math.log2(math.e)),
        grid=(b, t // tr),
        in_specs=[pl.BlockSpec((1, tr, d), lambda i, j: (i, j, 0)),
                  pl.BlockSpec((1, tr, kvw), lambda i, j: (i, j, k_blk)),
                  pl.BlockSpec((1, tr, kvw), lambda i, j: (i, j, v_blk)),
                  tab, tab, tab],
        out_specs=[pl.BlockSpec((1, tr, d), lambda i, j: (i, j, 0))] + [kv_out] * 4,
        out_shape=[jax.ShapeDtypeStruct((b, t, d), BF16),
                   jax.ShapeDtypeStruct((b, t, kvw), F32),
                   jax.ShapeDtypeStruct((b, t, kvw), F32),
                   jax.ShapeDtypeStruct((b, t, kvw), BF16),
                   jax.ShapeDtypeStruct((b, t, kvw), BF16)],
        compiler_params=_params("arbitrary", "arbitrary"),
        name="rope_kv",
    )(proj, proj, proj, cos, sup, sdn)


def _ssd_kernel(*refs, tl, has_state):
    (alog_ref, alr_ref, dsk_ref, mnw_ref, cwx_ref, cwb_ref, cwc_ref, cbx_ref, cbb_ref, cbc_ref,
     z_ref, x_ref, bm_ref, cm_ref, dt_ref, dtr_ref) = refs[:16]
    if has_state:
        csx_ref, csb_ref, csc_ref, s0_ref = refs[16:20]
        rest = refs[20:]
    else:
        rest = refs[16:]
    y_ref, sout_ref, xbuf, bbuf, cbuf, st_scr = rest
    g = pl.program_id(1)
    c = pl.program_id(2)
    L = CHUNK
    nsub = tl // L
    halo = M_CONV - 1
    base = 8

    @pl.when(c == 0)
    def _():
        if has_state:
            xbuf[base - halo:base, :] = csx_ref[0]
            bbuf[base - halo:base, :] = csb_ref[0]
            cbuf[base - halo:base, :] = csc_ref[0]
            st_scr[...] = s0_ref[0].reshape(GROUP_W, M_STATE).T
        else:
            xbuf[0:base, :] = jnp.zeros((base, GROUP_W), F32)
            bbuf[0:base, :] = jnp.zeros((base, M_STATE), F32)
            cbuf[0:base, :] = jnp.zeros((base, M_STATE), F32)
            st_scr[...] = jnp.zeros_like(st_scr)

    xbuf[base:base + tl, :] = x_ref[0]
    bbuf[base:base + tl, :] = bm_ref[0]
    cbuf[base:base + tl, :] = cm_ref[0]

    r64 = lax.broadcasted_iota(jnp.int32, (L, L), 0)
    c64 = lax.broadcasted_iota(jnp.int32, (L, L), 1)
    tril = (c64 <= r64).astype(BF16)
    heads = dt_ref.shape[2]
    hrow = lax.broadcasted_iota(jnp.int32, (heads, GROUP_W), 0)
    hcol = lax.broadcasted_iota(jnp.int32, (heads, GROUP_W), 1)
    expand = (hrow == g * M_HPG + hcol // M_HEAD_DIM).astype(BF16)
    br = lax.broadcasted_iota(jnp.int32, (LANES, LANES), 0)
    bc = lax.broadcasted_iota(jnp.int32, (LANES, LANES), 1)
    same_half = (br // L) == (bc // L)
    triu2 = (same_half & ((br % L) <= (bc % L))).astype(BF16)
    pr = lax.broadcasted_iota(jnp.int32, (L, LANES), 0)
    pc = lax.broadcasted_iota(jnp.int32, (L, LANES), 1)
    causal2 = (pc % L) <= pr

    a_col = -jnp.exp(alog_ref[...])
    a_row = -jnp.exp(alr_ref[0])
    n_rows = dtr_ref.shape[2]
    da_r = dtr_ref[0, 0] * jnp.concatenate([a_row] * (n_rows // 8), axis=0)
    hi, lo = _split_hi_lo(da_r)
    acum_r_all = _dot(hi, triu2) + _dot(lo, triu2)

    def conv(buf, w_ref, b_ref, i):
        acc = b_ref[...] + w_ref[M_CONV - 1:M_CONV, :] * buf[base + i * L:base + (i + 1) * L, :]
        for k in range(M_CONV - 1):
            off = base - halo + k + i * L
            acc = acc + w_ref[k:k + 1, :] * buf[off:off + L, :]
        return _silu(acc)

    for i in range(nsub):
        rows = slice(i * L, (i + 1) * L)
        x = conv(xbuf, cwx_ref, cbx_ref, i)
        bm = conv(bbuf, cwb_ref, cbb_ref, i).astype(BF16)
        cm = conv(cbuf, cwc_ref, cbc_ref, i).astype(BF16)
        dt = dt_ref[0, rows, :]
        da = dt * a_col
        hi, lo = _split_hi_lo(da)
        acum = _dot(tril, hi) + _dot(tril, lo)
        hi, lo = _split_hi_lo(acum)
        acum_x = _dot(hi, expand) + _dot(lo, expand)
        hi, lo = _split_hi_lo(dt)
        dt_x = _dot(hi, expand) + _dot(lo, expand)
        xdt = x * dt_x
        xdt_b = xdt.astype(BF16)
        alast_x = acum_x[L - 1:L, :]
        cb2 = _dot_nt(cm, jnp.concatenate([bm, bm], axis=0))
        st_b = st_scr[...].astype(BF16)
        y_off = _dot(cm, st_b) * jnp.exp(acum_x)
        y_parts = []
        for jj in range(GROUP_W // LANES):
            ls = slice(jj * LANES, (jj + 1) * LANES)
            seg = acum_x[:, ls] - acum_r_all[i * 8 + jj:i * 8 + jj + 1, :]
            m2 = (jnp.where(causal2, jnp.exp(seg), 0.0) * cb2).astype(BF16)
            x2 = xdt_b[:, ls]
            rhs = jnp.where(same_half, jnp.concatenate([x2, x2], axis=0), jnp.zeros((), BF16))
            y_parts.append(_dot(m2, rhs))
        y = jnp.concatenate(y_parts, axis=1) + y_off + dsk_ref[0] * x
        y = y * _silu(z_ref[0, rows, :])
        ms = jnp.mean(y * y, axis=-1, keepdims=True)
        y_ref[0, rows, :] = (y * lax.rsqrt(ms + EPS) * mnw_ref[0]).astype(y_ref.dtype)
        xw = (xdt * jnp.exp(alast_x - acum_x)).astype(BF16)
        st_scr[...] = st_scr[...] * jnp.exp(alast_x) + _dot_tn(bm, xw)

    tx = xbuf[base + tl - halo:base + tl, :]
    tb = bbuf[base + tl - halo:base + tl, :]
    tc = cbuf[base + tl - halo:base + tl, :]
    xbuf[base - halo:base, :] = tx
    bbuf[base - halo:base, :] = tb
    cbuf[base - halo:base, :] = tc

    @pl.when(c == pl.num_programs(2) - 1)
    def _():
        sout_ref[0] = st_scr[...].T.reshape(M_HPG, M_HEAD_DIM, M_STATE)


def _ssd(proj, dt, conv_state, ssm_state, conv_w, conv_b, a_log, d_skip, mnorm_w, z_off, xbc_off, d_inner):
    b, t, _ = proj.shape
    groups = d_inner // GROUP_W
    heads = groups * M_HPG
    bcw = groups * M_STATE
    has_state = conv_state is not None
    tl = _pick(t, 256, CHUNK)
    nsub = tl // CHUNK
    nc = t // tl
    dtr = dt.reshape(b, t // CHUNK, CHUNK, groups, M_HPG // 2, 2).transpose(0, 3, 1, 4, 5, 2)
    dtr = dtr.reshape(b, groups, (t // CHUNK) * 8, LANES)
    n_rows = nsub * 8
    if n_rows < 16:
        dtr = jnp.concatenate([dtr, jnp.zeros_like(dtr)], axis=2)
        n_rows = 16
    alr = jnp.repeat(a_log.reshape(groups, M_HPG // 2, 2), CHUNK, axis=2)
    dsk = jnp.repeat(d_skip.reshape(groups, 1, M_HPG), M_HEAD_DIM, axis=2)
    mnw = mnorm_w.reshape(groups, 1, GROUP_W)
    xb, bb, cb = xbc_off // GROUP_W, (xbc_off + d_inner) // M_STATE, (xbc_off + d_inner + bcw) // M_STATE
    cxb, cbb, ccb = 0, d_inner // M_STATE, (d_inner + bcw) // M_STATE
    zb = z_off // GROUP_W
    conv_b2 = conv_b.reshape(1, -1)
    gmap = lambda blk: (lambda i, g, c: (0, blk + g))
    in_specs = [
        pl.BlockSpec((1, heads), lambda i, g, c: (0, 0)),
        pl.BlockSpec((1, 8, LANES), lambda i, g, c: (g, 0, 0)),
        pl.BlockSpec((1, 1, GROUP_W), lambda i, g, c: (g, 0, 0)),
        pl.BlockSpec((1, 1, GROUP_W), lambda i, g, c: (g, 0, 0)),
        pl.BlockSpec((M_CONV, GROUP_W), gmap(cxb)),
        pl.BlockSpec((M_CONV, M_STATE), gmap(cbb)),
        pl.BlockSpec((M_CONV, M_STATE), gmap(ccb)),
        pl.BlockSpec((1, GROUP_W), gmap(cxb)),
        pl.BlockSpec((1, M_STATE), gmap(cbb)),
        pl.BlockSpec((1, M_STATE), gmap(ccb)),
        pl.BlockSpec((1, tl, GROUP_W), lambda i, g, c: (i, c, zb + g)),
        pl.BlockSpec((1, tl, GROUP_W), lambda i, g, c: (i, c, xb + g)),
        pl.BlockSpec((1, tl, M_STATE), lambda i, g, c: (i, c, bb + g)),
        pl.BlockSpec((1, tl, M_STATE), lambda i, g, c: (i, c, cb + g)),
        pl.BlockSpec((1, tl, heads), lambda i, g, c: (i, c, 0)),
        pl.BlockSpec((1, 1, n_rows, LANES), lambda i, g, c: (i, g, c, 0)),
    ]
    args = [a_log.reshape(1, heads), alr, dsk, mnw, conv_w, conv_w, conv_w, conv_b2, conv_b2, conv_b2,
            proj, proj, proj, proj, dt.reshape(b, t, heads), dtr]
    if has_state:
        halo = M_CONV - 1
        in_specs += [
            pl.BlockSpec((1, halo, GROUP_W), lambda i, g, c: (i, 0, cxb + g)),
            pl.BlockSpec((1, halo, M_STATE), lambda i, g, c: (i, 0, cbb + g)),
            pl.BlockSpec((1, halo, M_STATE), lambda i, g, c: (i, 0, ccb + g)),
            pl.BlockSpec((1, M_HPG, M_HEAD_DIM, M_STATE), lambda i, g, c: (i, g, 0, 0)),
        ]
        args += [conv_state, conv_state, conv_state, ssm_state]
    return pl.pallas_call(
        functools.partial(_ssd_kernel, tl=tl, has_state=has_state),
        grid=(b, groups, nc),
        in_specs=in_specs,
        out_specs=[pl.BlockSpec((1, tl, GROUP_W), lambda i, g, c: (i, c, g)),
                   pl.BlockSpec((1, M_HPG, M_HEAD_DIM, M_STATE), lambda i, g, c: (i, g, 0, 0))],
        out_shape=[jax.ShapeDtypeStruct((b, t, d_inner), BF16),
                   jax.ShapeDtypeStruct((b, heads, M_HEAD_DIM, M_STATE), F32)],
        scratch_shapes=[pltpu.VMEM((tl + 8, GROUP_W), F32),
                        pltpu.VMEM((tl + 8, M_STATE), F32),
                        pltpu.VMEM((tl + 8, M_STATE), F32),
                        pltpu.VMEM((M_STATE, GROUP_W), F32)],
        compiler_params=_params("arbitrary", "arbitrary", "arbitrary"),
        name="ssd",
    )(*args)


def _attn_kernel(*refs, tq, tk, rc, n_qt, past, tkp, pos0, lam_init):
    lq1_ref, lk1_ref, lq2_ref, lk2_ref, nw_ref, q_ref, k_ref, v_ref, za_ref = refs[:9]
    if past:
        kp_ref, vp_ref = refs[9:11]
        rest = refs[11:]
    else:
        rest = refs[9:]
    o_ref, qz_scr, s_scr, p_scr, m_scr, l_scr, acc_scr = rest
    qi = pl.program_id(2)
    rows = A_REP * tq

    lane = lax.broadcasted_iota(jnp.int32, (tq, KV_W), 1)
    zero = jnp.zeros((), BF16)
    for r in range(A_REP):
        q = q_ref[0, :, r * KV_W:(r + 1) * KV_W]
        qz_scr[0, r * tq:(r + 1) * tq, :] = jnp.where(lane < A_HEAD_DIM, q, zero)
        qz_scr[1, r * tq:(r + 1) * tq, :] = jnp.where(lane >= A_HEAD_DIM, q, zero)
    m_scr[...] = jnp.full(m_scr.shape, -jnp.inf, F32)
    l_scr[...] = jnp.zeros(l_scr.shape, F32)
    acc_scr[...] = jnp.zeros(acc_scr.shape, F32)

    def tile(k, v, nkeys, mask_fn):
        reps = nkeys // LANES
        for comp in range(2):
            s_scr[comp, :, :nkeys] = _dot_nt(qz_scr[comp], k)
        for comp in range(2):
            for i in range(rows // rc):
                rs = slice(i * rc, (i + 1) * rc)
                s = s_scr[comp, rs, :nkeys]
                if mask_fn is not None:
                    s = jnp.where(mask_fn(i), s, -jnp.inf)
                m_old = m_scr[comp, rs, :]
                m_new = jnp.maximum(m_old, jnp.max(s, axis=-1, keepdims=True))
                p = jnp.exp2(s - jnp.concatenate([m_new] * reps, axis=1))
                alpha = jnp.exp2(m_old - m_new)
                psum = p[:, :LANES]
                for u in range(1, reps):
                    psum = psum + p[:, u * LANES:(u + 1) * LANES]
                l_scr[comp, rs, :] = alpha * l_scr[comp, rs, :] + psum
                acc_scr[comp, rs, :] = alpha * acc_scr[comp, rs, :]
                m_scr[comp, rs, :] = m_new
                p_scr[comp, rs, :nkeys] = p.astype(BF16)
            acc_scr[comp] = acc_scr[comp] + _dot(p_scr[comp, :, :nkeys], v)

    def diag_mask(i, nkeys, n_valid, k_pos0):
        t0 = (i * rc) % tq
        qpos = pos0 + qi * tq + t0 + lax.broadcasted_iota(jnp.int32, (rc, nkeys), 0)
        kidx = lax.broadcasted_iota(jnp.int32, (rc, nkeys), 1)
        ok = ((k_pos0 + kidx) // CHUNK) <= (qpos // CHUNK)
        if n_valid < nkeys:
            ok = ok & (kidx < n_valid)
        return ok

    if past:
        def pbody(j, carry):
            sl = pl.ds(pl.multiple_of(j * tkp, tkp), tkp)
            tile(kp_ref[0, sl, :].astype(BF16), vp_ref[0, sl, :].astype(BF16), tkp, None)
            return carry

        lax.fori_loop(0, past // tkp, pbody, 0)

    def nbody(j, carry):
        sl = pl.ds(pl.multiple_of(j * tk, tk), tk)
        tile(k_ref[0, sl, :], v_ref[0, sl, :], tk, None)
        return carry

    if n_qt > 1:
        lax.fori_loop(0, (qi * tq) // tk, nbody, 0)
    tkd = max(tk, LANES)
    for d in range(tq // tk):
        k0 = qi * tq + d * tk
        sl = pl.ds(pl.multiple_of(k0, tk), tk)
        k, v = k_ref[0, sl, :], v_ref[0, sl, :]
        if tkd > tk:
            pad = jnp.zeros((tkd - tk, KV_W), BF16)
            k, v = jnp.concatenate([k, pad], axis=0), jnp.concatenate([v, pad], axis=0)
        tile(k, v, tkd, functools.partial(diag_mask, nkeys=tkd, n_valid=tk, k_pos0=pos0 + k0))

    lam = (jnp.exp(jnp.sum(lq1_ref[...] * lk1_ref[...], axis=-1, keepdims=True))
           - jnp.exp(jnp.sum(lq2_ref[...] * lk2_ref[...], axis=-1, keepdims=True)) + lam_init)
    l0 = jnp.sum(l_scr[0], axis=-1, keepdims=True)
    l1 = jnp.sum(l_scr[1], axis=-1, keepdims=True)
    o = acc_scr[0] / l0 - lam * (acc_scr[1] / l1)
    ms = jnp.mean(o * o, axis=-1, keepdims=True)
    on = o * lax.rsqrt(ms + EPS) * nw_ref[...] * (1.0 - lam_init)
    for r in range(A_REP):
        sl = slice(r * KV_W, (r + 1) * KV_W)
        o_ref[0, :, sl] = (on[r * tq:(r + 1) * tq, :] * _silu(za_ref[0, :, sl])).astype(o_ref.dtype)


def _attention(qr, kb, vb, proj, za_off, k_past, v_past, lam_vecs, norm_w, lam_init, pos0):
    b, t, d = qr.shape
    kvh = d // QH_W
    past = 0 if k_past is None else k_past.shape[1]
    tq = _pick(t, 512, CHUNK)
    tk = tq
    tkp = _pick(past, 512, LANES) if past else 0
    rc = min(64, tq)
    zb = za_off // QH_W
    rows = A_REP * tq
    smax = max(tk, LANES, tkp)
    vec = pl.BlockSpec((1, A_HEAD_DIM), lambda i, h, j: (0, 0))
    in_specs = [vec, vec, vec, vec,
                pl.BlockSpec((1, KV_W), lambda i, h, j: (0, 0)),
                pl.BlockSpec((1, tq, QH_W), lambda i, h, j: (i, j, h)),
                pl.BlockSpec((1, t, KV_W), lambda i, h, j: (i, 0, h)),
                pl.BlockSpec((1, t, KV_W), lambda i, h, j: (i, 0, h)),
                pl.BlockSpec((1, tq, QH_W), lambda i, h, j: (i, j, zb + h))]
    args = [v.reshape(1, A_HEAD_DIM) for v in lam_vecs] + [norm_w.reshape(1, KV_W), qr, kb, vb, proj]
    if past:
        in_specs += [pl.BlockSpec((1, past, KV_W), lambda i, h, j: (i, 0, h)),
                     pl.BlockSpec((1, past, KV_W), lambda i, h, j: (i, 0, h))]
        args += [k_past, v_past]
    return pl.pallas_call(
        functools.partial(_attn_kernel, tq=tq, tk=tk, rc=rc, n_qt=t // tq, past=past, tkp=tkp, pos0=pos0,
                          lam_init=lam_init),
        grid=(b, kvh, t // tq),
        in_specs=in_specs,
        out_specs=pl.BlockSpec((1, tq, QH_W), lambda i, h, j: (i, j, h)),
        out_shape=jax.ShapeDtypeStruct((b, t, d), BF16),
        scratch_shapes=[pltpu.VMEM((2, rows, KV_W), BF16),
                        pltpu.VMEM((2, rows, smax), F32),
                        pltpu.VMEM((2, rows, smax), BF16),
                        pltpu.VMEM((2, rows, LANES), F32),
                        pltpu.VMEM((2, rows, LANES), F32),
                        pltpu.VMEM((2, rows, KV_W), F32)],
        compiler_params=_params("arbitrary", "arbitrary", "arbitrary"),
        name="diff_attn",
    )(*args)


def _merge_kernel(ym_ref, ya_ref, wm_ref, wa_ref, gm_ref, ga_ref, o_ref):
    pm = _dot(ym_ref[...], wm_ref[...])
    pa = _dot(ya_ref[...], wa_ref[...])
    o_ref[...] = (jax.nn.sigmoid(gm_ref[...]) * pm + jax.nn.sigmoid(ga_ref[...]) * pa).astype(o_ref.dtype)


def _merge(y_m, y_a, w_pm, w_pa, proj2d, gm_off, ga_off):
    m, km = y_m.shape
    ka = y_a.shape[1]
    d = w_pm.shape[1]
    tm = _pick(m, 256)
    tn = _pick(d, 512, LANES)
    gmb, gab = gm_off // tn, ga_off // tn
    return pl.pallas_call(
        _merge_kernel,
        grid=(d // tn, m // tm),
        in_specs=[pl.BlockSpec((tm, km), lambda j, i: (i, 0)),
                  pl.BlockSpec((tm, ka), lambda j, i: (i, 0)),
                  pl.BlockSpec((km, tn), lambda j, i: (0, j)),
                  pl.BlockSpec((ka, tn), lambda j, i: (0, j)),
                  pl.BlockSpec((tm, tn), lambda j, i: (i, gmb + j)),
                  pl.BlockSpec((tm, tn), lambda j, i: (i, gab + j))],
        out_specs=pl.BlockSpec((tm, tn), lambda j, i: (i, j)),
        out_shape=jax.ShapeDtypeStruct((m, d), BF16),
        compiler_params=_params("arbitrary", "arbitrary"),
        name="merge",
    )(y_m, y_a, w_pm, w_pa, proj2d, proj2d)


def _out_kernel(mg_ref, w_ref, x_ref, gate_ref, fw_ref, o_ref, *, tn):
    j = pl.program_id(2)
    cols = pl.ds(pl.multiple_of(j * tn, tn), tn)
    o_ref[0, :, cols] = x_ref[0] + gate_ref[0] * _dot(mg_ref[0], w_ref[...])

    @pl.when(j == pl.num_programs(2) - 1)
    def _():
        r = o_ref[0]
        ms = jnp.mean(r * r, axis=-1, keepdims=True)
        o_ref[0] = r * lax.rsqrt(ms + EPS) * fw_ref[...]


def _out_proj(merged, w_out, x, gate, final_w):
    shape = x.shape
    b, t, d = shape
    gate = gate.reshape(b, 1, d)
    per_row_gate = t < 512 and b > 1
    if per_row_gate:
        gate = jnp.broadcast_to(gate, (b, t, d)).reshape(1, b * t, d)
        x = x.reshape(1, b * t, d)
        b, t = 1, b * t
    tm = _pick(t, 512)
    tn = _pick(d, 512, LANES)
    gate_spec =(pl.BlockSpec((1, tm, tn), lambda i, r, j: (i, r, j)) if per_row_gate else
                 pl.BlockSpec((1, 1, tn), lambda i, r, j: (i, 0, j)))
    out = pl.pallas_call(
        functools.partial(_out_kernel, tn=tn),
        grid=(b, t // tm, d // tn),
        in_specs=[pl.BlockSpec((1, tm, d), lambda i, r, j: (i, r, 0)),
                  pl.BlockSpec((d, tn), lambda i, r, j: (0, j)),
                  pl.BlockSpec((1, tm, tn), lambda i, r, j: (i, r, j)),
                  gate_spec,
                  pl.BlockSpec((1, d), lambda i, r, j: (0, 0))],
        out_specs=pl.BlockSpec((1, tm, d), lambda i, r, j: (i, r, 0)),
        out_shape=jax.ShapeDtypeStruct((b, t, d), F32),
        compiler_params=_params("arbitrary", "arbitrary", "arbitrary"),
        name="out_proj",
    )(merged.reshape(b, t, d), w_out, x, gate, final_w.reshape(1, d))
    return out.reshape(shape)


def _layer_path(x, mod, k_past, v_past, conv_state, ssm_state, pos0, w, lam_init, final_w):
    b, t, d = x.shape
    d_inner = 2 * d
    groups = d_inner // GROUP_W
    bcw = groups * M_STATE
    conv_dim = d_inner + 2 * bcw
    kvw = (d // QH_W) * KV_W
    z_off, xbc_off = 0, d_inner
    k_off, v_off, za_off, gm_off, ga_off = d, d + kvw, d + 2 * kvw, 2 * d + 2 * kvw, 3 * d + 2 * kvw
    assert k_off % kvw == 0 and za_off % QH_W == 0, "consumer column blocks must be block-aligned"
    shift, scale, gate = mod[:, :d], mod[:, d:2 * d], mod[:, 2 * d:]

    h = _prenorm(x, w['norm_w'], scale, shift).reshape(b * t, d)
    proj_m = _matmul_f32w(h, w['w_in'], w['layer'], 0, xbc_off + conv_dim, "in_proj_m").reshape(b, t, -1)
    proj_a = _matmul_f32w(h, w['w_in'], w['layer'], w['a_col0'], ga_off + d, "in_proj_a").reshape(b, t, -1)
    dt = _dt_proj(h, w['w_in'], w['layer'], w['dt_col0'], w['dt_bias']).reshape(b, t, -1)

    pos = pos0 + jnp.arange(t, dtype=jnp.int32)
    qr, k_new, v_new, kb, vb = _rope_kv(proj_a, pos, d, kvw, k_off // kvw, v_off // kvw)

    y_m, ssm_new = _ssd(proj_m, dt, conv_state, ssm_state, w['conv_w'], w['conv_b'], w['a_log'], w['d_skip'],
                        w['mamba_norm_w'], z_off, xbc_off, d_inner)
    halo = M_CONV - 1
    conv_new = proj_m[:, t - halo:, xbc_off:xbc_off + conv_dim]

    y_a = _attention(qr, kb, vb, proj_a, za_off, k_past, v_past,
                     (w['lam_q1'], w['lam_k1'], w['lam_q2'], w['lam_k2']), w['attn_norm_w'], lam_init, pos0)

    merged = _merge(y_m.reshape(b * t, d_inner), y_a.reshape(b * t, d), w['w_proj_m'], w['w_proj_a'],
                    proj_a.reshape(b * t, -1), gm_off, ga_off)
    y = _out_proj(merged, w['w_out'], x, gate, final_w)
    kvh = d // QH_W
    return (y, k_new.reshape(b, t, kvh, 2, A_HEAD_DIM), v_new.reshape(b, t, kvh, KV_W), conv_new, ssm_new)


def kernel(x_prompt, x_sample, cache_k, cache_v, state_conv, state_ssm, c_prompt, c_sample,
           w_ada, b_ada, norm_w, w_in, conv_w, conv_b, dt_bias, a_log, d_skip, mamba_norm_w,
           lam_q1, lam_k1, lam_q2, lam_k2, attn_norm_w, w_proj_m, w_proj_a, w_out, final_norm_w):
    depth = w_in.shape[0]
    assert depth == 1, "the final norm is fused into the single layer's output projection"
    bp, d = c_prompt.shape
    bs = c_sample.shape[0]
    past = cache_k.shape[2]
    d_inner = 2 * d
    groups = d_inner // GROUP_W
    heads = groups * M_HPG
    conv_dim = d_inner + 2 * groups * M_STATE
    kvw = (d // QH_W) * KV_W
    sizes = (d_inner, conv_dim, heads, d, kvw, kvw, d, d, d)
    offs = [0]
    for s in sizes:
        offs.append(offs[-1] + s)

    i = 0
    wi = w_in[i]
    w = {
        'w_in': w_in, 'layer': i, 'dt_col0': offs[2], 'a_col0': offs[3],
        'norm_w': norm_w[i], 'conv_w': conv_w[i], 'conv_b': conv_b[i], 'dt_bias': dt_bias[i], 'a_log': a_log[i],
        'd_skip': d_skip[i], 'mamba_norm_w': mamba_norm_w[i], 'lam_q1': lam_q1[i], 'lam_k1': lam_k1[i],
        'lam_q2': lam_q2[i], 'lam_k2': lam_k2[i], 'attn_norm_w': attn_norm_w[i],
        'w_proj_m': w_proj_m[i].astype(BF16), 'w_proj_a': w_proj_a[i].astype(BF16), 'w_out': w_out[i].astype(BF16),
    }
    lam_init = 0.8 - 0.6 * math.exp(-0.3 * i)
    pad_rows = -(bp + bs) % 16
    c_all = jnp.concatenate([c_prompt, c_sample, jnp.zeros((pad_rows, d), F32)], axis=0)
    mod = _ada_mod(c_all, w_ada[i], b_ada[i])

    yp, kp, vp, cp, sp = _layer_path(x_prompt, mod[:bp], None, None, None, None, 0, w, lam_init, final_norm_w)
    ck = cache_k[i].reshape(bs, past, kvw)
    cv = cache_v[i].reshape(bs, past, kvw)
    ys, kq, vq, cq, sq = _layer_path(x_sample, mod[bp:bp + bs], ck, cv, state_conv[i], state_ssm[i], past, w, lam_init,
                                     final_norm_w)
    st = lambda a: a[None]
    return (yp, ys, st(kp), st(vp), st(cp), st(sp), st(kq), st(vq), st(cq), st(sq))
```

```python
import functools
import math

import jax
import jax.numpy as jnp
from jax import lax
from jax.experimental import pallas as pl
from jax.experimental.pallas import tpu as pltpu

F32 = jnp.float32
BF16 = jnp.bfloat16

CHUNK = 64
EPS = 1e-6
M_HEAD_DIM = 64
M_HPG = 16
M_STATE = 128
M_CONV = 4
GROUP_W = M_HPG * M_HEAD_DIM
A_HEAD_DIM = 64
A_REP = 4
KV_W = 2 * A_HEAD_DIM
QH_W = A_REP * KV_W
ROPE_DIM = 16
ROPE_THETA = 500000.0
LANES = 128
VMEM_LIMIT_BYTES = 56 * 1024 * 1024


def _pick(n, target, mult=8):
    if n <= target:
        return n
    for t in range(target, 0, -1):
        if n % t == 0 and t % mult == 0:
            return t
    return n


def _params(*sem):
    return pltpu.CompilerParams(dimension_semantics=sem, vmem_limit_bytes=VMEM_LIMIT_BYTES)


def _silu(x):
    return x * jax.nn.sigmoid(x)


def _split_hi_lo(x):
    hi = x.astype(BF16)
    lo = (x - hi.astype(F32)).astype(BF16)
    return hi, lo


def _dot(a, b):
    return jnp.dot(a, b, preferred_element_type=F32)


def _dot_nt(a, b):
    return lax.dot_general(a, b, (((1,), (1,)), ((), ())), preferred_element_type=F32)


def _dot_tn(a, b):
    return lax.dot_general(a, b, (((0,), (0,)), ((), ())), preferred_element_type=F32)


def _ada_kernel(c_ref, w_ref, b_ref, o_ref):
    a = _silu(c_ref[...]).astype(BF16)
    o_ref[...] = _dot(a, w_ref[...].astype(BF16)) + b_ref[...]


def _ada_mod(c, w_ada, b_ada):
    m, d = c.shape
    n = w_ada.shape[1]
    tn = _pick(n, 512, LANES)
    return pl.pallas_call(
        _ada_kernel,
        grid=(n // tn,),
        in_specs=[pl.BlockSpec((m, d), lambda j: (0, 0)),
                  pl.BlockSpec((d, tn), lambda j: (0, j)),
                  pl.BlockSpec((1, tn), lambda j: (0, j))],
        out_specs=pl.BlockSpec((m, tn), lambda j: (0, j)),
        out_shape=jax.ShapeDtypeStruct((m, n), F32),
        compiler_params=_params("arbitrary"),
        name="ada_mod",
    )(c, w_ada, b_ada.reshape(1, n))


def _prenorm_kernel(x_ref, nw_ref, sc_ref, sh_ref, o_ref):
    x = x_ref[0]
    ms = jnp.mean(x * x, axis=-1, keepdims=True)
    y = x * lax.rsqrt(ms + EPS) * nw_ref[...]
    o_ref[0] = (y * (1.0 + sc_ref[0]) + sh_ref[0]).astype(o_ref.dtype)


def _prenorm(x, norm_w, scale, shift):
    b, t, d = x.shape
    tr = _pick(t, 256)
    return pl.pallas_call(
        _prenorm_kernel,
        grid=(b, t // tr),
        in_specs=[pl.BlockSpec((1, tr, d), lambda i, j: (i, j, 0)),
                  pl.BlockSpec((1, d), lambda i, j: (0, 0)),
                  pl.BlockSpec((1, 1, d), lambda i, j: (i, 0, 0)),
                  pl.BlockSpec((1, 1, d), lambda i, j: (i, 0, 0))],
        out_specs=pl.BlockSpec((1, tr, d), lambda i, j: (i, j, 0)),
        out_shape=jax.ShapeDtypeStruct((b, t, d), BF16),
        compiler_params=_params("arbitrary", "arbitrary"),
        name="prenorm",
    )(x, norm_w.reshape(1, d), scale.reshape(b, 1, d), shift.reshape(b, 1, d))


def _mm_kernel(a_ref, w_ref, o_ref):
    o_ref[...] = _dot(a_ref[...], w_ref[...]).astype(o_ref.dtype)


def _mm_castw_kernel(a_ref, w_ref, o_ref):
    o_ref[...] = _dot(a_ref[...], w_ref[0].astype(BF16))


def _matmul_f32w(a, w3, layer, col0, n, name):
    m, k = a.shape
    tm = _pick(m, 1024)
    tn = _pick(n, 512, LANES)
    if col0 % LANES:
        w3, col0 = w3[:, :, col0:col0 + n], 0
    return pl.pallas_call(
        _mm_castw_kernel,
        grid=(m // tm, n // tn),
        in_specs=[pl.BlockSpec((tm, k), lambda i, j: (i, 0)),
                  pl.BlockSpec((pl.Element(1), pl.Element(k), pl.Element(tn)),
                               lambda i, j: (layer, 0, pl.multiple_of(col0 + j * tn, LANES)))],
        out_specs=pl.BlockSpec((tm, tn), lambda i, j: (i, j)),
        out_shape=jax.ShapeDtypeStruct((m, n), F32),
        compiler_params=_params("arbitrary", "arbitrary"),
        name=name,
    )(a, w3)


def _matmul(a, w, name):
    m, k = a.shape
    n = w.shape[1]
    tm = _pick(m, 1024)
    tn = _pick(n, 1024, LANES)
    return pl.pallas_call(
        _mm_kernel,
        grid=(n // tn, m // tm),
        in_specs=[pl.BlockSpec((tm, k), lambda j, i: (i, 0)),
                  pl.BlockSpec((k, tn), lambda j, i: (0, j))],
        out_specs=pl.BlockSpec((tm, tn), lambda j, i: (i, j)),
        out_shape=jax.ShapeDtypeStruct((m, n), F32),
        compiler_params=_params("arbitrary", "arbitrary"),
        name=name,
    )(a, w)


def _dt_kernel(a_ref, w_ref, b_ref, o_ref, wb_scr):
    @pl.when(pl.program_id(0) == 0)
    def _():
        wb_scr[...] = w_ref[0].astype(BF16)

    x = _dot(a_ref[...], wb_scr[...]) + b_ref[...]
    o_ref[...] = jnp.maximum(x, 0.0) + jnp.log1p(jnp.exp(-jnp.abs(x)))


def _dt_proj(a, w3, layer, col0, dt_bias):
    m, k = a.shape
    n = dt_bias.shape[0]
    tm = _pick(m, 512)
    if col0 % LANES or n % LANES:
        w3, col0 = w3[:, :, col0:col0 + n], 0
    return pl.pallas_call(
        _dt_kernel,
        grid=(m // tm,),
        in_specs=[pl.BlockSpec((tm, k), lambda i: (i, 0)),
                  pl.BlockSpec((pl.Element(1), pl.Element(k), pl.Element(n)),
                               lambda i: (layer, 0, col0)),
                  pl.BlockSpec((1, n), lambda i: (0, 0))],
        out_specs=pl.BlockSpec((tm, n), lambda i: (i, 0)),
        out_shape=jax.ShapeDtypeStruct((m, n), F32),
        scratch_shapes=[pltpu.VMEM((k, n), BF16)],
        compiler_params=_params("arbitrary"),
        name="dt_proj",
    )(a, w3, dt_bias.reshape(1, n))


def _rope_slab(x, cos, sin_up, sin_dn):
    return x * cos + pltpu.roll(x, LANES - ROPE_DIM // 2, 1) * sin_up + pltpu.roll(x, ROPE_DIM // 2, 1) * sin_dn


def _rope_kernel(q_ref, k_ref, v_ref, cos_ref, sup_ref, sdn_ref, qo_ref, ko_ref, vo_ref, kb_ref, vb_ref, *, q_scale):
    cos, sup, sdn = cos_ref[...], sup_ref[...], sdn_ref[...]
    for s in range(q_ref.shape[2] // LANES):
        sl = slice(s * LANES, (s + 1) * LANES)
        qo_ref[0, :, sl] = (_rope_slab(q_ref[0, :, sl], cos, sup, sdn) * q_scale).astype(qo_ref.dtype)
    for s in range(k_ref.shape[2] // LANES):
        sl = slice(s * LANES, (s + 1) * LANES)
        k = _rope_slab(k_ref[0, :, sl], cos, sup, sdn)
        ko_ref[0, :, sl] = k
        kb_ref[0, :, sl] = k.astype(kb_ref.dtype)
    v = v_ref[...]
    vo_ref[...] = v
    vb_ref[...] = v.astype(vb_ref.dtype)


def _rope_tables(pos):
    half = ROPE_DIM // 2
    inv = ROPE_THETA ** (-jnp.arange(half, dtype=F32) / half)
    ang = pos.astype(F32)[:, None] * inv[None, :]
    cos, sin = jnp.cos(ang), jnp.sin(ang)
    t = pos.shape[0]
    ones = jnp.ones((t, A_HEAD_DIM - ROPE_DIM), F32)
    zeros = jnp.zeros((t, A_HEAD_DIM - ROPE_DIM), F32)
    zh = jnp.zeros((t, half), F32)
    cos64 = jnp.concatenate([cos, cos, ones], axis=1)
    sup64 = jnp.concatenate([-sin, zh, zeros], axis=1)
    sdn64 = jnp.concatenate([zh, sin, zeros], axis=1)
    rep = lambda a: jnp.concatenate([a, a], axis=1)
    return rep(cos64), rep(sup64), rep(sdn64)


def _rope_kv(proj, pos, d, kvw, k_blk, v_blk):
    b, t, _ = proj.shape
    tr = _pick(t, 256)
    cos, sup, sdn = _rope_tables(pos)
    tab = pl.BlockSpec((tr, LANES), lambda i, j: (j, 0))
    kv_out = pl.BlockSpec((1, tr, kvw), lambda i, j: (i, j, 0))
    return pl.pallas_call(
        functools.partial(_rope_kernel, q_scale=A_HEAD_DIM ** -0.5 * math.log2(math.e)),
        grid=(b, t // tr),
        in_specs=[pl.BlockSpec((1, tr, d), lambda i, j: (i, j, 0)),
                  pl.BlockSpec((1, tr, kvw), lambda i, j: (i, j, k_blk)),
                  pl.BlockSpec((1, tr, kvw), lambda i, j: (i, j, v_blk)),
                  tab, tab, tab],
        out_specs=[pl.BlockSpec((1, tr, d), lambda i, j: (i, j, 0))] + [kv_out] * 4,
        out_shape=[jax.ShapeDtypeStruct((b, t, d), BF16),
                   jax.ShapeDtypeStruct((b, t, kvw), F32),
                   jax.ShapeDtypeStruct((b, t, kvw), F32),
                   jax.ShapeDtypeStruct((b, t, kvw), BF16),
                   jax.ShapeDtypeStruct((b, t, kvw), BF16)],
        compiler_params=_params("arbitrary", "arbitrary"),
        name="rope_kv",
    )(proj, proj, proj, cos, sup, sdn)


def _ssd_kernel(*refs, tl, has_state):
    (alog_ref, alr_ref, dsk_ref, mnw_ref, cwx_ref, cwb_ref, cwc_ref, cbx_ref, cbb_ref, cbc_ref,
     z_ref, x_ref, bm_ref, cm_ref, dt_ref, dtr_ref) = refs[:16]
    if has_state:
        csx_ref, csb_ref, csc_ref, s0_ref = refs[16:20]
        rest = refs[20:]
    else:
        rest = refs[16:]
    y_ref, sout_ref, xbuf, bbuf, cbuf, st_scr = rest
    g = pl.program_id(1)
    c = pl.program_id(2)
    L = CHUNK
    nsub = tl // L
    halo = M_CONV - 1
    base = 8

    @pl.when(c == 0)
    def _():
        if has_state:
            xbuf[base - halo:base, :] = csx_ref[0]
            bbuf[base - halo:base, :] = csb_ref[0]
            cbuf[base - halo:base, :] = csc_ref[0]
            st_scr[...] = s0_ref[0].reshape(GROUP_W, M_STATE).T
        else:
            xbuf[0:base, :] = jnp.zeros((base, GROUP_W), F32)
            bbuf[0:base, :] = jnp.zeros((base, M_STATE), F32)
            cbuf[0:base, :] = jnp.zeros((base, M_STATE), F32)
            st_scr[...] = jnp.zeros_like(st_scr)

    xbuf[base:base + tl, :] = x_ref[0]
    bbuf[base:base + tl, :] = bm_ref[0]
    cbuf[base:base + tl, :] = cm_ref[0]

    r64 = lax.broadcasted_iota(jnp.int32, (L, L), 0)
    c64 = lax.broadcasted_iota(jnp.int32, (L, L), 1)
    tril = (c64 <= r64).astype(BF16)
    heads = dt_ref.shape[2]
    hrow = lax.broadcasted_iota(jnp.int32, (heads, GROUP_W), 0)
    hcol = lax.broadcasted_iota(jnp.int32, (heads, GROUP_W), 1)
    expand = (hrow == g * M_HPG + hcol // M_HEAD_DIM).astype(BF16)
    br = lax.broadcasted_iota(jnp.int32, (LANES, LANES), 0)
    bc = lax.broadcasted_iota(jnp.int32, (LANES, LANES), 1)
    same_half = (br // L) == (bc // L)
    triu2 = (same_half & ((br % L) <= (bc % L))).astype(BF16)
    pr = lax.broadcasted_iota(jnp.int32, (L, LANES), 0)
    pc = lax.broadcasted_iota(jnp.int32, (L, LANES), 1)
    causal2 = (pc % L) <= pr

    a_col = -jnp.exp(alog_ref[...])
    a_row = -jnp.exp(alr_ref[0])
    n_rows = dtr_ref.shape[2]
    da_r = dtr_ref[0, 0] * jnp.concatenate([a_row] * (n_rows // 8), axis=0)
    hi, lo = _split_hi_lo(da_r)
    acum_r_all = _dot(hi, triu2) + _dot(lo, triu2)

    def conv(buf, w_ref, b_ref, i):
        acc = b_ref[...] + w_ref[M_CONV - 1:M_CONV, :] * buf[base + i * L:base + (i + 1) * L, :]
        for k in range(M_CONV - 1):
            off = base - halo + k + i * L
            acc = acc + w_ref[k:k + 1, :] * buf[off:off + L, :]
        return _silu(acc)

    for i in range(nsub):
        rows = slice(i * L, (i + 1) * L)
        x = conv(xbuf, cwx_ref, cbx_ref, i)
        bm = conv(bbuf, cwb_ref, cbb_ref, i).astype(BF16)
        cm = conv(cbuf, cwc_ref, cbc_ref, i).astype(BF16)
        dt = dt_ref[0, rows, :]
        da = dt * a_col
        hi, lo = _split_hi_lo(da)
        acum = _dot(tril, hi) + _dot(tril, lo)
        hi, lo = _split_hi_lo(acum)
        acum_x = _dot(hi, expand) + _dot(lo, expand)
        hi, lo = _split_hi_lo(dt)
        dt_x = _dot(hi, expand) + _dot(lo, expand)
        xdt = x * dt_x
        xdt_b = xdt.astype(BF16)
        alast_x = acum_x[L - 1:L, :]
        cb2 = _dot_nt(cm, jnp.concatenate([bm, bm], axis=0))
        st_b = st_scr[...].astype(BF16)
        y_off = _dot(cm, st_b) * jnp.exp(acum_x)
        y_parts = []
        for jj in range(GROUP_W // LANES):
            ls = slice(jj * LANES, (jj + 1) * LANES)
            seg = acum_x[:, ls] - acum_r_all[i * 8 + jj:i * 8 + jj + 1, :]
            m2 = (jnp.where(causal2, jnp.exp(seg), 0.0) * cb2).astype(BF16)
            x2 = xdt_b[:, ls]
            rhs = jnp.where(same_half, jnp.concatenate([x2, x2], axis=0), jnp.zeros((), BF16))
            y_parts.append(_dot(m2, rhs))
        y = jnp.concatenate(y_parts, axis=1) + y_off + dsk_ref[0] * x
        y = y * _silu(z_ref[0, rows, :])
        ms = jnp.mean(y * y, axis=-1, keepdims=True)
        y_ref[0, rows, :] = (y * lax.rsqrt(ms + EPS) * mnw_ref[0]).astype(y_ref.dtype)
        xw = (xdt * jnp.exp(alast_x - acum_x)).astype(BF16)
        st_scr[...] = st_scr[...] * jnp.exp(alast_x) + _dot_tn(bm, xw)

    tx = xbuf[base + tl - halo:base + tl, :]
    tb = bbuf[base + tl - halo:base + tl, :]
    tc = cbuf[base + tl - halo:base + tl, :]
    xbuf[base - halo:base, :] = tx
    bbuf[base - halo:base, :] = tb
    cbuf[base - halo:base, :] = tc

    @pl.when(c == pl.num_programs(2) - 1)
    def _():
        sout_ref[0] = st_scr[...].T.reshape(M_HPG, M_HEAD_DIM, M_STATE)


def _ssd(proj, dt, conv_state, ssm_state, conv_w, conv_b, a_log, d_skip, mnorm_w, z_off, xbc_off, d_inner):
    b, t, _ = proj.shape
    groups = d_inner // GROUP_W
    heads = groups * M_HPG
    bcw = groups * M_STATE
    has_state = conv_state is not None
    tl = _pick(t, 256, CHUNK)
    nsub = tl // CHUNK
    nc = t // tl
    dtr = dt.reshape(b, t // CHUNK, CHUNK, groups, M_HPG // 2, 2).transpose(0, 3, 1, 4, 5, 2)
    dtr = dtr.reshape(b, groups, (t // CHUNK) * 8, LANES)
    n_rows = nsub * 8
    if n_rows < 16:
        dtr = jnp.concatenate([dtr, jnp.zeros_like(dtr)], axis=2)
        n_rows = 16
    alr = jnp.repeat(a_log.reshape(groups, M_HPG // 2, 2), CHUNK, axis=2)
    dsk = jnp.repeat(d_skip.reshape(groups, 1, M_HPG), M_HEAD_DIM, axis=2)
    mnw = mnorm_w.reshape(groups, 1, GROUP_W)
    xb, bb, cb = xbc_off // GROUP_W, (xbc_off + d_inner) // M_STATE, (xbc_off + d_inner + bcw) // M_STATE
    cxb, cbb, ccb = 0, d_inner // M_STATE, (d_inner + bcw) // M_STATE
    zb = z_off // GROUP_W
    conv_b2 = conv_b.reshape(1, -1)
    gmap = lambda blk: (lambda i, g, c: (0, blk + g))
    in_specs = [
        pl.BlockSpec((1, heads), lambda i, g, c: (0, 0)),
        pl.BlockSpec((1, 8, LANES), lambda i, g, c: (g, 0, 0)),
        pl.BlockSpec((1, 1, GROUP_W), lambda i, g, c: (g, 0, 0)),
        pl.BlockSpec((1, 1, GROUP_W), lambda i, g, c: (g, 0, 0)),
        pl.BlockSpec((M_CONV, GROUP_W), gmap(cxb)),
        pl.BlockSpec((M_CONV, M_STATE), gmap(cbb)),
        pl.BlockSpec((M_CONV, M_STATE), gmap(ccb)),
        pl.BlockSpec((1, GROUP_W), gmap(cxb)),
        pl.BlockSpec((1, M_STATE), gmap(cbb)),
        pl.BlockSpec((1, M_STATE), gmap(ccb)),
        pl.BlockSpec((1, tl, GROUP_W), lambda i, g, c: (i, c, zb + g)),
        pl.BlockSpec((1, tl, GROUP_W), lambda i, g, c: (i, c, xb + g)),
        pl.BlockSpec((1, tl, M_STATE), lambda i, g, c: (i, c, bb + g)),
        pl.BlockSpec((1, tl, M_STATE), lambda i, g, c: (i, c, cb + g)),
        pl.BlockSpec((1, tl, heads), lambda i, g, c: (i, c, 0)),
        pl.BlockSpec((1, 1, n_rows, LANES), lambda i, g, c: (i, g, c, 0)),
    ]
    args = [a_log.reshape(1, heads), alr, dsk, mnw, conv_w, conv_w, conv_w, conv_b2, conv_b2, conv_b2,
            proj, proj, proj, proj, dt.reshape(b, t, heads), dtr]
    if has_state:
        halo = M_CONV - 1
        in_specs += [
            pl.BlockSpec((1, halo, GROUP_W), lambda i, g, c: (i, 0, cxb + g)),
            pl.BlockSpec((1, halo, M_STATE), lambda i, g, c: (i, 0, cbb + g)),
            pl.BlockSpec((1, halo, M_STATE), lambda i, g, c: (i, 0, ccb + g)),
            pl.BlockSpec((1, M_HPG, M_HEAD_DIM, M_STATE), lambda i, g, c: (i, g, 0, 0)),
        ]
        args += [conv_state, conv_state, conv_state, ssm_state]
    return pl.pallas_call(
        functools.partial(_ssd_kernel, tl=tl, has_state=has_state),
        grid=(b, groups, nc),
        in_specs=in_specs,
        out_specs=[pl.BlockSpec((1, tl, GROUP_W), lambda i, g, c: (i, c, g)),
                   pl.BlockSpec((1, M_HPG, M_HEAD_DIM, M_STATE), lambda i, g, c: (i, g, 0, 0))],
        out_shape=[jax.ShapeDtypeStruct((b, t, d_inner), BF16),
                   jax.ShapeDtypeStruct((b, heads, M_HEAD_DIM, M_STATE), F32)],
        scratch_shapes=[pltpu.VMEM((tl + 8, GROUP_W), F32),
                        pltpu.VMEM((tl + 8, M_STATE), F32),
                        pltpu.VMEM((tl + 8, M_STATE), F32),
                        pltpu.VMEM((M_STATE, GROUP_W), F32)],
        compiler_params=_params("arbitrary", "arbitrary", "arbitrary"),
        name="ssd",
    )(*args)


def _attn_kernel(*refs, tq, tk, rc, n_qt, past, tkp, pos0, lam_init):
    lq1_ref, lk1_ref, lq2_ref, lk2_ref, nw_ref, q_ref, k_ref, v_ref, za_ref = refs[:9]
    if past:
        kp_ref, vp_ref = refs[9:11]
        rest = refs[11:]
    else:
        rest = refs[9:]
    o_ref, qz_scr, s_scr, p_scr, m_scr, l_scr, acc_scr = rest
    qi = pl.program_id(2)
    rows = A_REP * tq

    lane = lax.broadcasted_iota(jnp.int32, (tq, KV_W), 1)
    zero = jnp.zeros((), BF16)
    for r in range(A_REP):
        q = q_ref[0, :, r * KV_W:(r + 1) * KV_W]
        qz_scr[0, r * tq:(r + 1) * tq, :] = jnp.where(lane < A_HEAD_DIM, q, zero)
        qz_scr[1, r * tq:(r + 1) * tq, :] = jnp.where(lane >= A_HEAD_DIM, q, zero)
    m_scr[...] = jnp.full(m_scr.shape, -jnp.inf, F32)
    l_scr[...] = jnp.zeros(l_scr.shape, F32)
    acc_scr[...] = jnp.zeros(acc_scr.shape, F32)

    def tile(k, v, nkeys, mask_fn):
        reps = nkeys // LANES
        for comp in range(2):
            s_scr[comp, :, :nkeys] = _dot_nt(qz_scr[comp], k)
        for comp in range(2):
            for i in range(rows // rc):
                rs = slice(i * rc, (i + 1) * rc)
                s = s_scr[comp, rs, :nkeys]
                if mask_fn is not None:
                    s = jnp.where(mask_fn(i), s, -jnp.inf)
                m_old = m_scr[comp, rs, :]
                m_new = jnp.maximum(m_old, jnp.max(s, axis=-1, keepdims=True))
                p = jnp.exp2(s - jnp.concatenate([m_new] * reps, axis=1))
                alpha = jnp.exp2(m_old - m_new)
                psum = p[:, :LANES]
                for u in range(1, reps):
                    psum = psum + p[:, u * LANES:(u + 1) * LANES]
                l_scr[comp, rs, :] = alpha * l_scr[comp, rs, :] + psum
                acc_scr[comp, rs, :] = alpha * acc_scr[comp, rs, :]
                m_scr[comp, rs, :] = m_new
                p_scr[comp, rs, :nkeys] = p.astype(BF16)
            acc_scr[comp] = acc_scr[comp] + _dot(p_scr[comp, :, :nkeys], v)

    def diag_mask(i, nkeys, n_valid, k_pos0):
        t0 = (i * rc) % tq
        qpos = pos0 + qi * tq + t0 + lax.broadcasted_iota(jnp.int32, (rc, nkeys), 0)
        kidx = lax.broadcasted_iota(jnp.int32, (rc, nkeys), 1)
        ok = ((k_pos0 + kidx) // CHUNK) <= (qpos // CHUNK)
        if n_valid < nkeys:
            ok = ok & (kidx < n_valid)
        return ok

    if past:
        def pbody(j, carry):
            sl = pl.ds(pl.multiple_of(j * tkp, tkp), tkp)
            tile(kp_ref[0, sl, :].astype(BF16), vp_ref[0, sl, :].astype(BF16), tkp, None)
            return carry

        lax.fori_loop(0, past // tkp, pbody, 0)

    def nbody(j, carry):
        sl = pl.ds(pl.multiple_of(j * tk, tk), tk)
        tile(k_ref[0, sl, :], v_ref[0, sl, :], tk, None)
        return carry

    if n_qt > 1:
        lax.fori_loop(0, (qi * tq) // tk, nbody, 0)
    tkd = max(tk, LANES)
    for d in range(tq // tk):
        k0 = qi * tq + d * tk
        sl = pl.ds(pl.multiple_of(k0, tk), tk)
        k, v = k_ref[0, sl, :], v_ref[0, sl, :]
        if tkd > tk:
            pad = jnp.zeros((tkd - tk, KV_W), BF16)
            k, v = jnp.concatenate([k, pad], axis=0), jnp.concatenate([v, pad], axis=0)
        tile(k, v, tkd, functools.partial(diag_mask, nkeys=tkd, n_valid=tk, k_pos0=pos0 + k0))

    lam = (jnp.exp(jnp.sum(lq1_ref[...] * lk1_ref[...], axis=-1, keepdims=True))
           - jnp.exp(jnp.sum(lq2_ref[...] * lk2_ref[...], axis=-1, keepdims=True)) + lam_init)
    l0 = jnp.sum(l_scr[0], axis=-1, keepdims=True)
    l1 = jnp.sum(l_scr[1], axis=-1, keepdims=True)
    o = acc_scr[0] / l0 - lam * (acc_scr[1] / l1)
    ms = jnp.mean(o * o, axis=-1, keepdims=True)
    on = o * lax.rsqrt(ms + EPS) * nw_ref[...] * (1.0 - lam_init)
    for r in range(A_REP):
        sl = slice(r * KV_W, (r + 1) * KV_W)
        o_ref[0, :, sl] = (on[r * tq:(r + 1) * tq, :] * _silu(za_ref[0, :, sl])).astype(o_ref.dtype)


def _attention(qr, kb, vb, proj, za_off, k_past, v_past, lam_vecs, norm_w, lam_init, pos0):
    b, t, d = qr.shape
    kvh = d // QH_W
    past = 0 if k_past is None else k_past.shape[1]
    tq = _pick(t, 512, CHUNK)
    tk = tq
    tkp = _pick(past, 512, LANES) if past else 0
    rc = min(64, tq)
    zb = za_off // QH_W
    rows = A_REP * tq
    smax = max(tk, LANES, tkp)
    vec = pl.BlockSpec((1, A_HEAD_DIM), lambda i, h, j: (0, 0))
    in_specs = [vec, vec, vec, vec,
                pl.BlockSpec((1, KV_W), lambda i, h, j: (0, 0)),
                pl.BlockSpec((1, tq, QH_W), lambda i, h, j: (i, j, h)),
                pl.BlockSpec((1, t, KV_W), lambda i, h, j: (i, 0, h)),
                pl.BlockSpec((1, t, KV_W), lambda i, h, j: (i, 0, h)),
                pl.BlockSpec((1, tq, QH_W), lambda i, h, j: (i, j, zb + h))]
    args = [v.reshape(1, A_HEAD_DIM) for v in lam_vecs] + [norm_w.reshape(1, KV_W), qr, kb, vb, proj]
    if past:
        in_specs += [pl.BlockSpec((1, past, KV_W), lambda i, h, j: (i, 0, h)),
                     pl.BlockSpec((1, past, KV_W), lambda i, h, j: (i, 0, h))]
        args += [k_past, v_past]
    return pl.pallas_call(
        functools.partial(_attn_kernel, tq=tq, tk=tk, rc=rc, n_qt=t // tq, past=past, tkp=tkp, pos0=pos0,
                          lam_init=lam_init),
        grid=(b, kvh, t // tq),
        in_specs=in_specs,
        out_specs=pl.BlockSpec((1, tq, QH_W), lambda i, h, j: (i, j, h)),
        out_shape=jax.ShapeDtypeStruct((b, t, d), BF16),
        scratch_shapes=[pltpu.VMEM((2, rows, KV_W), BF16),
                        pltpu.VMEM((2, rows, smax), F32),
                        pltpu.VMEM((2, rows, smax), BF16),
                        pltpu.VMEM((2, rows, LANES), F32),
                        pltpu.VMEM((2, rows, LANES), F32),
                        pltpu.VMEM((2, rows, KV_W), F32)],
        compiler_params=_params("arbitrary", "arbitrary", "arbitrary"),
        name="diff_attn",
    )(*args)


def _merge_kernel(ym_ref, ya_ref, wm_ref, wa_ref, gm_ref, ga_ref, o_ref):
    pm = _dot(ym_ref[...], wm_ref[...])
    pa = _dot(ya_ref[...], wa_ref[...])
    o_ref[...] = (jax.nn.sigmoid(gm_ref[...]) * pm + jax.nn.sigmoid(ga_ref[...]) * pa).astype(o_ref.dtype)


def _merge(y_m, y_a, w_pm, w_pa, proj2d, gm_off, ga_off):
    m, km = y_m.shape
    ka = y_a.shape[1]
    d = w_pm.shape[1]
    tm = _pick(m, 256)
    tn = _pick(d, 512, LANES)
    gmb, gab = gm_off // tn, ga_off // tn
    return pl.pallas_call(
        _merge_kernel,
        grid=(d // tn, m // tm),
        in_specs=[pl.BlockSpec((tm, km), lambda j, i: (i, 0)),
                  pl.BlockSpec((tm, ka), lambda j, i: (i, 0)),
                  pl.BlockSpec((km, tn), lambda j, i: (0, j)),
                  pl.BlockSpec((ka, tn), lambda j, i: (0, j)),
                  pl.BlockSpec((tm, tn), lambda j, i: (i, gmb + j)),
                  pl.BlockSpec((tm, tn), lambda j, i: (i, gab + j))],
        out_specs=pl.BlockSpec((tm, tn), lambda j, i: (i, j)),
        out_shape=jax.ShapeDtypeStruct((m, d), BF16),
        compiler_params=_params("arbitrary", "arbitrary"),
        name="merge",
    )(y_m, y_a, w_pm, w_pa, proj2d, proj2d)


def _out_kernel(mg_ref, w_ref, x_ref, gate_ref, fw_ref, o_ref, *, tn):
    j = pl.program_id(2)
    cols = pl.ds(pl.multiple_of(j * tn, tn), tn)
    o_ref[0, :, cols] = x_ref[0] + gate_ref[0] * _dot(mg_ref[0], w_ref[...])

    @pl.when(j == pl.num_programs(2) - 1)
    def _():
        r = o_ref[0]
        ms = jnp.mean(r * r, axis=-1, keepdims=True)
        o_ref[0] = r * lax.rsqrt(ms + EPS) * fw_ref[...]


def _out_proj(merged, w_out, x, gate, final_w):
    shape = x.shape
    b, t, d = shape
    gate = gate.reshape(b, 1, d)
    per_row_gate = t < 512 and b > 1
    if per_row_gate:
        gate = jnp.broadcast_to(gate, (b, t, d)).reshape(1, b * t, d)
        x = x.reshape(1, b * t, d)
        b, t = 1, b * t
    tm = _pick(t, 512)
    tn = _pick(d, 512, LANES)
    gate_spec =(pl.BlockSpec((1, tm, tn), lambda i, r, j: (i, r, j)) if per_row_gate else
                 pl.BlockSpec((1, 1, tn), lambda i, r, j: (i, 0, j)))
    out = pl.pallas_call(
        functools.partial(_out_kernel, tn=tn),
        grid=(b, t // tm, d // tn),
        in_specs=[pl.BlockSpec((1, tm, d), lambda i, r, j: (i, r, 0)),
                  pl.BlockSpec((d, tn), lambda i, r, j: (0, j)),
                  pl.BlockSpec((1, tm, tn), lambda i, r, j: (i, r, j)),
                  gate_spec,
                  pl.BlockSpec((1, d), lambda i, r, j: (0, 0))],
        out_specs=pl.BlockSpec((1, tm, d), lambda i, r, j: (i, r, 0)),
        out_shape=jax.ShapeDtypeStruct((b, t, d), F32),
        compiler_params=_params("arbitrary", "arbitrary", "arbitrary"),
        name="out_proj",
    )(merged.reshape(b, t, d), w_out, x, gate, final_w.reshape(1, d))
    return out.reshape(shape)


def _layer_path(x, mod, k_past, v_past, conv_state, ssm_state, pos0, w, lam_init, final_w):
    b, t, d = x.shape
    d_inner = 2 * d
    groups = d_inner // GROUP_W
    bcw = groups * M_STATE
    conv_dim = d_inner + 2 * bcw
    kvw = (d // QH_W) * KV_W
    z_off, xbc_off = 0, d_inner
    k_off, v_off, za_off, gm_off, ga_off = d, d + kvw, d + 2 * kvw, 2 * d + 2 * kvw, 3 * d + 2 * kvw
    assert k_off % kvw == 0 and za_off % QH_W == 0, "consumer column blocks must be block-aligned"
    shift, scale, gate = mod[:, :d], mod[:, d:2 * d], mod[:, 2 * d:]

    h = _prenorm(x, w['norm_w'], scale, shift).reshape(b * t, d)
    proj_m = _matmul_f32w(h, w['w_in'], w['layer'], 0, xbc_off + conv_dim, "in_proj_m").reshape(b, t, -1)
    proj_a = _matmul_f32w(h, w['w_in'], w['layer'], w['a_col0'], ga_off + d, "in_proj_a").reshape(b, t, -1)
    dt = _dt_proj(h, w['w_in'], w['layer'], w['dt_col0'], w['dt_bias']).reshape(b, t, -1)

    pos = pos0 + jnp.arange(t, dtype=jnp.int32)
    qr, k_new, v_new, kb, vb = _rope_kv(proj_a, pos, d, kvw, k_off // kvw, v_off // kvw)

    y_m, ssm_new = _ssd(proj_m, dt, conv_state, ssm_state, w['conv_w'], w['conv_b'], w['a_log'], w['d_skip'],
                        w['mamba_norm_w'], z_off, xbc_off, d_inner)
    halo = M_CONV - 1
    conv_new = proj_m[:, t - halo:, xbc_off:xbc_off + conv_dim]

    y_a = _attention(qr, kb, vb, proj_a, za_off, k_past, v_past,
                     (w['lam_q1'], w['lam_k1'], w['lam_q2'], w['lam_k2']), w['attn_norm_w'], lam_init, pos0)

    merged = _merge(y_m.reshape(b * t, d_inner), y_a.reshape(b * t, d), w['w_proj_m'], w['w_proj_a'],
                    proj_a.reshape(b * t, -1), gm_off, ga_off)
    y = _out_proj(merged, w['w_out'], x, gate, final_w)
    kvh = d // QH_W
    return (y, k_new.reshape(b, t, kvh, 2, A_HEAD_DIM), v_new.reshape(b, t, kvh, KV_W), conv_new, ssm_new)


def kernel(x_prompt, x_sample, cache_k, cache_v, state_conv, state_ssm, c_prompt, c_sample,
           w_ada, b_ada, norm_w, w_in, conv_w, conv_b, dt_bias, a_log, d_skip, mamba_norm_w,
           lam_q1, lam_k1, lam_q2, lam_k2, attn_norm_w, w_proj_m, w_proj_a, w_out, final_norm_w):
    depth = w_in.shape[0]
    assert depth == 1, "the final norm is fused into the single layer's output projection"
    bp, d = c_prompt.shape
    bs = c_sample.shape[0]
    past = cache_k.shape[2]
    d_inner = 2 * d
    groups = d_inner // GROUP_W
    heads = groups * M_HPG
    conv_dim = d_inner + 2 * groups * M_STATE
    kvw = (d // QH_W) * KV_W
    sizes = (d_inner, conv_dim, heads, d, kvw, kvw, d, d, d)
    offs = [0]
    for s in sizes:
        offs.append(offs[-1] + s)

    i = 0
    wi = w_in[i]
    w = {
        'w_in': w_in, 'layer': i, 'dt_col0': offs[2], 'a_col0': offs[3],
        'norm_w': norm_w[i], 'conv_w': conv_w[i], 'conv_b': conv_b[i], 'dt_bias': dt_bias[i], 'a_log': a_log[i],
        'd_skip': d_skip[i], 'mamba_norm_w': mamba_norm_w[i], 'lam_q1': lam_q1[i], 'lam_k1': lam_k1[i],
        'lam_q2': lam_q2[i], 'lam_k2': lam_k2[i], 'attn_norm_w': attn_norm_w[i],
        'w_proj_m': w_proj_m[i].astype(BF16), 'w_proj_a': w_proj_a[i].astype(BF16), 'w_out': w_out[i].astype(BF16),
    }
    lam_init = 0.8 - 0.6 * math.exp(-0.3 * i)
    pad_rows = -(bp + bs) % 16
    c_all = jnp.concatenate([c_prompt, c_sample, jnp.zeros((pad_rows, d), F32)], axis=0)
    mod = _ada_mod(c_all, w_ada[i], b_ada[i])

    yp, kp, vp, cp, sp = _layer_path(x_prompt, mod[:bp], None, None, None, None, 0, w, lam_init, final_norm_w)
    ck = cache_k[i].reshape(bs, past, kvw)
    cv = cache_v[i].reshape(bs, past, kvw)
    ys, kq, vq, cq, sq = _layer_path(x_sample, mod[bp:bp + bs], ck, cv, state_conv[i], state_ssm[i], past, w, lam_init,
                                     final_norm_w)
    st = lambda a: a[None]
    return (yp, ys, st(kp), st(vp), st(cp), st(sp), st(kq), st(vq), st(cq), st(sq))
```

```python
import functools
import math

import jax
import jax.numpy as jnp
from jax import lax
from jax.experimental import pallas as pl
from jax.experimental.pallas import tpu as pltpu

F32 = jnp.float32
BF16 = jnp.bfloat16

CHUNK = 64
EPS = 1e-6
M_HEAD_DIM = 64
M_HPG = 16
M_STATE = 128
M_CONV = 4
GROUP_W = M_HPG * M_HEAD_DIM
A_HEAD_DIM = 64
A_REP = 4
KV_W = 2 * A_HEAD_DIM
QH_W = A_REP * KV_W
ROPE_DIM = 16
ROPE_THETA = 500000.0
LANES = 128
VMEM_LIMIT_BYTES = 56 * 1024 * 1024


def _pick(n, target, mult=8):
    if n <= target:
        return n
    for t in range(target, 0, -1):
        if n % t == 0 and t % mult == 0:
            return t
    return n


def _params(*sem):
    return pltpu.CompilerParams(dimension_semantics=sem, vmem_limit_bytes=VMEM_LIMIT_BYTES)


def _silu(x):
    return x * jax.nn.sigmoid(x)


def _split_hi_lo(x):
    hi = x.astype(BF16)
    lo = (x - hi.astype(F32)).astype(BF16)
    return hi, lo


def _dot(a, b):
    return jnp.dot(a, b, preferred_element_type=F32)


def _dot_nt(a, b):
    return lax.dot_general(a, b, (((1,), (1,)), ((), ())), preferred_element_type=F32)


def _dot_tn(a, b):
    return lax.dot_general(a, b, (((0,), (0,)), ((), ())), preferred_element_type=F32)


def _ada_kernel(c_ref, w_ref, b_ref, o_ref):
    a = _silu(c_ref[...]).astype(BF16)
    o_ref[...] = _dot(a, w_ref[...].astype(BF16)) + b_ref[...]


def _ada_mod(c, w_ada, b_ada):
    m, d = c.shape
    n = w_ada.shape[1]
    tn = _pick(n, 512, LANES)
    return pl.pallas_call(
        _ada_kernel,
        grid=(n // tn,),
        in_specs=[pl.BlockSpec((m, d), lambda j: (0, 0)),
                  pl.BlockSpec((d, tn), lambda j: (0, j)),
                  pl.BlockSpec((1, tn), lambda j: (0, j))],
        out_specs=pl.BlockSpec((m, tn), lambda j: (0, j)),
        out_shape=jax.ShapeDtypeStruct((m, n), F32),
        compiler_params=_params("arbitrary"),
        name="ada_mod",
    )(c, w_ada, b_ada.reshape(1, n))


def _prenorm_kernel(x_ref, nw_ref, sc_ref, sh_ref, o_ref):
    x = x_ref[0]
    ms = jnp.mean(x * x, axis=-1, keepdims=True)
    y = x * lax.rsqrt(ms + EPS) * nw_ref[...]
    o_ref[0] = (y * (1.0 + sc_ref[0]) + sh_ref[0]).astype(o_ref.dtype)


def _prenorm(x, norm_w, scale, shift):
    b, t, d = x.shape
    tr = _pick(t, 256)
    return pl.pallas_call(
        _prenorm_kernel,
        grid=(b, t // tr),
        in_specs=[pl.BlockSpec((1, tr, d), lambda i, j: (i, j, 0)),
                  pl.BlockSpec((1, d), lambda i, j: (0, 0)),
                  pl.BlockSpec((1, 1, d), lambda i, j: (i, 0, 0)),
                  pl.BlockSpec((1, 1, d), lambda i, j: (i, 0, 0))],
        out_specs=pl.BlockSpec((1, tr, d), lambda i, j: (i, j, 0)),
        out_shape=jax.ShapeDtypeStruct((b, t, d), BF16),
        compiler_params=_params("arbitrary", "arbitrary"),
        name="prenorm",
    )(x, norm_w.reshape(1, d), scale.reshape(b, 1, d), shift.reshape(b, 1, d))


def _mm_kernel(a_ref, w_ref, o_ref):
    o_ref[...] = _dot(a_ref[...], w_ref[...]).astype(o_ref.dtype)


def _mm_castw_kernel(a_ref, w_ref, o_ref):
    o_ref[...] = _dot(a_ref[...], w_ref[0].astype(BF16))


def _matmul_f32w(a, w3, layer, col0, n, name):
    m, k = a.shape
    tm = _pick(m, 1024)
    tn = _pick(n, 512, LANES)
    if col0 % LANES:
        w3, col0 = w3[:, :, col0:col0 + n], 0
    return pl.pallas_call(
        _mm_castw_kernel,
        grid=(m // tm, n // tn),
        in_specs=[pl.BlockSpec((tm, k), lambda i, j: (i, 0)),
                  pl.BlockSpec((pl.Element(1), pl.Element(k), pl.Element(tn)),
                               lambda i, j: (layer, 0, pl.multiple_of(col0 + j * tn, LANES)))],
        out_specs=pl.BlockSpec((tm, tn), lambda i, j: (i, j)),
        out_shape=jax.ShapeDtypeStruct((m, n), F32),
        compiler_params=_params("arbitrary", "arbitrary"),
        name=name,
    )(a, w3)


def _matmul(a, w, name):
    m, k = a.shape
    n = w.shape[1]
    tm = _pick(m, 1024)
    tn = _pick(n, 1024, LANES)
    return pl.pallas_call(
        _mm_kernel,
        grid=(n // tn, m // tm),
        in_specs=[pl.BlockSpec((tm, k), lambda j, i: (i, 0)),
                  pl.BlockSpec((k, tn), lambda j, i: (0, j))],
        out_specs=pl.BlockSpec((tm, tn), lambda j, i: (i, j)),
        out_shape=jax.ShapeDtypeStruct((m, n), F32),
        compiler_params=_params("arbitrary", "arbitrary"),
        name=name,
    )(a, w)


def _dt_kernel(a_ref, w_ref, b_ref, o_ref, wb_scr):
    @pl.when(pl.program_id(0) == 0)
    def _():
        wb_scr[...] = w_ref[0].astype(BF16)

    x = _dot(a_ref[...], wb_scr[...]) + b_ref[...]
    o_ref[...] = jnp.maximum(x, 0.0) + jnp.log1p(jnp.exp(-jnp.abs(x)))


def _dt_proj(a, w3, layer, col0, dt_bias):
    m, k = a.shape
    n = dt_bias.shape[0]
    tm = _pick(m, 512)
    if col0 % LANES or n % LANES:
        w3, col0 = w3[:, :, col0:col0 + n], 0
    return pl.pallas_call(
        _dt_kernel,
        grid=(m // tm,),
        in_specs=[pl.BlockSpec((tm, k), lambda i: (i, 0)),
                  pl.BlockSpec((pl.Element(1), pl.Element(k), pl.Element(n)),
                               lambda i: (layer, 0, col0)),
                  pl.BlockSpec((1, n), lambda i: (0, 0))],
        out_specs=pl.BlockSpec((tm, n), lambda i: (i, 0)),
        out_shape=jax.ShapeDtypeStruct((m, n), F32),
        scratch_shapes=[pltpu.VMEM((k, n), BF16)],
        compiler_params=_params("arbitrary"),
        name="dt_proj",
    )(a, w3, dt_bias.reshape(1, n))


def _rope_slab(x, cos, sin_up, sin_dn):
    return x * cos + pltpu.roll(x, LANES - ROPE_DIM // 2, 1) * sin_up + pltpu.roll(x, ROPE_DIM // 2, 1) * sin_dn


def _rope_kernel(q_ref, k_ref, v_ref, cos_ref, sup_ref, sdn_ref, qo_ref, ko_ref, vo_ref, kb_ref, vb_ref, *, q_scale):
    cos, sup, sdn = cos_ref[...], sup_ref[...], sdn_ref[...]
    for s in range(q_ref.shape[2] // LANES):
        sl = slice(s * LANES, (s + 1) * LANES)
        qo_ref[0, :, sl] = (_rope_slab(q_ref[0, :, sl], cos, sup, sdn) * q_scale).astype(qo_ref.dtype)
    for s in range(k_ref.shape[2] // LANES):
        sl = slice(s * LANES, (s + 1) * LANES)
        k = _rope_slab(k_ref[0, :, sl], cos, sup, sdn)
        ko_ref[0, :, sl] = k
        kb_ref[0, :, sl] = k.astype(kb_ref.dtype)
    v = v_ref[...]
    vo_ref[...] = v
    vb_ref[...] = v.astype(vb_ref.dtype)


def _rope_tables(pos):
    half = ROPE_DIM // 2
    inv = ROPE_THETA ** (-jnp.arange(half, dtype=F32) / half)
    ang = pos.astype(F32)[:, None] * inv[None, :]
    cos, sin = jnp.cos(ang), jnp.sin(ang)
    t = pos.shape[0]
    ones = jnp.ones((t, A_HEAD_DIM - ROPE_DIM), F32)
    zeros = jnp.zeros((t, A_HEAD_DIM - ROPE_DIM), F32)
    zh = jnp.zeros((t, half), F32)
    cos64 = jnp.concatenate([cos, cos, ones], axis=1)
    sup64 = jnp.concatenate([-sin, zh, zeros], axis=1)
    sdn64 = jnp.concatenate([zh, sin, zeros], axis=1)
    rep = lambda a: jnp.concatenate([a, a], axis=1)
    return rep(cos64), rep(sup64), rep(sdn64)


def _rope_kv(proj, pos, d, kvw, k_blk, v_blk):
    b, t, _ = proj.shape
    tr = _pick(t, 256)
    cos, sup, sdn = _rope_tables(pos)
    tab = pl.BlockSpec((tr, LANES), lambda i, j: (j, 0))
    kv_out = pl.BlockSpec((1, tr, kvw), lambda i, j: (i, j, 0))
    return pl.pallas_call(
        functools.partial(_rope_kernel, q_scale=A_HEAD_DIM ** -0.5 * math.log2(math.e)),
        grid=(b, t // tr),
        in_specs=[pl.BlockSpec((1, tr, d), lambda i, j: (i, j, 0)),
                  pl.BlockSpec((1, tr, kvw), lambda i, j: (i, j, k_blk)),
                  pl.BlockSpec((1, tr, kvw), lambda i, j: (i, j, v_blk)),
                  tab, tab, tab],
        out_specs=[pl.BlockSpec((1, tr, d), lambda i, j: (i, j, 0))] + [kv_out] * 4,
        out_shape=[jax.ShapeDtypeStruct((b, t, d), BF16),
                   jax.ShapeDtypeStruct((b, t, kvw), F32),
                   jax.ShapeDtypeStruct((b, t, kvw), F32),
                   jax.ShapeDtypeStruct((b, t, kvw), BF16),
                   jax.ShapeDtypeStruct((b, t, kvw), BF16)],
        compiler_params=_params("arbitrary", "arbitrary"),
        name="rope_kv",
    )(proj, proj, proj, cos, sup, sdn)


def _ssd_kernel(*refs, tl, has_state):
    (alog_ref, alr_ref, dsk_ref, mnw_ref, cwx_ref, cwb_ref, cwc_ref, cbx_ref, cbb_ref, cbc_ref,
     z_ref, x_ref, bm_ref, cm_ref, dt_ref, dtr_ref) = refs[:16]
    if has_state:
        csx_ref, csb_ref, csc_ref, s0_ref = refs[16:20]
        rest = refs[20:]
    else:
        rest = refs[16:]
    y_ref, sout_ref, xbuf, bbuf, cbuf, st_scr = rest
    g = pl.program_id(1)
    c = pl.program_id(2)
    L = CHUNK
    nsub = tl // L
    halo = M_CONV - 1
    base = 8

    @pl.when(c == 0)
    def _():
        if has_state:
            xbuf[base - halo:base, :] = csx_ref[0]
            bbuf[base - halo:base, :] = csb_ref[0]
            cbuf[base - halo:base, :] = csc_ref[0]
            st_scr[...] = s0_ref[0].reshape(GROUP_W, M_STATE).T
        else:
            xbuf[0:base, :] = jnp.zeros((base, GROUP_W), F32)
            bbuf[0:base, :] = jnp.zeros((base, M_STATE), F32)
            cbuf[0:base, :] = jnp.zeros((base, M_STATE), F32)
            st_scr[...] = jnp.zeros_like(st_scr)

    xbuf[base:base + tl, :] = x_ref[0]
    bbuf[base:base + tl, :] = bm_ref[0]
    cbuf[base:base + tl, :] = cm_ref[0]

    r64 = lax.broadcasted_iota(jnp.int32, (L, L), 0)
    c64 = lax.broadcasted_iota(jnp.int32, (L, L), 1)
    tril = (c64 <= r64).astype(BF16)
    heads = dt_ref.shape[2]
    hrow = lax.broadcasted_iota(jnp.int32, (heads, GROUP_W), 0)
    hcol = lax.broadcasted_iota(jnp.int32, (heads, GROUP_W), 1)
    expand = (hrow == g * M_HPG + hcol // M_HEAD_DIM).astype(BF16)
    br = lax.broadcasted_iota(jnp.int32, (LANES, LANES), 0)
    bc = lax.broadcasted_iota(jnp.int32, (LANES, LANES), 1)
    same_half = (br // L) == (bc // L)
    triu2 = (same_half & ((br % L) <= (bc % L))).astype(BF16)
    pr = lax.broadcasted_iota(jnp.int32, (L, LANES), 0)
    pc = lax.broadcasted_iota(jnp.int32, (L, LANES), 1)
    causal2 = (pc % L) <= pr

    a_col = -jnp.exp(alog_ref[...])
    a_row = -jnp.exp(alr_ref[0])
    n_rows = dtr_ref.shape[2]
    da_r = dtr_ref[0, 0] * jnp.concatenate([a_row] * (n_rows // 8), axis=0)
    hi, lo = _split_hi_lo(da_r)
    acum_r_all = _dot(hi, triu2) + _dot(lo, triu2)

    def conv(buf, w_ref, b_ref, i):
        acc = b_ref[...] + w_ref[M_CONV - 1:M_CONV, :] * buf[base + i * L:base + (i + 1) * L, :]
        for k in range(M_CONV - 1):
            off = base - halo + k + i * L
            acc = acc + w_ref[k:k + 1, :] * buf[off:off + L, :]
        return _silu(acc)

    for i in range(nsub):
        rows = slice(i * L, (i + 1) * L)
        x = conv(xbuf, cwx_ref, cbx_ref, i)
        bm = conv(bbuf, cwb_ref, cbb_ref, i).astype(BF16)
        cm = conv(cbuf, cwc_ref, cbc_ref, i).astype(BF16)
        dt = dt_ref[0, rows, :]
        da = dt * a_col
        hi, lo = _split_hi_lo(da)
        acum = _dot(tril, hi) + _dot(tril, lo)
        hi, lo = _split_hi_lo(acum)
        acum_x = _dot(hi, expand) + _dot(lo, expand)
        hi, lo = _split_hi_lo(dt)
        dt_x = _dot(hi, expand) + _dot(lo, expand)
        xdt = x * dt_x
        xdt_b = xdt.astype(BF16)
        alast_x = acum_x[L - 1:L, :]
        cb2 = _dot_nt(cm, jnp.concatenate([bm, bm], axis=0))
        st_b = st_scr[...].astype(BF16)
        y_off = _dot(cm, st_b) * jnp.exp(acum_x)
        y_parts = []
        for jj in range(GROUP_W // LANES):
            ls = slice(jj * LANES, (jj + 1) * LANES)
            seg = acum_x[:, ls] - acum_r_all[i * 8 + jj:i * 8 + jj + 1, :]
            m2 = (jnp.where(causal2, jnp.exp(seg), 0.0) * cb2).astype(BF16)
            x2 = xdt_b[:, ls]
            rhs = jnp.where(same_half, jnp.concatenate([x2, x2], axis=0), jnp.zeros((), BF16))
            y_parts.append(_dot(m2, rhs))
        y = jnp.concatenate(y_parts, axis=1) + y_off + dsk_ref[0] * x
        y = y * _silu(z_ref[0, rows, :])
        ms = jnp.mean(y * y, axis=-1, keepdims=True)
        y_ref[0, rows, :] = (y * lax.rsqrt(ms + EPS) * mnw_ref[0]).astype(y_ref.dtype)
        xw = (xdt * jnp.exp(alast_x - acum_x)).astype(BF16)
        st_scr[...] = st_scr[...] * jnp.exp(alast_x) + _dot_tn(bm, xw)

    tx = xbuf[base + tl - halo:base + tl, :]
    tb = bbuf[base + tl - halo:base + tl, :]
    tc = cbuf[base + tl - halo:base + tl, :]
    xbuf[base - halo:base, :] = tx
    bbuf[base - halo:base, :] = tb
    cbuf[base - halo:base, :] = tc

    @pl.when(c == pl.num_programs(2) - 1)
    def _():
        sout_ref[0] = st_scr[...].T.reshape(M_HPG, M_HEAD_DIM, M_STATE)


def _ssd(proj, dt, conv_state, ssm_state, conv_w, conv_b, a_log, d_skip, mnorm_w, z_off, xbc_off, d_inner):
    b, t, _ = proj.shape
    groups = d_inner // GROUP_W
    heads = groups * M_HPG
    bcw = groups * M_STATE
    has_state = conv_state is not None
    tl = _pick(t, 256, CHUNK)
    nsub = tl // CHUNK
    nc = t // tl
    dtr = dt.reshape(b, t // CHUNK, CHUNK, groups, M_HPG // 2, 2).transpose(0, 3, 1, 4, 5, 2)
    dtr = dtr.reshape(b, groups, (t // CHUNK) * 8, LANES)
    n_rows = nsub * 8
    if n_rows < 16:
        dtr = jnp.concatenate([dtr, jnp.zeros_like(dtr)], axis=2)
        n_rows = 16
    alr = jnp.repeat(a_log.reshape(groups, M_HPG // 2, 2), CHUNK, axis=2)
    dsk = jnp.repeat(d_skip.reshape(groups, 1, M_HPG), M_HEAD_DIM, axis=2)
    mnw = mnorm_w.reshape(groups, 1, GROUP_W)
    xb, bb, cb = xbc_off // GROUP_W, (xbc_off + d_inner) // M_STATE, (xbc_off + d_inner + bcw) // M_STATE
    cxb, cbb, ccb = 0, d_inner // M_STATE, (d_inner + bcw) // M_STATE
    zb = z_off // GROUP_W
    conv_b2 = conv_b.reshape(1, -1)
    gmap = lambda blk: (lambda i, g, c: (0, blk + g))
    in_specs = [
        pl.BlockSpec((1, heads), lambda i, g, c: (0, 0)),
        pl.BlockSpec((1, 8, LANES), lambda i, g, c: (g, 0, 0)),
        pl.BlockSpec((1, 1, GROUP_W), lambda i, g, c: (g, 0, 0)),
        pl.BlockSpec((1, 1, GROUP_W), lambda i, g, c: (g, 0, 0)),
        pl.BlockSpec((M_CONV, GROUP_W), gmap(cxb)),
        pl.BlockSpec((M_CONV, M_STATE), gmap(cbb)),
        pl.BlockSpec((M_CONV, M_STATE), gmap(ccb)),
        pl.BlockSpec((1, GROUP_W), gmap(cxb)),
        pl.BlockSpec((1, M_STATE), gmap(cbb)),
        pl.BlockSpec((1, M_STATE), gmap(ccb)),
        pl.BlockSpec((1, tl, GROUP_W), lambda i, g, c: (i, c, zb + g)),
        pl.BlockSpec((1, tl, GROUP_W), lambda i, g, c: (i, c, xb + g)),
        pl.BlockSpec((1, tl, M_STATE), lambda i, g, c: (i, c, bb + g)),
        pl.BlockSpec((1, tl, M_STATE), lambda i, g, c: (i, c, cb + g)),
        pl.BlockSpec((1, tl, heads), lambda i, g, c: (i, c, 0)),
        pl.BlockSpec((1, 1, n_rows, LANES), lambda i, g, c: (i, g, c, 0)),
    ]
    args = [a_log.reshape(1, heads), alr, dsk, mnw, conv_w, conv_w, conv_w, conv_b2, conv_b2, conv_b2,
            proj, proj, proj, proj, dt.reshape(b, t, heads), dtr]
    if has_state:
        halo = M_CONV - 1
        in_specs += [
            pl.BlockSpec((1, halo, GROUP_W), lambda i, g, c: (i, 0, cxb + g)),
            pl.BlockSpec((1, halo, M_STATE), lambda i, g, c: (i, 0, cbb + g)),
            pl.BlockSpec((1, halo, M_STATE), lambda i, g, c: (i, 0, ccb + g)),
            pl.BlockSpec((1, M_HPG, M_HEAD_DIM, M_STATE), lambda i, g, c: (i, g, 0, 0)),
        ]
        args += [conv_state, conv_state, conv_state, ssm_state]
    return pl.pallas_call(
        functools.partial(_ssd_kernel, tl=tl, has_state=has_state),
        grid=(b, groups, nc),
        in_specs=in_specs,
        out_specs=[pl.BlockSpec((1, tl, GROUP_W), lambda i, g, c: (i, c, g)),
                   pl.BlockSpec((1, M_HPG, M_HEAD_DIM, M_STATE), lambda i, g, c: (i, g, 0, 0))],
        out_shape=[jax.ShapeDtypeStruct((b, t, d_inner), BF16),
                   jax.ShapeDtypeStruct((b, heads, M_HEAD_DIM, M_STATE), F32)],
        scratch_shapes=[pltpu.VMEM((tl + 8, GROUP_W), F32),
                        pltpu.VMEM((tl + 8, M_STATE), F32),
                        pltpu.VMEM((tl + 8, M_STATE), F32),
                        pltpu.VMEM((M_STATE, GROUP_W), F32)],
        compiler_params=_params("arbitrary", "arbitrary", "arbitrary"),
        name="ssd",
    )(*args)


def _attn_kernel(*refs, tq, tk, rc, n_qt, past, tkp, pos0, lam_init):
    lq1_ref, lk1_ref, lq2_ref, lk2_ref, nw_ref, q_ref, k_ref, v_ref, za_ref = refs[:9]
    if past:
        kp_ref, vp_ref = refs[9:11]
        rest = refs[11:]
    else:
        rest = refs[9:]
    o_ref, qz_scr, s_scr, p_scr, m_scr, l_scr, acc_scr = rest
    qi = pl.program_id(2)
    rows = A_REP * tq

    lane = lax.broadcasted_iota(jnp.int32, (tq, KV_W), 1)
    zero = jnp.zeros((), BF16)
    for r in range(A_REP):
        q = q_ref[0, :, r * KV_W:(r + 1) * KV_W]
        qz_scr[0, r * tq:(r + 1) * tq, :] = jnp.where(lane < A_HEAD_DIM, q, zero)
        qz_scr[1, r * tq:(r + 1) * tq, :] = jnp.where(lane >= A_HEAD_DIM, q, zero)
    m_scr[...] = jnp.full(m_scr.shape, -jnp.inf, F32)
    l_scr[...] = jnp.zeros(l_scr.shape, F32)
    acc_scr[...] = jnp.zeros(acc_scr.shape, F32)

    def tile(k, v, nkeys, mask_fn):
        reps = nkeys // LANES
        for comp in range(2):
            s_scr[comp, :, :nkeys] = _dot_nt(qz_scr[comp], k)
            for i in range(rows // rc):
                rs = slice(i * rc, (i + 1) * rc)
                s = s_scr[comp, rs, :nkeys]
                if mask_fn is not None:
                    s = jnp.where(mask_fn(i), s, -jnp.inf)
                m_old = m_scr[comp, rs, :]
                m_new = jnp.maximum(m_old, jnp.max(s, axis=-1, keepdims=True))
                p = jnp.exp2(s - jnp.concatenate([m_new] * reps, axis=1))
                alpha = jnp.exp2(m_old - m_new)
                psum = p[:, :LANES]
                for u in range(1, reps):
                    psum = psum + p[:, u * LANES:(u + 1) * LANES]
                l_scr[comp, rs, :] = alpha * l_scr[comp, rs, :] + psum
                acc_scr[comp, rs, :] = alpha * acc_scr[comp, rs, :]
                m_scr[comp, rs, :] = m_new
                p_scr[comp, rs, :nkeys] = p.astype(BF16)
            acc_scr[comp] = acc_scr[comp] + _dot(p_scr[comp, :, :nkeys], v)

    def diag_mask(i, nkeys, n_valid, k_pos0):
        t0 = (i * rc) % tq
        qpos = pos0 + qi * tq + t0 + lax.broadcasted_iota(jnp.int32, (rc, nkeys), 0)
        kidx = lax.broadcasted_iota(jnp.int32, (rc, nkeys), 1)
        ok = ((k_pos0 + kidx) // CHUNK) <= (qpos // CHUNK)
        if n_valid < nkeys:
            ok = ok & (kidx < n_valid)
        return ok

    if past:
        def pbody(j, carry):
            sl = pl.ds(pl.multiple_of(j * tkp, tkp), tkp)
            tile(kp_ref[0, sl, :].astype(BF16), vp_ref[0, sl, :].astype(BF16), tkp, None)
            return carry

        lax.fori_loop(0, past // tkp, pbody, 0)

    def nbody(j, carry):
        sl = pl.ds(pl.multiple_of(j * tk, tk), tk)
        tile(k_ref[0, sl, :], v_ref[0, sl, :], tk, None)
        return carry

    if n_qt > 1:
        lax.fori_loop(0, (qi * tq) // tk, nbody, 0)
    tkd = max(tk, LANES)
    for d in range(tq // tk):
        k0 = qi * tq + d * tk
        sl = pl.ds(pl.multiple_of(k0, tk), tk)
        k, v = k_ref[0, sl, :], v_ref[0, sl, :]
        if tkd > tk:
            pad = jnp.zeros((tkd - tk, KV_W), BF16)
            k, v = jnp.concatenate([k, pad], axis=0), jnp.concatenate([v, pad], axis=0)
        tile(k, v, tkd, functools.partial(diag_mask, nkeys=tkd, n_valid=tk, k_pos0=pos0 + k0))

    lam = (jnp.exp(jnp.sum(lq1_ref[...] * lk1_ref[...], axis=-1, keepdims=True))
           - jnp.exp(jnp.sum(lq2_ref[...] * lk2_ref[...], axis=-1, keepdims=True)) + lam_init)
    l0 = jnp.sum(l_scr[0], axis=-1, keepdims=True)
    l1 = jnp.sum(l_scr[1], axis=-1, keepdims=True)
    o = acc_scr[0] / l0 - lam * (acc_scr[1] / l1)
    ms = jnp.mean(o * o, axis=-1, keepdims=True)
    on = o * lax.rsqrt(ms + EPS) * nw_ref[...] * (1.0 - lam_init)
    for r in range(A_REP):
        sl = slice(r * KV_W, (r + 1) * KV_W)
        o_ref[0, :, sl] = (on[r * tq:(r + 1) * tq, :] * _silu(za_ref[0, :, sl])).astype(o_ref.dtype)


def _attention(qr, kb, vb, proj, za_off, k_past, v_past, lam_vecs, norm_w, lam_init, pos0):
    b, t, d = qr.shape
    kvh = d // QH_W
    past = 0 if k_past is None else k_past.shape[1]
    tq = _pick(t, 512, CHUNK)
    tk = tq
    tkp = _pick(past, 2048, LANES) if past else 0
    rc = min(64, tq)
    zb = za_off // QH_W
    rows = A_REP * tq
    smax = max(tk, LANES, tkp)
    vec = pl.BlockSpec((1, A_HEAD_DIM), lambda i, h, j: (0, 0))
    in_specs = [vec, vec, vec, vec,
                pl.BlockSpec((1, KV_W), lambda i, h, j: (0, 0)),
                pl.BlockSpec((1, tq, QH_W), lambda i, h, j: (i, j, h)),
                pl.BlockSpec((1, t, KV_W), lambda i, h, j: (i, 0, h)),
                pl.BlockSpec((1, t, KV_W), lambda i, h, j: (i, 0, h)),
                pl.BlockSpec((1, tq, QH_W), lambda i, h, j: (i, j, zb + h))]
    args = [v.reshape(1, A_HEAD_DIM) for v in lam_vecs] + [norm_w.reshape(1, KV_W), qr, kb, vb, proj]
    if past:
        in_specs += [pl.BlockSpec((1, past, KV_W), lambda i, h, j: (i, 0, h)),
                     pl.BlockSpec((1, past, KV_W), lambda i, h, j: (i, 0, h))]
        args += [k_past, v_past]
    return pl.pallas_call(
        functools.partial(_attn_kernel, tq=tq, tk=tk, rc=rc, n_qt=t // tq, past=past, tkp=tkp, pos0=pos0,
                          lam_init=lam_init),
        grid=(b, kvh, t // tq),
        in_specs=in_specs,
        out_specs=pl.BlockSpec((1, tq, QH_W), lambda i, h, j: (i, j, h)),
        out_shape=jax.ShapeDtypeStruct((b, t, d), BF16),
        scratch_shapes=[pltpu.VMEM((2, rows, KV_W), BF16),
                        pltpu.VMEM((2, rows, smax), F32),
                        pltpu.VMEM((2, rows, smax), BF16),
                        pltpu.VMEM((2, rows, LANES), F32),
                        pltpu.VMEM((2, rows, LANES), F32),
                        pltpu.VMEM((2, rows, KV_W), F32)],
        compiler_params=_params("arbitrary", "arbitrary", "arbitrary"),
        name="diff_attn",
    )(*args)


def _merge_kernel(ym_ref, ya_ref, wm_ref, wa_ref, gm_ref, ga_ref, o_ref):
    pm = _dot(ym_ref[...], wm_ref[...])
    pa = _dot(ya_ref[...], wa_ref[...])
    o_ref[...] = (jax.nn.sigmoid(gm_ref[...]) * pm + jax.nn.sigmoid(ga_ref[...]) * pa).astype(o_ref.dtype)


def _merge(y_m, y_a, w_pm, w_pa, proj2d, gm_off, ga_off):
    m, km = y_m.shape
    ka = y_a.shape[1]
    d = w_pm.shape[1]
    tm = _pick(m, 256)
    tn = _pick(d, 512, LANES)
    gmb, gab = gm_off // tn, ga_off // tn
    return pl.pallas_call(
        _merge_kernel,
        grid=(d // tn, m // tm),
        in_specs=[pl.BlockSpec((tm, km), lambda j, i: (i, 0)),
                  pl.BlockSpec((tm, ka), lambda j, i: (i, 0)),
                  pl.BlockSpec((km, tn), lambda j, i: (0, j)),
                  pl.BlockSpec((ka, tn), lambda j, i: (0, j)),
                  pl.BlockSpec((tm, tn), lambda j, i: (i, gmb + j)),
                  pl.BlockSpec((tm, tn), lambda j, i: (i, gab + j))],
        out_specs=pl.BlockSpec((tm, tn), lambda j, i: (i, j)),
        out_shape=jax.ShapeDtypeStruct((m, d), BF16),
        compiler_params=_params("arbitrary", "arbitrary"),
        name="merge",
    )(y_m, y_a, w_pm, w_pa, proj2d, proj2d)


def _out_kernel(mg_ref, w_ref, x_ref, gate_ref, fw_ref, o_ref, *, tn):
    j = pl.program_id(2)
    cols = pl.ds(pl.multiple_of(j * tn, tn), tn)
    o_ref[0, :, cols] = x_ref[0] + gate_ref[0] * _dot(mg_ref[0], w_ref[...])

    @pl.when(j == pl.num_programs(2) - 1)
    def _():
        r = o_ref[0]
        ms = jnp.mean(r * r, axis=-1, keepdims=True)
        o_ref[0] = r * lax.rsqrt(ms + EPS) * fw_ref[...]


def _out_proj(merged, w_out, x, gate, final_w):
    shape = x.shape
    b, t, d = shape
    gate = gate.reshape(b, 1, d)
    per_row_gate = t < 512 and b > 1
    if per_row_gate:
        gate = jnp.broadcast_to(gate, (b, t, d)).reshape(1, b * t, d)
        x = x.reshape(1, b * t, d)
        b, t = 1, b * t
    tm = _pick(t, 512)
    tn = _pick(d, 512, LANES)
    gate_spec =(pl.BlockSpec((1, tm, tn), lambda i, r, j: (i, r, j)) if per_row_gate else
                 pl.BlockSpec((1, 1, tn), lambda i, r, j: (i, 0, j)))
    out = pl.pallas_call(
        functools.partial(_out_kernel, tn=tn),
        grid=(b, t // tm, d // tn),
        in_specs=[pl.BlockSpec((1, tm, d), lambda i, r, j: (i, r, 0)),
                  pl.BlockSpec((d, tn), lambda i, r, j: (0, j)),
                  pl.BlockSpec((1, tm, tn), lambda i, r, j: (i, r, j)),
                  gate_spec,
                  pl.BlockSpec((1, d), lambda i, r, j: (0, 0))],
        out_specs=pl.BlockSpec((1, tm, d), lambda i, r, j: (i, r, 0)),
        out_shape=jax.ShapeDtypeStruct((b, t, d), F32),
        compiler_params=_params("arbitrary", "arbitrary", "arbitrary"),
        name="out_proj",
    )(merged.reshape(b, t, d), w_out, x, gate, final_w.reshape(1, d))
    return out.reshape(shape)


def _layer_path(x, mod, k_past, v_past, conv_state, ssm_state, pos0, w, lam_init, final_w):
    b, t, d = x.shape
    d_inner = 2 * d
    groups = d_inner // GROUP_W
    bcw = groups * M_STATE
    conv_dim = d_inner + 2 * bcw
    kvw = (d // QH_W) * KV_W
    z_off, xbc_off = 0, d_inner
    k_off, v_off, za_off, gm_off, ga_off = d, d + kvw, d + 2 * kvw, 2 * d + 2 * kvw, 3 * d + 2 * kvw
    assert k_off % kvw == 0 and za_off % QH_W == 0, "consumer column blocks must be block-aligned"
    shift, scale, gate = mod[:, :d], mod[:, d:2 * d], mod[:, 2 * d:]

    h = _prenorm(x, w['norm_w'], scale, shift).reshape(b * t, d)
    proj_m = _matmul_f32w(h, w['w_in'], w['layer'], 0, xbc_off + conv_dim, "in_proj_m").reshape(b, t, -1)
    proj_a = _matmul_f32w(h, w['w_in'], w['layer'], w['a_col0'], ga_off + d, "in_proj_a").reshape(b, t, -1)
    dt = _dt_proj(h, w['w_in'], w['layer'], w['dt_col0'], w['dt_bias']).reshape(b, t, -1)

    pos = pos0 + jnp.arange(t, dtype=jnp.int32)
    qr, k_new, v_new, kb, vb = _rope_kv(proj_a, pos, d, kvw, k_off // kvw, v_off // kvw)

    y_m, ssm_new = _ssd(proj_m, dt, conv_state, ssm_state, w['conv_w'], w['conv_b'], w['a_log'], w['d_skip'],
                        w['mamba_norm_w'], z_off, xbc_off, d_inner)
    halo = M_CONV - 1
    conv_new = proj_m[:, t - halo:, xbc_off:xbc_off + conv_dim]

    y_a = _attention(qr, kb, vb, proj_a, za_off, k_past, v_past,
                     (w['lam_q1'], w['lam_k1'], w['lam_q2'], w['lam_k2']), w['attn_norm_w'], lam_init, pos0)

    merged = _merge(y_m.reshape(b * t, d_inner), y_a.reshape(b * t, d), w['w_proj_m'], w['w_proj_a'],
                    proj_a.reshape(b * t, -1), gm_off, ga_off)
    y = _out_proj(merged, w['w_out'], x, gate, final_w)
    kvh = d // QH_W
    return (y, k_new.reshape(b, t, kvh, 2, A_HEAD_DIM), v_new.reshape(b, t, kvh, KV_W), conv_new, ssm_new)


def kernel(x_prompt, x_sample, cache_k, cache_v, state_conv, state_ssm, c_prompt, c_sample,
           w_ada, b_ada, norm_w, w_in, conv_w, conv_b, dt_bias, a_log, d_skip, mamba_norm_w,
           lam_q1, lam_k1, lam_q2, lam_k2, attn_norm_w, w_proj_m, w_proj_a, w_out, final_norm_w):
    depth = w_in.shape[0]
    assert depth == 1, "the final norm is fused into the single layer's output projection"
    bp, d = c_prompt.shape
    bs = c_sample.shape[0]
    past = cache_k.shape[2]
    d_inner = 2 * d
    groups = d_inner // GROUP_W
    heads = groups * M_HPG
    conv_dim = d_inner + 2 * groups * M_STATE
    kvw = (d // QH_W) * KV_W
    sizes = (d_inner, conv_dim, heads, d, kvw, kvw, d, d, d)
    offs = [0]
    for s in sizes:
        offs.append(offs[-1] + s)

    i = 0
    wi = w_in[i]
    w = {
        'w_in': w_in, 'layer': i, 'dt_col0': offs[2], 'a_col0': offs[3],
        'norm_w': norm_w[i], 'conv_w': conv_w[i], 'conv_b': conv_b[i], 'dt_bias': dt_bias[i], 'a_log': a_log[i],
        'd_skip': d_skip[i], 'mamba_norm_w': mamba_norm_w[i], 'lam_q1': lam_q1[i], 'lam_k1': lam_k1[i],
        'lam_q2': lam_q2[i], 'lam_k2': lam_k2[i], 'attn_norm_w': attn_norm_w[i],
        'w_proj_m': w_proj_m[i].astype(BF16), 'w_proj_a': w_proj_a[i].astype(BF16), 'w_out': w_out[i].astype(BF16),
    }
    lam_init = 0.8 - 0.6 * math.exp(-0.3 * i)
    pad_rows = -(bp + bs) % 16
    c_all = jnp.concatenate([c_prompt, c_sample, jnp.zeros((pad_rows, d), F32)], axis=0)
    mod = _ada_mod(c_all, w_ada[i], b_ada[i])

    yp, kp, vp, cp, sp = _layer_path(x_prompt, mod[:bp], None, None, None, None, 0, w, lam_init, final_norm_w)
    ck = cache_k[i].reshape(bs, past, kvw)
    cv = cache_v[i].reshape(bs, past, kvw)
    ys, kq, vq, cq, sq = _layer_path(x_sample, mod[bp:bp + bs], ck, cv, state_conv[i], state_ssm[i], past, w, lam_init,
                                     final_norm_w)
    st = lambda a: a[None]
    return (yp, ys, st(kp), st(vp), st(cp), st(sp), st(kq), st(vq), st(cq), st(sq))
```

```python
import functools
import math

import jax
import jax.numpy as jnp
from jax import lax
from jax.experimental import pallas as pl
from jax.experimental.pallas import tpu as pltpu

F32 = jnp.float32
BF16 = jnp.bfloat16

CHUNK = 64
EPS = 1e-6
M_HEAD_DIM = 64
M_HPG = 16
M_STATE = 128
M_CONV = 4
GROUP_W = M_HPG * M_HEAD_DIM
A_HEAD_DIM = 64
A_REP = 4
KV_W = 2 * A_HEAD_DIM
QH_W = A_REP * KV_W
ROPE_DIM = 16
ROPE_THETA = 500000.0
LANES = 128
VMEM_LIMIT_BYTES = 56 * 1024 * 1024


def _pick(n, target, mult=8):
    if n <= target:
        return n
    for t in range(target, 0, -1):
        if n % t == 0 and t % mult == 0:
            return t
    return n


def _params(*sem):
    return pltpu.CompilerParams(dimension_semantics=sem, vmem_limit_bytes=VMEM_LIMIT_BYTES)


def _silu(x):
    return x * jax.nn.sigmoid(x)


def _split_hi_lo(x):
    hi = x.astype(BF16)
    lo = (x - hi.astype(F32)).astype(BF16)
    return hi, lo


def _dot(a, b):
    return jnp.dot(a, b, preferred_element_type=F32)


def _dot_nt(a, b):
    return lax.dot_general(a, b, (((1,), (1,)), ((), ())), preferred_element_type=F32)


def _dot_tn(a, b):
    return lax.dot_general(a, b, (((0,), (0,)), ((), ())), preferred_element_type=F32)


def _ada_kernel(c_ref, w_ref, b_ref, o_ref):
    a = _silu(c_ref[...]).astype(BF16)
    o_ref[...] = _dot(a, w_ref[...].astype(BF16)) + b_ref[...]


def _ada_mod(c, w_ada, b_ada):
    m, d = c.shape
    n = w_ada.shape[1]
    tn = _pick(n, 512, LANES)
    return pl.pallas_call(
        _ada_kernel,
        grid=(n // tn,),
        in_specs=[pl.BlockSpec((m, d), lambda j: (0, 0)),
                  pl.BlockSpec((d, tn), lambda j: (0, j)),
                  pl.BlockSpec((1, tn), lambda j: (0, j))],
        out_specs=pl.BlockSpec((m, tn), lambda j: (0, j)),
        out_shape=jax.ShapeDtypeStruct((m, n), F32),
        compiler_params=_params("arbitrary"),
        name="ada_mod",
    )(c, w_ada, b_ada.reshape(1, n))


def _prenorm_kernel(x_ref, nw_ref, sc_ref, sh_ref, o_ref):
    x = x_ref[0]
    ms = jnp.mean(x * x, axis=-1, keepdims=True)
    y = x * lax.rsqrt(ms + EPS) * nw_ref[...]
    o_ref[0] = (y * (1.0 + sc_ref[0]) + sh_ref[0]).astype(o_ref.dtype)


def _prenorm(x, norm_w, scale, shift):
    b, t, d = x.shape
    tr = _pick(t, 256)
    return pl.pallas_call(
        _prenorm_kernel,
        grid=(b, t // tr),
        in_specs=[pl.BlockSpec((1, tr, d), lambda i, j: (i, j, 0)),
                  pl.BlockSpec((1, d), lambda i, j: (0, 0)),
                  pl.BlockSpec((1, 1, d), lambda i, j: (i, 0, 0)),
                  pl.BlockSpec((1, 1, d), lambda i, j: (i, 0, 0))],
        out_specs=pl.BlockSpec((1, tr, d), lambda i, j: (i, j, 0)),
        out_shape=jax.ShapeDtypeStruct((b, t, d), BF16),
        compiler_params=_params("arbitrary", "arbitrary"),
        name="prenorm",
    )(x, norm_w.reshape(1, d), scale.reshape(b, 1, d), shift.reshape(b, 1, d))


def _mm_kernel(a_ref, w_ref, o_ref):
    o_ref[...] = _dot(a_ref[...], w_ref[...]).astype(o_ref.dtype)


def _mm_castw_kernel(a_ref, w_ref, o_ref):
    o_ref[...] = _dot(a_ref[...], w_ref[0].astype(BF16))


def _matmul_f32w(a, w3, layer, col0, n, name):
    m, k = a.shape
    tm = _pick(m, 1024)
    tn = _pick(n, 512, LANES)
    if col0 % LANES:
        w3, col0 = w3[:, :, col0:col0 + n], 0
    return pl.pallas_call(
        _mm_castw_kernel,
        grid=(m // tm, n // tn),
        in_specs=[pl.BlockSpec((tm, k), lambda i, j: (i, 0)),
                  pl.BlockSpec((pl.Element(1), pl.Element(k), pl.Element(tn)),
                               lambda i, j: (layer, 0, pl.multiple_of(col0 + j * tn, LANES)))],
        out_specs=pl.BlockSpec((tm, tn), lambda i, j: (i, j)),
        out_shape=jax.ShapeDtypeStruct((m, n), F32),
        compiler_params=_params("arbitrary", "arbitrary"),
        name=name,
    )(a, w3)


def _matmul(a, w, name):
    m, k = a.shape
    n = w.shape[1]
    tm = _pick(m, 1024)
    tn = _pick(n, 1024, LANES)
    return pl.pallas_call(
        _mm_kernel,
        grid=(n // tn, m // tm),
        in_specs=[pl.BlockSpec((tm, k), lambda j, i: (i, 0)),
                  pl.BlockSpec((k, tn), lambda j, i: (0, j))],
        out_specs=pl.BlockSpec((tm, tn), lambda j, i: (i, j)),
        out_shape=jax.ShapeDtypeStruct((m, n), F32),
        compiler_params=_params("arbitrary", "arbitrary"),
        name=name,
    )(a, w)


def _dt_kernel(a_ref, w_ref, b_ref, o_ref, wb_scr):
    @pl.when(pl.program_id(0) == 0)
    def _():
        wb_scr[...] = w_ref[0].astype(BF16)

    x = _dot(a_ref[...], wb_scr[...]) + b_ref[...]
    o_ref[...] = jnp.maximum(x, 0.0) + jnp.log1p(jnp.exp(-jnp.abs(x)))


def _dt_proj(a, w3, layer, col0, dt_bias):
    m, k = a.shape
    n = dt_bias.shape[0]
    tm = _pick(m, 512)
    if col0 % LANES or n % LANES:
        w3, col0 = w3[:, :, col0:col0 + n], 0
    return pl.pallas_call(
        _dt_kernel,
        grid=(m // tm,),
        in_specs=[pl.BlockSpec((tm, k), lambda i: (i, 0)),
                  pl.BlockSpec((pl.Element(1), pl.Element(k), pl.Element(n)),
                               lambda i: (layer, 0, col0)),
                  pl.BlockSpec((1, n), lambda i: (0, 0))],
        out_specs=pl.BlockSpec((tm, n), lambda i: (i, 0)),
        out_shape=jax.ShapeDtypeStruct((m, n), F32),
        scratch_shapes=[pltpu.VMEM((k, n), BF16)],
        compiler_params=_params("arbitrary"),
        name="dt_proj",
    )(a, w3, dt_bias.reshape(1, n))


def _rope_slab(x, cos, sin_up, sin_dn):
    return x * cos + pltpu.roll(x, LANES - ROPE_DIM // 2, 1) * sin_up + pltpu.roll(x, ROPE_DIM // 2, 1) * sin_dn


def _rope_kernel(q_ref, k_ref, v_ref, cos_ref, sup_ref, sdn_ref, qo_ref, ko_ref, vo_ref, kb_ref, vb_ref, *, q_scale):
    cos, sup, sdn = cos_ref[...], sup_ref[...], sdn_ref[...]
    for s in range(q_ref.shape[2] // LANES):
        sl = slice(s * LANES, (s + 1) * LANES)
        qo_ref[0, :, sl] = (_rope_slab(q_ref[0, :, sl], cos, sup, sdn) * q_scale).astype(qo_ref.dtype)
    for s in range(k_ref.shape[2] // LANES):
        sl = slice(s * LANES, (s + 1) * LANES)
        k = _rope_slab(k_ref[0, :, sl], cos, sup, sdn)
        ko_ref[0, :, sl] = k
        kb_ref[0, :, sl] = k.astype(kb_ref.dtype)
    v = v_ref[...]
    vo_ref[...] = v
    vb_ref[...] = v.astype(vb_ref.dtype)


def _rope_tables(pos):
    half = ROPE_DIM // 2
    inv = ROPE_THETA ** (-jnp.arange(half, dtype=F32) / half)
    ang = pos.astype(F32)[:, None] * inv[None, :]
    cos, sin = jnp.cos(ang), jnp.sin(ang)
    t = pos.shape[0]
    ones = jnp.ones((t, A_HEAD_DIM - ROPE_DIM), F32)
    zeros = jnp.zeros((t, A_HEAD_DIM - ROPE_DIM), F32)
    zh = jnp.zeros((t, half), F32)
    cos64 = jnp.concatenate([cos, cos, ones], axis=1)
    sup64 = jnp.concatenate([-sin, zh, zeros], axis=1)
    sdn64 = jnp.concatenate([zh, sin, zeros], axis=1)
    rep = lambda a: jnp.concatenate([a, a], axis=1)
    return rep(cos64), rep(sup64), rep(sdn64)


def _rope_kv(proj, pos, d, kvw, k_blk, v_blk):
    b, t, _ = proj.shape
    tr = _pick(t, 256)
    cos, sup, sdn = _rope_tables(pos)
    tab = pl.BlockSpec((tr, LANES), lambda i, j: (j, 0))
    kv_out = pl.BlockSpec((1, tr, kvw), lambda i, j: (i, j, 0))
    return pl.pallas_call(
        functools.partial(_rope_kernel, q_scale=A_HEAD_DIM ** -0.5 * math.log2(math.e)),
        grid=(b, t // tr),
        in_specs=[pl.BlockSpec((1, tr, d), lambda i, j: (i, j, 0)),
                  pl.BlockSpec((1, tr, kvw), lambda i, j: (i, j, k_blk)),
                  pl.BlockSpec((1, tr, kvw), lambda i, j: (i, j, v_blk)),
                  tab, tab, tab],
        out_specs=[pl.BlockSpec((1, tr, d), lambda i, j: (i, j, 0))] + [kv_out] * 4,
        out_shape=[jax.ShapeDtypeStruct((b, t, d), BF16),
                   jax.ShapeDtypeStruct((b, t, kvw), F32),
                   jax.ShapeDtypeStruct((b, t, kvw), F32),
                   jax.ShapeDtypeStruct((b, t, kvw), BF16),
                   jax.ShapeDtypeStruct((b, t, kvw), BF16)],
        compiler_params=_params("arbitrary", "arbitrary"),
        name="rope_kv",
    )(proj, proj, proj, cos, sup, sdn)


def _ssd_kernel(*refs, tl, has_state):
    (alog_ref, alr_ref, dsk_ref, mnw_ref, cwx_ref, cwb_ref, cwc_ref, cbx_ref, cbb_ref, cbc_ref,
     z_ref, x_ref, bm_ref, cm_ref, dt_ref, dtr_ref) = refs[:16]
    if has_state:
        csx_ref, csb_ref, csc_ref, s0_ref = refs[16:20]
        rest = refs[20:]
    else:
        rest = refs[16:]
    y_ref, sout_ref, xbuf, bbuf, cbuf, st_scr = rest
    g = pl.program_id(1)
    c = pl.program_id(2)
    L = CHUNK
    nsub = tl // L
    halo = M_CONV - 1
    base = 8

    @pl.when(c == 0)
    def _():
        if has_state:
            xbuf[base - halo:base, :] = csx_ref[0]
            bbuf[base - halo:base, :] = csb_ref[0]
            cbuf[base - halo:base, :] = csc_ref[0]
            st_scr[...] = s0_ref[0].reshape(GROUP_W, M_STATE).T
        else:
            xbuf[0:base, :] = jnp.zeros((base, GROUP_W), F32)
            bbuf[0:base, :] = jnp.zeros((base, M_STATE), F32)
            cbuf[0:base, :] = jnp.zeros((base, M_STATE), F32)
            st_scr[...] = jnp.zeros_like(st_scr)

    xbuf[base:base + tl, :] = x_ref[0]
    bbuf[base:base + tl, :] = bm_ref[0]
    cbuf[base:base + tl, :] = cm_ref[0]

    r64 = lax.broadcasted_iota(jnp.int32, (L, L), 0)
    c64 = lax.broadcasted_iota(jnp.int32, (L, L), 1)
    tril = (c64 <= r64).astype(BF16)
    heads = dt_ref.shape[2]
    hrow = lax.broadcasted_iota(jnp.int32, (heads, GROUP_W), 0)
    hcol = lax.broadcasted_iota(jnp.int32, (heads, GROUP_W), 1)
    expand = (hrow == g * M_HPG + hcol // M_HEAD_DIM).astype(BF16)
    br = lax.broadcasted_iota(jnp.int32, (LANES, LANES), 0)
    bc = lax.broadcasted_iota(jnp.int32, (LANES, LANES), 1)
    same_half = (br // L) == (bc // L)
    triu2 = (same_half & ((br % L) <= (bc % L))).astype(BF16)
    pr = lax.broadcasted_iota(jnp.int32, (L, LANES), 0)
    pc = lax.broadcasted_iota(jnp.int32, (L, LANES), 1)
    causal2 = (pc % L) <= pr

    a_col = -jnp.exp(alog_ref[...])
    a_row = -jnp.exp(alr_ref[0])
    n_rows = dtr_ref.shape[2]
    da_r = dtr_ref[0, 0] * jnp.concatenate([a_row] * (n_rows // 8), axis=0)
    hi, lo = _split_hi_lo(da_r)
    acum_r_all = _dot(hi, triu2) + _dot(lo, triu2)

    def conv(buf, w_ref, b_ref, i):
        acc = b_ref[...] + w_ref[M_CONV - 1:M_CONV, :] * buf[base + i * L:base + (i + 1) * L, :]
        for k in range(M_CONV - 1):
            off = base - halo + k + i * L
            acc = acc + w_ref[k:k + 1, :] * buf[off:off + L, :]
        return _silu(acc)

    for i in range(nsub):
        rows = slice(i * L, (i + 1) * L)
        x = conv(xbuf, cwx_ref, cbx_ref, i)
        bm = conv(bbuf, cwb_ref, cbb_ref, i).astype(BF16)
        cm = conv(cbuf, cwc_ref, cbc_ref, i).astype(BF16)
        dt = dt_ref[0, rows, :]
        da = dt * a_col
        hi, lo = _split_hi_lo(da)
        acum = _dot(tril, hi) + _dot(tril, lo)
        hi, lo = _split_hi_lo(acum)
        acum_x = _dot(hi, expand) + _dot(lo, expand)
        hi, lo = _split_hi_lo(dt)
        dt_x = _dot(hi, expand) + _dot(lo, expand)
        xdt = x * dt_x
        xdt_b = xdt.astype(BF16)
        alast_x = acum_x[L - 1:L, :]
        cb2 = _dot_nt(cm, jnp.concatenate([bm, bm], axis=0))
        st_b = st_scr[...].astype(BF16)
        y_off = _dot(cm, st_b) * jnp.exp(acum_x)
        y_parts = []
        for jj in range(GROUP_W // LANES):
            ls = slice(jj * LANES, (jj + 1) * LANES)
            seg = acum_x[:, ls] - acum_r_all[i * 8 + jj:i * 8 + jj + 1, :]
            m2 = (jnp.where(causal2, jnp.exp(seg), 0.0) * cb2).astype(BF16)
            x2 = xdt_b[:, ls]
            rhs = jnp.where(same_half, jnp.concatenate([x2, x2], axis=0), jnp.zeros((), BF16))
            y_parts.append(_dot(m2, rhs))
        y = jnp.concatenate(y_parts, axis=1) + y_off + dsk_ref[0] * x
        y = y * _silu(z_ref[0, rows, :])
        ms = jnp.mean(y * y, axis=-1, keepdims=True)
        y_ref[0, rows, :] = (y * lax.rsqrt(ms + EPS) * mnw_ref[0]).astype(y_ref.dtype)
        xw = (xdt * jnp.exp(alast_x - acum_x)).astype(BF16)
        st_scr[...] = st_scr[...] * jnp.exp(alast_x) + _dot_tn(bm, xw)

    tx = xbuf[base + tl - halo:base + tl, :]
    tb = bbuf[base + tl - halo:base + tl, :]
    tc = cbuf[base + tl - halo:base + tl, :]
    xbuf[base - halo:base, :] = tx
    bbuf[base - halo:base, :] = tb
    cbuf[base - halo:base, :] = tc

    @pl.when(c == pl.num_programs(2) - 1)
    def _():
        sout_ref[0] = st_scr[...].T.reshape(M_HPG, M_HEAD_DIM, M_STATE)


def _ssd(proj, dt, conv_state, ssm_state, conv_w, conv_b, a_log, d_skip, mnorm_w, z_off, xbc_off, d_inner):
    b, t, _ = proj.shape
    groups = d_inner // GROUP_W
    heads = groups * M_HPG
    bcw = groups * M_STATE
    has_state = conv_state is not None
    tl = _pick(t, 1024, CHUNK)
    nsub = tl // CHUNK
    nc = t // tl
    dtr = dt.reshape(b, t // CHUNK, CHUNK, groups, M_HPG // 2, 2).transpose(0, 3, 1, 4, 5, 2)
    dtr = dtr.reshape(b, groups, (t // CHUNK) * 8, LANES)
    n_rows = nsub * 8
    if n_rows < 16:
        dtr = jnp.concatenate([dtr, jnp.zeros_like(dtr)], axis=2)
        n_rows = 16
    alr = jnp.repeat(a_log.reshape(groups, M_HPG // 2, 2), CHUNK, axis=2)
    dsk = jnp.repeat(d_skip.reshape(groups, 1, M_HPG), M_HEAD_DIM, axis=2)
    mnw = mnorm_w.reshape(groups, 1, GROUP_W)
    xb, bb, cb = xbc_off // GROUP_W, (xbc_off + d_inner) // M_STATE, (xbc_off + d_inner + bcw) // M_STATE
    cxb, cbb, ccb = 0, d_inner // M_STATE, (d_inner + bcw) // M_STATE
    zb = z_off // GROUP_W
    conv_b2 = conv_b.reshape(1, -1)
    gmap = lambda blk: (lambda i, g, c: (0, blk + g))
    in_specs = [
        pl.BlockSpec((1, heads), lambda i, g, c: (0, 0)),
        pl.BlockSpec((1, 8, LANES), lambda i, g, c: (g, 0, 0)),
        pl.BlockSpec((1, 1, GROUP_W), lambda i, g, c: (g, 0, 0)),
        pl.BlockSpec((1, 1, GROUP_W), lambda i, g, c: (g, 0, 0)),
        pl.BlockSpec((M_CONV, GROUP_W), gmap(cxb)),
        pl.BlockSpec((M_CONV, M_STATE), gmap(cbb)),
        pl.BlockSpec((M_CONV, M_STATE), gmap(ccb)),
        pl.BlockSpec((1, GROUP_W), gmap(cxb)),
        pl.BlockSpec((1, M_STATE), gmap(cbb)),
        pl.BlockSpec((1, M_STATE), gmap(ccb)),
        pl.BlockSpec((1, tl, GROUP_W), lambda i, g, c: (i, c, zb + g)),
        pl.BlockSpec((1, tl, GROUP_W), lambda i, g, c: (i, c, xb + g)),
        pl.BlockSpec((1, tl, M_STATE), lambda i, g, c: (i, c, bb + g)),
        pl.BlockSpec((1, tl, M_STATE), lambda i, g, c: (i, c, cb + g)),
        pl.BlockSpec((1, tl, heads), lambda i, g, c: (i, c, 0)),
        pl.BlockSpec((1, 1, n_rows, LANES), lambda i, g, c: (i, g, c, 0)),
    ]
    args = [a_log.reshape(1, heads), alr, dsk, mnw, conv_w, conv_w, conv_w, conv_b2, conv_b2, conv_b2,
            proj, proj, proj, proj, dt.reshape(b, t, heads), dtr]
    if has_state:
        halo = M_CONV - 1
        in_specs += [
            pl.BlockSpec((1, halo, GROUP_W), lambda i, g, c: (i, 0, cxb + g)),
            pl.BlockSpec((1, halo, M_STATE), lambda i, g, c: (i, 0, cbb + g)),
            pl.BlockSpec((1, halo, M_STATE), lambda i, g, c: (i, 0, ccb + g)),
            pl.BlockSpec((1, M_HPG, M_HEAD_DIM, M_STATE), lambda i, g, c: (i, g, 0, 0)),
        ]
        args += [conv_state, conv_state, conv_state, ssm_state]
    return pl.pallas_call(
        functools.partial(_ssd_kernel, tl=tl, has_state=has_state),
        grid=(b, groups, nc),
        in_specs=in_specs,
        out_specs=[pl.BlockSpec((1, tl, GROUP_W), lambda i, g, c: (i, c, g)),
                   pl.BlockSpec((1, M_HPG, M_HEAD_DIM, M_STATE), lambda i, g, c: (i, g, 0, 0))],
        out_shape=[jax.ShapeDtypeStruct((b, t, d_inner), BF16),
                   jax.ShapeDtypeStruct((b, heads, M_HEAD_DIM, M_STATE), F32)],
        scratch_shapes=[pltpu.VMEM((tl + 8, GROUP_W), F32),
                        pltpu.VMEM((tl + 8, M_STATE), F32),
                        pltpu.VMEM((tl + 8, M_STATE), F32),
                        pltpu.VMEM((M_STATE, GROUP_W), F32)],
        compiler_params=_params("arbitrary", "arbitrary", "arbitrary"),
        name="ssd",
    )(*args)


def _attn_kernel(*refs, tq, tk, rc, n_qt, past, tkp, pos0, lam_init):
    lq1_ref, lk1_ref, lq2_ref, lk2_ref, nw_ref, q_ref, k_ref, v_ref, za_ref = refs[:9]
    if past:
        kp_ref, vp_ref = refs[9:11]
        rest = refs[11:]
    else:
        rest = refs[9:]
    o_ref, qz_scr, s_scr, p_scr, m_scr, l_scr, acc_scr = rest
    qi = pl.program_id(2)
    rows = A_REP * tq

    lane = lax.broadcasted_iota(jnp.int32, (tq, KV_W), 1)
    zero = jnp.zeros((), BF16)
    for r in range(A_REP):
        q = q_ref[0, :, r * KV_W:(r + 1) * KV_W]
        qz_scr[0, r * tq:(r + 1) * tq, :] = jnp.where(lane < A_HEAD_DIM, q, zero)
        qz_scr[1, r * tq:(r + 1) * tq, :] = jnp.where(lane >= A_HEAD_DIM, q, zero)
    m_scr[...] = jnp.full(m_scr.shape, -jnp.inf, F32)
    l_scr[...] = jnp.zeros(l_scr.shape, F32)
    acc_scr[...] = jnp.zeros(acc_scr.shape, F32)

    def tile(k, v, nkeys, mask_fn):
        reps = nkeys // LANES
        for comp in range(2):
            s_scr[comp, :, :nkeys] = _dot_nt(qz_scr[comp], k)
            for i in range(rows // rc):
                rs = slice(i * rc, (i + 1) * rc)
                s = s_scr[comp, rs, :nkeys]
                if mask_fn is not None:
                    s = jnp.where(mask_fn(i), s, -jnp.inf)
                m_old = m_scr[comp, rs, :]
                m_new = jnp.maximum(m_old, jnp.max(s, axis=-1, keepdims=True))
                p = jnp.exp2(s - jnp.concatenate([m_new] * reps, axis=1))
                alpha = jnp.exp2(m_old - m_new)
                psum = p[:, :LANES]
                for u in range(1, reps):
                    psum = psum + p[:, u * LANES:(u + 1) * LANES]
                l_scr[comp, rs, :] = alpha * l_scr[comp, rs, :] + psum
                acc_scr[comp, rs, :] = alpha * acc_scr[comp, rs, :]
                m_scr[comp, rs, :] = m_new
                p_scr[comp, rs, :nkeys] = p.astype(BF16)
            acc_scr[comp] = acc_scr[comp] + _dot(p_scr[comp, :, :nkeys], v)

    def diag_mask(i, nkeys, n_valid, k_pos0):
        t0 = (i * rc) % tq
        qpos = pos0 + qi * tq + t0 + lax.broadcasted_iota(jnp.int32, (rc, nkeys), 0)
        kidx = lax.broadcasted_iota(jnp.int32, (rc, nkeys), 1)
        ok = ((k_pos0 + kidx) // CHUNK) <= (qpos // CHUNK)
        if n_valid < nkeys:
            ok = ok & (kidx < n_valid)
        return ok

    if past:
        def pbody(j, carry):
            sl = pl.ds(pl.multiple_of(j * tkp, tkp), tkp)
            tile(kp_ref[0, sl, :].astype(BF16), vp_ref[0, sl, :].astype(BF16), tkp, None)
            return carry

        lax.fori_loop(0, past // tkp, pbody, 0)

    def nbody(j, carry):
        sl = pl.ds(pl.multiple_of(j * tk, tk), tk)
        tile(k_ref[0, sl, :], v_ref[0, sl, :], tk, None)
        return carry

    if n_qt > 1:
        lax.fori_loop(0, (qi * tq) // tk, nbody, 0)
    tkd = max(tk, LANES)
    for d in range(tq // tk):
        k0 = qi * tq + d * tk
        sl = pl.ds(pl.multiple_of(k0, tk), tk)
        k, v = k_ref[0, sl, :], v_ref[0, sl, :]
        if tkd > tk:
            pad = jnp.zeros((tkd - tk, KV_W), BF16)
            k, v = jnp.concatenate([k, pad], axis=0), jnp.concatenate([v, pad], axis=0)
        tile(k, v, tkd, functools.partial(diag_mask, nkeys=tkd, n_valid=tk, k_pos0=pos0 + k0))

    lam = (jnp.exp(jnp.sum(lq1_ref[...] * lk1_ref[...], axis=-1, keepdims=True))
           - jnp.exp(jnp.sum(lq2_ref[...] * lk2_ref[...], axis=-1, keepdims=True)) + lam_init)
    l0 = jnp.sum(l_scr[0], axis=-1, keepdims=True)
    l1 = jnp.sum(l_scr[1], axis=-1, keepdims=True)
    o = acc_scr[0] / l0 - lam * (acc_scr[1] / l1)
    ms = jnp.mean(o * o, axis=-1, keepdims=True)
    on = o * lax.rsqrt(ms + EPS) * nw_ref[...] * (1.0 - lam_init)
    for r in range(A_REP):
        sl = slice(r * KV_W, (r + 1) * KV_W)
        o_ref[0, :, sl] = (on[r * tq:(r + 1) * tq, :] * _silu(za_ref[0, :, sl])).astype(o_ref.dtype)


def _attention(qr, kb, vb, proj, za_off, k_past, v_past, lam_vecs, norm_w, lam_init, pos0):
    b, t, d = qr.shape
    kvh = d // QH_W
    past = 0 if k_past is None else k_past.shape[1]
    tq = _pick(t, 512, CHUNK)
    tk = tq
    tkp = _pick(past, 2048, LANES) if past else 0
    rc = min(64, tq)
    zb = za_off // QH_W
    rows = A_REP * tq
    smax = max(tk, LANES, tkp)
    vec = pl.BlockSpec((1, A_HEAD_DIM), lambda i, h, j: (0, 0))
    in_specs = [vec, vec, vec, vec,
                pl.BlockSpec((1, KV_W), lambda i, h, j: (0, 0)),
                pl.BlockSpec((1, tq, QH_W), lambda i, h, j: (i, j, h)),
                pl.BlockSpec((1, t, KV_W), lambda i, h, j: (i, 0, h)),
                pl.BlockSpec((1, t, KV_W), lambda i, h, j: (i, 0, h)),
                pl.BlockSpec((1, tq, QH_W), lambda i, h, j: (i, j, zb + h))]
    args = [v.reshape(1, A_HEAD_DIM) for v in lam_vecs] + [norm_w.reshape(1, KV_W), qr, kb, vb, proj]
    if past:
        in_specs += [pl.BlockSpec((1, past, KV_W), lambda i, h, j: (i, 0, h)),
                     pl.BlockSpec((1, past, KV_W), lambda i, h, j: (i, 0, h))]
        args += [k_past, v_past]
    return pl.pallas_call(
        functools.partial(_attn_kernel, tq=tq, tk=tk, rc=rc, n_qt=t // tq, past=past, tkp=tkp, pos0=pos0,
                          lam_init=lam_init),
        grid=(b, kvh, t // tq),
        in_specs=in_specs,
        out_specs=pl.BlockSpec((1, tq, QH_W), lambda i, h, j: (i, j, h)),
        out_shape=jax.ShapeDtypeStruct((b, t, d), BF16),
        scratch_shapes=[pltpu.VMEM((2, rows, KV_W), BF16),
                        pltpu.VMEM((2, rows, smax), F32),
                        pltpu.VMEM((2, rows, smax), BF16),
                        pltpu.VMEM((2, rows, LANES), F32),
                        pltpu.VMEM((2, rows, LANES), F32),
                        pltpu.VMEM((2, rows, KV_W), F32)],
        compiler_params=_params("arbitrary", "arbitrary", "arbitrary"),
        name="diff_attn",
    )(*args)


def _merge_kernel(ym_ref, ya_ref, wm_ref, wa_ref, gm_ref, ga_ref, o_ref):
    pm = _dot(ym_ref[...], wm_ref[...])
    pa = _dot(ya_ref[...], wa_ref[...])
    o_ref[...] = (jax.nn.sigmoid(gm_ref[...]) * pm + jax.nn.sigmoid(ga_ref[...]) * pa).astype(o_ref.dtype)


def _merge(y_m, y_a, w_pm, w_pa, proj2d, gm_off, ga_off):
    m, km = y_m.shape
    ka = y_a.shape[1]
    d = w_pm.shape[1]
    tm = _pick(m, 512)
    tn = _pick(d, 512, LANES)
    gmb, gab = gm_off // tn, ga_off // tn
    once = pl.Buffered(1)
    return pl.pallas_call(
        _merge_kernel,
        grid=(d // tn, m // tm),
        in_specs=[pl.BlockSpec((tm, km), lambda j, i: (i, 0)),
                  pl.BlockSpec((tm, ka), lambda j, i: (i, 0)),
                  pl.BlockSpec((km, tn), lambda j, i: (0, j), pipeline_mode=once),
                  pl.BlockSpec((ka, tn), lambda j, i: (0, j), pipeline_mode=once),
                  pl.BlockSpec((tm, tn), lambda j, i: (i, gmb + j)),
                  pl.BlockSpec((tm, tn), lambda j, i: (i, gab + j))],
        out_specs=pl.BlockSpec((tm, tn), lambda j, i: (i, j)),
        out_shape=jax.ShapeDtypeStruct((m, d), BF16),
        compiler_params=_params("arbitrary", "arbitrary"),
        name="merge",
    )(y_m, y_a, w_pm, w_pa, proj2d, proj2d)


def _out_kernel(mg_ref, w_ref, x_ref, gate_ref, fw_ref, o_ref, *, tn):
    j = pl.program_id(2)
    cols = pl.ds(pl.multiple_of(j * tn, tn), tn)
    o_ref[0, :, cols] = x_ref[0] + gate_ref[0] * _dot(mg_ref[0], w_ref[...])

    @pl.when(j == pl.num_programs(2) - 1)
    def _():
        r = o_ref[0]
        ms = jnp.mean(r * r, axis=-1, keepdims=True)
        o_ref[0] = r * lax.rsqrt(ms + EPS) * fw_ref[...]


def _out_proj(merged, w_out, x, gate, final_w):
    shape = x.shape
    b, t, d = shape
    gate = gate.reshape(b, 1, d)
    per_row_gate = t < 512 and b > 1
    if per_row_gate:
        gate = jnp.broadcast_to(gate, (b, t, d)).reshape(1, b * t, d)
        x = x.reshape(1, b * t, d)
        b, t = 1, b * t
    tm = _pick(t, 512)
    tn = _pick(d, 512, LANES)
    gate_spec =(pl.BlockSpec((1, tm, tn), lambda i, r, j: (i, r, j)) if per_row_gate else
                 pl.BlockSpec((1, 1, tn), lambda i, r, j: (i, 0, j)))
    out = pl.pallas_call(
        functools.partial(_out_kernel, tn=tn),
        grid=(b, t // tm, d // tn),
        in_specs=[pl.BlockSpec((1, tm, d), lambda i, r, j: (i, r, 0)),
                  pl.BlockSpec((d, tn), lambda i, r, j: (0, j)),
                  pl.BlockSpec((1, tm, tn), lambda i, r, j: (i, r, j)),
                  gate_spec,
                  pl.BlockSpec((1, d), lambda i, r, j: (0, 0))],
        out_specs=pl.BlockSpec((1, tm, d), lambda i, r, j: (i, r, 0)),
        out_shape=jax.ShapeDtypeStruct((b, t, d), F32),
        compiler_params=_params("arbitrary", "arbitrary", "arbitrary"),
        name="out_proj",
    )(merged.reshape(b, t, d), w_out, x, gate, final_w.reshape(1, d))
    return out.reshape(shape)


def _layer_path(x, mod, k_past, v_past, conv_state, ssm_state, pos0, w, lam_init, final_w):
    b, t, d = x.shape
    d_inner = 2 * d
    groups = d_inner // GROUP_W
    bcw = groups * M_STATE
    conv_dim = d_inner + 2 * bcw
    kvw = (d // QH_W) * KV_W
    z_off, xbc_off = 0, d_inner
    k_off, v_off, za_off, gm_off, ga_off = d, d + kvw, d + 2 * kvw, 2 * d + 2 * kvw, 3 * d + 2 * kvw
    assert k_off % kvw == 0 and za_off % QH_W == 0, "consumer column blocks must be block-aligned"
    shift, scale, gate = mod[:, :d], mod[:, d:2 * d], mod[:, 2 * d:]

    h = _prenorm(x, w['norm_w'], scale, shift).reshape(b * t, d)
    proj_m = _matmul_f32w(h, w['w_in'], w['layer'], 0, xbc_off + conv_dim, "in_proj_m").reshape(b, t, -1)
    proj_a = _matmul_f32w(h, w['w_in'], w['layer'], w['a_col0'], ga_off + d, "in_proj_a").reshape(b, t, -1)
    dt = _dt_proj(h, w['w_in'], w['layer'], w['dt_col0'], w['dt_bias']).reshape(b, t, -1)

    pos = pos0 + jnp.arange(t, dtype=jnp.int32)
    qr, k_new, v_new, kb, vb = _rope_kv(proj_a, pos, d, kvw, k_off // kvw, v_off // kvw)

    y_m, ssm_new = _ssd(proj_m, dt, conv_state, ssm_state, w['conv_w'], w['conv_b'], w['a_log'], w['d_skip'],
                        w['mamba_norm_w'], z_off, xbc_off, d_inner)
    halo = M_CONV - 1
    conv_new = proj_m[:, t - halo:, xbc_off:xbc_off + conv_dim]

    y_a = _attention(qr, kb, vb, proj_a, za_off, k_past, v_past,
                     (w['lam_q1'], w['lam_k1'], w['lam_q2'], w['lam_k2']), w['attn_norm_w'], lam_init, pos0)

    merged = _merge(y_m.reshape(b * t, d_inner), y_a.reshape(b * t, d), w['w_proj_m'], w['w_proj_a'],
                    proj_a.reshape(b * t, -1), gm_off, ga_off)
    y = _out_proj(merged, w['w_out'], x, gate, final_w)
    kvh = d // QH_W
    return (y, k_new.reshape(b, t, kvh, 2, A_HEAD_DIM), v_new.reshape(b, t, kvh, KV_W), conv_new, ssm_new)


def kernel(x_prompt, x_sample, cache_k, cache_v, state_conv, state_ssm, c_prompt, c_sample,
           w_ada, b_ada, norm_w, w_in, conv_w, conv_b, dt_bias, a_log, d_skip, mamba_norm_w,
           lam_q1, lam_k1, lam_q2, lam_k2, attn_norm_w, w_proj_m, w_proj_a, w_out, final_norm_w):
    depth = w_in.shape[0]
    assert depth == 1, "the final norm is fused into the single layer's output projection"
    bp, d = c_prompt.shape
    bs = c_sample.shape[0]
    past = cache_k.shape[2]
    d_inner = 2 * d
    groups = d_inner // GROUP_W
    heads = groups * M_HPG
    conv_dim = d_inner + 2 * groups * M_STATE
    kvw = (d // QH_W) * KV_W
    sizes = (d_inner, conv_dim, heads, d, kvw, kvw, d, d, d)
    offs = [0]
    for s in sizes:
        offs.append(offs[-1] + s)

    i = 0
    wi = w_in[i]
    w = {
        'w_in': w_in, 'layer': i, 'dt_col0': offs[2], 'a_col0': offs[3],
        'norm_w': norm_w[i], 'conv_w': conv_w[i], 'conv_b': conv_b[i], 'dt_bias': dt_bias[i], 'a_log': a_log[i],
        'd_skip': d_skip[i], 'mamba_norm_w': mamba_norm_w[i], 'lam_q1': lam_q1[i], 'lam_k1': lam_k1[i],
        'lam_q2': lam_q2[i], 'lam_k2': lam_k2[i], 'attn_norm_w': attn_norm_w[i],
        'w_proj_m': w_proj_m[i].astype(BF16), 'w_proj_a': w_proj_a[i].astype(BF16), 'w_out': w_out[i].astype(BF16),
    }
    lam_init = 0.8 - 0.6 * math.exp(-0.3 * i)
    pad_rows = -(bp + bs) % 16
    c_all = jnp.concatenate([c_prompt, c_sample, jnp.zeros((pad_rows, d), F32)], axis=0)
    mod = _ada_mod(c_all, w_ada[i], b_ada[i])

    yp, kp, vp, cp, sp = _layer_path(x_prompt, mod[:bp], None, None, None, None, 0, w, lam_init, final_norm_w)
    ck = cache_k[i].reshape(bs, past, kvw)
    cv = cache_v[i].reshape(bs, past, kvw)
    ys, kq, vq, cq, sq = _layer_path(x_sample, mod[bp:bp + bs], ck, cv, state_conv[i], state_ssm[i], past, w, lam_init,
                                     final_norm_w)
    st = lambda a: a[None]
    return (yp, ys, st(kp), st(vp), st(cp), st(sp), st(kq), st(vq), st(cq), st(sq))
```

```python
import functools
import math

import jax
import jax.numpy as jnp
from jax import lax
from jax.experimental import pallas as pl
from jax.experimental.pallas import tpu as pltpu

F32 = jnp.float32
BF16 = jnp.bfloat16

CHUNK = 64
EPS = 1e-6
M_HEAD_DIM = 64
M_HPG = 16
M_STATE = 128
M_CONV = 4
GROUP_W = M_HPG * M_HEAD_DIM
A_HEAD_DIM = 64
A_REP = 4
KV_W = 2 * A_HEAD_DIM
QH_W = A_REP * KV_W
ROPE_DIM = 16
ROPE_THETA = 500000.0
LANES = 128
VMEM_LIMIT_BYTES = 56 * 1024 * 1024


def _pick(n, target, mult=8):
    if n <= target:
        return n
    for t in range(target, 0, -1):
        if n % t == 0 and t % mult == 0:
            return t
    return n


def _params(*sem):
    return pltpu.CompilerParams(dimension_semantics=sem, vmem_limit_bytes=VMEM_LIMIT_BYTES)


def _silu(x):
    return x * jax.nn.sigmoid(x)


def _split_hi_lo(x):
    hi = x.astype(BF16)
    lo = (x - hi.astype(F32)).astype(BF16)
    return hi, lo


def _dot(a, b):
    return jnp.dot(a, b, preferred_element_type=F32)


def _dot_nt(a, b):
    return lax.dot_general(a, b, (((1,), (1,)), ((), ())), preferred_element_type=F32)


def _dot_tn(a, b):
    return lax.dot_general(a, b, (((0,), (0,)), ((), ())), preferred_element_type=F32)


def _ada_kernel(c_ref, w_ref, b_ref, o_ref):
    a = _silu(c_ref[...]).astype(BF16)
    o_ref[...] = _dot(a, w_ref[...].astype(BF16)) + b_ref[...]


def _ada_mod(c, w_ada, b_ada):
    m, d = c.shape
    n = w_ada.shape[1]
    tn = _pick(n, 512, LANES)
    return pl.pallas_call(
        _ada_kernel,
        grid=(n // tn,),
        in_specs=[pl.BlockSpec((m, d), lambda j: (0, 0)),
                  pl.BlockSpec((d, tn), lambda j: (0, j)),
                  pl.BlockSpec((1, tn), lambda j: (0, j))],
        out_specs=pl.BlockSpec((m, tn), lambda j: (0, j)),
        out_shape=jax.ShapeDtypeStruct((m, n), F32),
        compiler_params=_params("arbitrary"),
        name="ada_mod",
    )(c, w_ada, b_ada.reshape(1, n))


def _prenorm_kernel(x_ref, nw_ref, sc_ref, sh_ref, o_ref):
    x = x_ref[0]
    ms = jnp.mean(x * x, axis=-1, keepdims=True)
    y = x * lax.rsqrt(ms + EPS) * nw_ref[...]
    o_ref[0] = (y * (1.0 + sc_ref[0]) + sh_ref[0]).astype(o_ref.dtype)


def _prenorm(x, norm_w, scale, shift):
    b, t, d = x.shape
    tr = _pick(t, 256)
    return pl.pallas_call(
        _prenorm_kernel,
        grid=(b, t // tr),
        in_specs=[pl.BlockSpec((1, tr, d), lambda i, j: (i, j, 0)),
                  pl.BlockSpec((1, d), lambda i, j: (0, 0)),
                  pl.BlockSpec((1, 1, d), lambda i, j: (i, 0, 0)),
                  pl.BlockSpec((1, 1, d), lambda i, j: (i, 0, 0))],
        out_specs=pl.BlockSpec((1, tr, d), lambda i, j: (i, j, 0)),
        out_shape=jax.ShapeDtypeStruct((b, t, d), BF16),
        compiler_params=_params("arbitrary", "arbitrary"),
        name="prenorm",
    )(x, norm_w.reshape(1, d), scale.reshape(b, 1, d), shift.reshape(b, 1, d))


def _mm_kernel(a_ref, w_ref, o_ref):
    o_ref[...] = _dot(a_ref[...], w_ref[...]).astype(o_ref.dtype)


def _mm_castw_kernel(a_ref, w_ref, o_ref):
    o_ref[...] = _dot(a_ref[...], w_ref[0].astype(BF16))


def _matmul_f32w(a, w3, layer, col0, n, name):
    m, k = a.shape
    tm = _pick(m, 1024)
    tn = _pick(n, 512, LANES)
    if col0 % LANES:
        w3, col0 = w3[:, :, col0:col0 + n], 0
    return pl.pallas_call(
        _mm_castw_kernel,
        grid=(m // tm, n // tn),
        in_specs=[pl.BlockSpec((tm, k), lambda i, j: (i, 0)),
                  pl.BlockSpec((pl.Element(1), pl.Element(k), pl.Element(tn)),
                               lambda i, j: (layer, 0, pl.multiple_of(col0 + j * tn, LANES)))],
        out_specs=pl.BlockSpec((tm, tn), lambda i, j: (i, j)),
        out_shape=jax.ShapeDtypeStruct((m, n), F32),
        compiler_params=_params("arbitrary", "arbitrary"),
        name=name,
    )(a, w3)


def _matmul(a, w, name):
    m, k = a.shape
    n = w.shape[1]
    tm = _pick(m, 1024)
    tn = _pick(n, 1024, LANES)
    return pl.pallas_call(
        _mm_kernel,
        grid=(n // tn, m // tm),
        in_specs=[pl.BlockSpec((tm, k), lambda j, i: (i, 0)),
                  pl.BlockSpec((k, tn), lambda j, i: (0, j))],
        out_specs=pl.BlockSpec((tm, tn), lambda j, i: (i, j)),
        out_shape=jax.ShapeDtypeStruct((m, n), F32),
        compiler_params=_params("arbitrary", "arbitrary"),
        name=name,
    )(a, w)


def _dt_kernel(a_ref, w_ref, b_ref, o_ref, wb_scr):
    @pl.when(pl.program_id(0) == 0)
    def _():
        wb_scr[...] = w_ref[0].astype(BF16)

    x = _dot(a_ref[...], wb_scr[...]) + b_ref[...]
    o_ref[...] = jnp.maximum(x, 0.0) + jnp.log1p(jnp.exp(-jnp.abs(x)))


def _dt_proj(a, w3, layer, col0, dt_bias):
    m, k = a.shape
    n = dt_bias.shape[0]
    tm = _pick(m, 512)
    if col0 % LANES or n % LANES:
        w3, col0 = w3[:, :, col0:col0 + n], 0
    return pl.pallas_call(
        _dt_kernel,
        grid=(m // tm,),
        in_specs=[pl.BlockSpec((tm, k), lambda i: (i, 0)),
                  pl.BlockSpec((pl.Element(1), pl.Element(k), pl.Element(n)),
                               lambda i: (layer, 0, col0)),
                  pl.BlockSpec((1, n), lambda i: (0, 0))],
        out_specs=pl.BlockSpec((tm, n), lambda i: (i, 0)),
        out_shape=jax.ShapeDtypeStruct((m, n), F32),
        scratch_shapes=[pltpu.VMEM((k, n), BF16)],
        compiler_params=_params("arbitrary"),
        name="dt_proj",
    )(a, w3, dt_bias.reshape(1, n))


def _rope_slab(x, cos, sin_up, sin_dn):
    return x * cos + pltpu.roll(x, LANES - ROPE_DIM // 2, 1) * sin_up + pltpu.roll(x, ROPE_DIM // 2, 1) * sin_dn


def _rope_kernel(q_ref, k_ref, v_ref, cos_ref, sup_ref, sdn_ref, qo_ref, ko_ref, vo_ref, kb_ref, vb_ref, *, q_scale):
    cos, sup, sdn = cos_ref[...], sup_ref[...], sdn_ref[...]
    for s in range(q_ref.shape[2] // LANES):
        sl = slice(s * LANES, (s + 1) * LANES)
        qo_ref[0, :, sl] = (_rope_slab(q_ref[0, :, sl], cos, sup, sdn) * q_scale).astype(qo_ref.dtype)
    for s in range(k_ref.shape[2] // LANES):
        sl = slice(s * LANES, (s + 1) * LANES)
        k = _rope_slab(k_ref[0, :, sl], cos, sup, sdn)
        ko_ref[0, :, sl] = k
        kb_ref[0, :, sl] = k.astype(kb_ref.dtype)
    v = v_ref[...]
    vo_ref[...] = v
    vb_ref[...] = v.astype(vb_ref.dtype)


def _rope_tables(pos):
    half = ROPE_DIM // 2
    inv = ROPE_THETA ** (-jnp.arange(half, dtype=F32) / half)
    ang = pos.astype(F32)[:, None] * inv[None, :]
    cos, sin = jnp.cos(ang), jnp.sin(ang)
    t = pos.shape[0]
    ones = jnp.ones((t, A_HEAD_DIM - ROPE_DIM), F32)
    zeros = jnp.zeros((t, A_HEAD_DIM - ROPE_DIM), F32)
    zh = jnp.zeros((t, half), F32)
    cos64 = jnp.concatenate([cos, cos, ones], axis=1)
    sup64 = jnp.concatenate([-sin, zh, zeros], axis=1)
    sdn64 = jnp.concatenate([zh, sin, zeros], axis=1)
    rep = lambda a: jnp.concatenate([a, a], axis=1)
    return rep(cos64), rep(sup64), rep(sdn64)


def _rope_kv(proj, pos, d, kvw, k_blk, v_blk):
    b, t, _ = proj.shape
    tr = _pick(t, 256)
    cos, sup, sdn = _rope_tables(pos)
    tab = pl.BlockSpec((tr, LANES), lambda i, j: (j, 0))
    kv_out = pl.BlockSpec((1, tr, kvw), lambda i, j: (i, j, 0))
    return pl.pallas_call(
        functools.partial(_rope_kernel, q_scale=A_HEAD_DIM ** -0.5 * math.log2(math.e)),
        grid=(b, t // tr),
        in_specs=[pl.BlockSpec((1, tr, d), lambda i, j: (i, j, 0)),
                  pl.BlockSpec((1, tr, kvw), lambda i, j: (i, j, k_blk)),
                  pl.BlockSpec((1, tr, kvw), lambda i, j: (i, j, v_blk)),
                  tab, tab, tab],
        out_specs=[pl.BlockSpec((1, tr, d), lambda i, j: (i, j, 0))] + [kv_out] * 4,
        out_shape=[jax.ShapeDtypeStruct((b, t, d), BF16),
                   jax.ShapeDtypeStruct((b, t, kvw), F32),
                   jax.ShapeDtypeStruct((b, t, kvw), F32),
                   jax.ShapeDtypeStruct((b, t, kvw), BF16),
                   jax.ShapeDtypeStruct((b, t, kvw), BF16)],
        compiler_params=_params("arbitrary", "arbitrary"),
        name="rope_kv",
    )(proj, proj, proj, cos, sup, sdn)


def _ssd_kernel(*refs, tl, has_state):
    (alog_ref, alr_ref, dsk_ref, mnw_ref, cwx_ref, cwb_ref, cwc_ref, cbx_ref, cbb_ref, cbc_ref,
     z_ref, x_ref, bm_ref, cm_ref, dt_ref, dtr_ref) = refs[:16]
    if has_state:
        csx_ref, csb_ref, csc_ref, s0_ref = refs[16:20]
        rest = refs[20:]
    else:
        rest = refs[16:]
    y_ref, sout_ref, xbuf, bbuf, cbuf, st_scr = rest
    g = pl.program_id(1)
    c = pl.program_id(2)
    L = CHUNK
    nsub = tl // L
    halo = M_CONV - 1
    base = 8

    @pl.when(c == 0)
    def _():
        if has_state:
            xbuf[base - halo:base, :] = csx_ref[0]
            bbuf[base - halo:base, :] = csb_ref[0]
            cbuf[base - halo:base, :] = csc_ref[0]
            st_scr[...] = s0_ref[0].reshape(GROUP_W, M_STATE).T
        else:
            xbuf[0:base, :] = jnp.zeros((base, GROUP_W), F32)
            bbuf[0:base, :] = jnp.zeros((base, M_STATE), F32)
            cbuf[0:base, :] = jnp.zeros((base, M_STATE), F32)
            st_scr[...] = jnp.zeros_like(st_scr)

    xbuf[base:base + tl, :] = x_ref[0]
    bbuf[base:base + tl, :] = bm_ref[0]
    cbuf[base:base + tl, :] = cm_ref[0]

    r64 = lax.broadcasted_iota(jnp.int32, (L, L), 0)
    c64 = lax.broadcasted_iota(jnp.int32, (L, L), 1)
    tril = (c64 <= r64).astype(BF16)
    heads = dt_ref.shape[2]
    hrow = lax.broadcasted_iota(jnp.int32, (heads, GROUP_W), 0)
    hcol = lax.broadcasted_iota(jnp.int32, (heads, GROUP_W), 1)
    expand = (hrow == g * M_HPG + hcol // M_HEAD_DIM).astype(BF16)
    br = lax.broadcasted_iota(jnp.int32, (LANES, LANES), 0)
    bc = lax.broadcasted_iota(jnp.int32, (LANES, LANES), 1)
    same_half = (br // L) == (bc // L)
    triu2 = (same_half & ((br % L) <= (bc % L))).astype(BF16)
    pr = lax.broadcasted_iota(jnp.int32, (L, LANES), 0)
    pc = lax.broadcasted_iota(jnp.int32, (L, LANES), 1)
    causal2 = (pc % L) <= pr

    a_col = -jnp.exp(alog_ref[...])
    a_row = -jnp.exp(alr_ref[0])
    n_rows = dtr_ref.shape[2]
    da_r = dtr_ref[0, 0] * jnp.concatenate([a_row] * (n_rows // 8), axis=0)
    hi, lo = _split_hi_lo(da_r)
    acum_r_all = _dot(hi, triu2) + _dot(lo, triu2)

    def conv(buf, w_ref, b_ref, i):
        acc = b_ref[...] + w_ref[M_CONV - 1:M_CONV, :] * buf[base + i * L:base + (i + 1) * L, :]
        for k in range(M_CONV - 1):
            off = base - halo + k + i * L
            acc = acc + w_ref[k:k + 1, :] * buf[off:off + L, :]
        return _silu(acc)

    for i in range(nsub):
        rows = slice(i * L, (i + 1) * L)
        x = conv(xbuf, cwx_ref, cbx_ref, i)
        bm = conv(bbuf, cwb_ref, cbb_ref, i).astype(BF16)
        cm = conv(cbuf, cwc_ref, cbc_ref, i).astype(BF16)
        dt = dt_ref[0, rows, :]
        da = dt * a_col
        hi, lo = _split_hi_lo(da)
        acum = _dot(tril, hi) + _dot(tril, lo)
        hi, lo = _split_hi_lo(acum)
        acum_x = _dot(hi, expand) + _dot(lo, expand)
        hi, lo = _split_hi_lo(dt)
        dt_x = _dot(hi, expand) + _dot(lo, expand)
        xdt = x * dt_x
        xdt_b = xdt.astype(BF16)
        alast_x = acum_x[L - 1:L, :]
        cb2 = _dot_nt(cm, jnp.concatenate([bm, bm], axis=0))
        st_b = st_scr[...].astype(BF16)
        y_off = _dot(cm, st_b) * jnp.exp(acum_x)
        y_parts = []
        for jj in range(GROUP_W // LANES):
            ls = slice(jj * LANES, (jj + 1) * LANES)
            seg = acum_x[:, ls] - acum_r_all[i * 8 + jj:i * 8 + jj + 1, :]
            m2 = (jnp.where(causal2, jnp.exp(seg), 0.0) * cb2).astype(BF16)
            x2 = xdt_b[:, ls]
            rhs = jnp.where(same_half, jnp.concatenate([x2, x2], axis=0), jnp.zeros((), BF16))
            y_parts.append(_dot(m2, rhs))
        y = jnp.concatenate(y_parts, axis=1) + y_off + dsk_ref[0] * x
        y = y * _silu(z_ref[0, rows, :])
        ms = jnp.mean(y * y, axis=-1, keepdims=True)
        y_ref[0, rows, :] = (y * lax.rsqrt(ms + EPS) * mnw_ref[0]).astype(y_ref.dtype)
        xw = (xdt * jnp.exp(alast_x - acum_x)).astype(BF16)
        st_scr[...] = st_scr[...] * jnp.exp(alast_x) + _dot_tn(bm, xw)

    tx = xbuf[base + tl - halo:base + tl, :]
    tb = bbuf[base + tl - halo:base + tl, :]
    tc = cbuf[base + tl - halo:base + tl, :]
    xbuf[base - halo:base, :] = tx
    bbuf[base - halo:base, :] = tb
    cbuf[base - halo:base, :] = tc

    @pl.when(c == pl.num_programs(2) - 1)
    def _():
        sout_ref[0] = st_scr[...].T.reshape(M_HPG, M_HEAD_DIM, M_STATE)


def _ssd(proj, dt, conv_state, ssm_state, conv_w, conv_b, a_log, d_skip, mnorm_w, z_off, xbc_off, d_inner):
    b, t, _ = proj.shape
    groups = d_inner // GROUP_W
    heads = groups * M_HPG
    bcw = groups * M_STATE
    has_state = conv_state is not None
    tl = _pick(t, 1024, CHUNK)
    nsub = tl // CHUNK
    nc = t // tl
    dtr = dt.reshape(b, t // CHUNK, CHUNK, groups, M_HPG // 2, 2).transpose(0, 3, 1, 4, 5, 2)
    dtr = dtr.reshape(b, groups, (t // CHUNK) * 8, LANES)
    n_rows = nsub * 8
    if n_rows < 16:
        dtr = jnp.concatenate([dtr, jnp.zeros_like(dtr)], axis=2)
        n_rows = 16
    alr = jnp.repeat(a_log.reshape(groups, M_HPG // 2, 2), CHUNK, axis=2)
    dsk = jnp.repeat(d_skip.reshape(groups, 1, M_HPG), M_HEAD_DIM, axis=2)
    mnw = mnorm_w.reshape(groups, 1, GROUP_W)
    xb, bb, cb = xbc_off // GROUP_W, (xbc_off + d_inner) // M_STATE, (xbc_off + d_inner + bcw) // M_STATE
    cxb, cbb, ccb = 0, d_inner // M_STATE, (d_inner + bcw) // M_STATE
    zb = z_off // GROUP_W
    conv_b2 = conv_b.reshape(1, -1)
    gmap = lambda blk: (lambda i, g, c: (0, blk + g))
    in_specs = [
        pl.BlockSpec((1, heads), lambda i, g, c: (0, 0)),
        pl.BlockSpec((1, 8, LANES), lambda i, g, c: (g, 0, 0)),
        pl.BlockSpec((1, 1, GROUP_W), lambda i, g, c: (g, 0, 0)),
        pl.BlockSpec((1, 1, GROUP_W), lambda i, g, c: (g, 0, 0)),
        pl.BlockSpec((M_CONV, GROUP_W), gmap(cxb)),
        pl.BlockSpec((M_CONV, M_STATE), gmap(cbb)),
        pl.BlockSpec((M_CONV, M_STATE), gmap(ccb)),
        pl.BlockSpec((1, GROUP_W), gmap(cxb)),
        pl.BlockSpec((1, M_STATE), gmap(cbb)),
        pl.BlockSpec((1, M_STATE), gmap(ccb)),
        pl.BlockSpec((1, tl, GROUP_W), lambda i, g, c: (i, c, zb + g)),
        pl.BlockSpec((1, tl, GROUP_W), lambda i, g, c: (i, c, xb + g)),
        pl.BlockSpec((1, tl, M_STATE), lambda i, g, c: (i, c, bb + g)),
        pl.BlockSpec((1, tl, M_STATE), lambda i, g, c: (i, c, cb + g)),
        pl.BlockSpec((1, tl, heads), lambda i, g, c: (i, c, 0)),
        pl.BlockSpec((1, 1, n_rows, LANES), lambda i, g, c: (i, g, c, 0)),
    ]
    args = [a_log.reshape(1, heads), alr, dsk, mnw, conv_w, conv_w, conv_w, conv_b2, conv_b2, conv_b2,
            proj, proj, proj, proj, dt.reshape(b, t, heads), dtr]
    if has_state:
        halo = M_CONV - 1
        in_specs += [
            pl.BlockSpec((1, halo, GROUP_W), lambda i, g, c: (i, 0, cxb + g)),
            pl.BlockSpec((1, halo, M_STATE), lambda i, g, c: (i, 0, cbb + g)),
            pl.BlockSpec((1, halo, M_STATE), lambda i, g, c: (i, 0, ccb + g)),
            pl.BlockSpec((1, M_HPG, M_HEAD_DIM, M_STATE), lambda i, g, c: (i, g, 0, 0)),
        ]
        args += [conv_state, conv_state, conv_state, ssm_state]
    return pl.pallas_call(
        functools.partial(_ssd_kernel, tl=tl, has_state=has_state),
        grid=(b, groups, nc),
        in_specs=in_specs,
        out_specs=[pl.BlockSpec((1, tl, GROUP_W), lambda i, g, c: (i, c, g)),
                   pl.BlockSpec((1, M_HPG, M_HEAD_DIM, M_STATE), lambda i, g, c: (i, g, 0, 0))],
        out_shape=[jax.ShapeDtypeStruct((b, t, d_inner), BF16),
                   jax.ShapeDtypeStruct((b, heads, M_HEAD_DIM, M_STATE), F32)],
        scratch_shapes=[pltpu.VMEM((tl + 8, GROUP_W), F32),
                        pltpu.VMEM((tl + 8, M_STATE), F32),
                        pltpu.VMEM((tl + 8, M_STATE), F32),
                        pltpu.VMEM((M_STATE, GROUP_W), F32)],
        compiler_params=_params("arbitrary", "arbitrary", "arbitrary"),
        name="ssd",
    )(*args)


def _attn_kernel(*refs, tq, tk, rc, n_qt, past, tkp, pos0, lam_init):
    lq1_ref, lk1_ref, lq2_ref, lk2_ref, nw_ref, q_ref, k_ref, v_ref, za_ref = refs[:9]
    if past:
        kp_ref, vp_ref = refs[9:11]
        rest = refs[11:]
    else:
        rest = refs[9:]
    o_ref, qz_scr, s_scr, p_scr, m_scr, l_scr, acc_scr = rest
    qi = pl.program_id(2)
    rows = A_REP * tq

    lane = lax.broadcasted_iota(jnp.int32, (tq, KV_W), 1)
    zero = jnp.zeros((), BF16)
    for r in range(A_REP):
        q = q_ref[0, :, r * KV_W:(r + 1) * KV_W]
        qz_scr[0, r * tq:(r + 1) * tq, :] = jnp.where(lane < A_HEAD_DIM, q, zero)
        qz_scr[1, r * tq:(r + 1) * tq, :] = jnp.where(lane >= A_HEAD_DIM, q, zero)
    m_scr[...] = jnp.full(m_scr.shape, -jnp.inf, F32)
    l_scr[...] = jnp.zeros(l_scr.shape, F32)
    acc_scr[...] = jnp.zeros(acc_scr.shape, F32)

    def tile(k, v, nkeys, mask_fn):
        reps = nkeys // LANES
        for comp in range(2):
            s_scr[comp, :, :nkeys] = _dot_nt(qz_scr[comp], k)
            for i in range(rows // rc):
                rs = slice(i * rc, (i + 1) * rc)
                s = s_scr[comp, rs, :nkeys]
                if mask_fn is not None:
                    s = jnp.where(mask_fn(i), s, -jnp.inf)
                m_old = m_scr[comp, rs, :]
                m_new = jnp.maximum(m_old, jnp.max(s, axis=-1, keepdims=True))
                p = jnp.exp2(s - jnp.concatenate([m_new] * reps, axis=1))
                alpha = jnp.exp2(m_old - m_new)
                psum = p[:, :LANES]
                for u in range(1, reps):
                    psum = psum + p[:, u * LANES:(u + 1) * LANES]
                l_scr[comp, rs, :] = alpha * l_scr[comp, rs, :] + psum
                acc_scr[comp, rs, :] = alpha * acc_scr[comp, rs, :]
                m_scr[comp, rs, :] = m_new
                p_scr[comp, rs, :nkeys] = p.astype(BF16)
            acc_scr[comp] = acc_scr[comp] + _dot(p_scr[comp, :, :nkeys], v)

    def diag_mask(i, nkeys, n_valid, k_pos0):
        t0 = (i * rc) % tq
        qpos = pos0 + qi * tq + t0 + lax.broadcasted_iota(jnp.int32, (rc, nkeys), 0)
        kidx = lax.broadcasted_iota(jnp.int32, (rc, nkeys), 1)
        ok = ((k_pos0 + kidx) // CHUNK) <= (qpos // CHUNK)
        if n_valid < nkeys:
            ok = ok & (kidx < n_valid)
        return ok

    if past:
        def pbody(j, carry):
            sl = pl.ds(pl.multiple_of(j * tkp, tkp), tkp)
            tile(kp_ref[0, sl, :].astype(BF16), vp_ref[0, sl, :].astype(BF16), tkp, None)
            return carry

        lax.fori_loop(0, past // tkp, pbody, 0)

    def full_tile(j):
        sl = pl.ds(pl.multiple_of(j * tk, tk), tk)
        tile(k_ref[0, sl, :], v_ref[0, sl, :], tk, None)

    def pair_body(j2, carry):
        full_tile(2 * j2)
        full_tile(2 * j2 + 1)
        return carry

    if n_qt > 1:
        n_full = (qi * tq) // tk
        lax.fori_loop(0, n_full // 2, pair_body, 0)
        pl.when(n_full % 2 == 1)(lambda: full_tile(n_full - 1))
    tkd = max(tk, LANES)
    for d in range(tq // tk):
        k0 = qi * tq + d * tk
        sl = pl.ds(pl.multiple_of(k0, tk), tk)
        k, v = k_ref[0, sl, :], v_ref[0, sl, :]
        if tkd > tk:
            pad = jnp.zeros((tkd - tk, KV_W), BF16)
            k, v = jnp.concatenate([k, pad], axis=0), jnp.concatenate([v, pad], axis=0)
        tile(k, v, tkd, functools.partial(diag_mask, nkeys=tkd, n_valid=tk, k_pos0=pos0 + k0))

    lam = (jnp.exp(jnp.sum(lq1_ref[...] * lk1_ref[...], axis=-1, keepdims=True))
           - jnp.exp(jnp.sum(lq2_ref[...] * lk2_ref[...], axis=-1, keepdims=True)) + lam_init)
    l0 = jnp.sum(l_scr[0], axis=-1, keepdims=True)
    l1 = jnp.sum(l_scr[1], axis=-1, keepdims=True)
    o = acc_scr[0] / l0 - lam * (acc_scr[1] / l1)
    ms = jnp.mean(o * o, axis=-1, keepdims=True)
    on = o * lax.rsqrt(ms + EPS) * nw_ref[...] * (1.0 - lam_init)
    for r in range(A_REP):
        sl = slice(r * KV_W, (r + 1) * KV_W)
        o_ref[0, :, sl] = (on[r * tq:(r + 1) * tq, :] * _silu(za_ref[0, :, sl])).astype(o_ref.dtype)


def _attention(qr, kb, vb, proj, za_off, k_past, v_past, lam_vecs, norm_w, lam_init, pos0):
    b, t, d = qr.shape
    kvh = d // QH_W
    past = 0 if k_past is None else k_past.shape[1]
    tq = _pick(t, 512, CHUNK)
    tk = tq
    tkp = _pick(past, 2048, LANES) if past else 0
    rc = min(64, tq)
    zb = za_off // QH_W
    rows = A_REP * tq
    smax = max(tk, LANES, tkp)
    vec = pl.BlockSpec((1, A_HEAD_DIM), lambda i, h, j: (0, 0))
    in_specs = [vec, vec, vec, vec,
                pl.BlockSpec((1, KV_W), lambda i, h, j: (0, 0)),
                pl.BlockSpec((1, tq, QH_W), lambda i, h, j: (i, j, h)),
                pl.BlockSpec((1, t, KV_W), lambda i, h, j: (i, 0, h)),
                pl.BlockSpec((1, t, KV_W), lambda i, h, j: (i, 0, h)),
                pl.BlockSpec((1, tq, QH_W), lambda i, h, j: (i, j, zb + h))]
    args = [v.reshape(1, A_HEAD_DIM) for v in lam_vecs] + [norm_w.reshape(1, KV_W), qr, kb, vb, proj]
    if past:
        in_specs += [pl.BlockSpec((1, past, KV_W), lambda i, h, j: (i, 0, h)),
                     pl.BlockSpec((1, past, KV_W), lambda i, h, j: (i, 0, h))]
        args += [k_past, v_past]
    return pl.pallas_call(
        functools.partial(_attn_kernel, tq=tq, tk=tk, rc=rc, n_qt=t // tq, past=past, tkp=tkp, pos0=pos0,
                          lam_init=lam_init),
        grid=(b, kvh, t // tq),
        in_specs=in_specs,
        out_specs=pl.BlockSpec((1, tq, QH_W), lambda i, h, j: (i, j, h)),
        out_shape=jax.ShapeDtypeStruct((b, t, d), BF16),
        scratch_shapes=[pltpu.VMEM((2, rows, KV_W), BF16),
                        pltpu.VMEM((2, rows, smax), F32),
                        pltpu.VMEM((2, rows, smax), BF16),
                        pltpu.VMEM((2, rows, LANES), F32),
                        pltpu.VMEM((2, rows, LANES), F32),
                        pltpu.VMEM((2, rows, KV_W), F32)],
        compiler_params=_params("arbitrary", "arbitrary", "arbitrary"),
        name="diff_attn",
    )(*args)


def _merge_kernel(ym_ref, ya_ref, wm_ref, wa_ref, gm_ref, ga_ref, o_ref):
    pm = _dot(ym_ref[...], wm_ref[...])
    pa = _dot(ya_ref[...], wa_ref[...])
    o_ref[...] = (jax.nn.sigmoid(gm_ref[...]) * pm + jax.nn.sigmoid(ga_ref[...]) * pa).astype(o_ref.dtype)


def _merge(y_m, y_a, w_pm, w_pa, proj2d, gm_off, ga_off):
    m, km = y_m.shape
    ka = y_a.shape[1]
    d = w_pm.shape[1]
    tn = _pick(d, 512, LANES)
    gmb, gab = gm_off // tn, ga_off // tn
    many_rows = m >= 16 * 512
    tm = _pick(m, 512 if many_rows else 256)
    once = pl.Buffered(1) if many_rows else None
    return pl.pallas_call(
        _merge_kernel,
        grid=(d // tn, m // tm),
        in_specs=[pl.BlockSpec((tm, km), lambda j, i: (i, 0)),
                  pl.BlockSpec((tm, ka), lambda j, i: (i, 0)),
                  pl.BlockSpec((km, tn), lambda j, i: (0, j), pipeline_mode=once),
                  pl.BlockSpec((ka, tn), lambda j, i: (0, j), pipeline_mode=once),
                  pl.BlockSpec((tm, tn), lambda j, i: (i, gmb + j)),
                  pl.BlockSpec((tm, tn), lambda j, i: (i, gab + j))],
        out_specs=pl.BlockSpec((tm, tn), lambda j, i: (i, j)),
        out_shape=jax.ShapeDtypeStruct((m, d), BF16),
        compiler_params=_params("arbitrary", "arbitrary"),
        name="merge",
    )(y_m, y_a, w_pm, w_pa, proj2d, proj2d)


def _out_kernel(mg_ref, w_ref, x_ref, gate_ref, fw_ref, o_ref, *, tn):
    j = pl.program_id(2)
    cols = pl.ds(pl.multiple_of(j * tn, tn), tn)
    o_ref[0, :, cols] = x_ref[0] + gate_ref[0] * _dot(mg_ref[0], w_ref[...])

    @pl.when(j == pl.num_programs(2) - 1)
    def _():
        r = o_ref[0]
        ms = jnp.mean(r * r, axis=-1, keepdims=True)
        o_ref[0] = r * lax.rsqrt(ms + EPS) * fw_ref[...]


def _out_proj(merged, w_out, x, gate, final_w):
    shape = x.shape
    b, t, d = shape
    gate = gate.reshape(b, 1, d)
    per_row_gate = t < 512 and b > 1
    if per_row_gate:
        gate = jnp.broadcast_to(gate, (b, t, d)).reshape(1, b * t, d)
        x = x.reshape(1, b * t, d)
        b, t = 1, b * t
    tm = _pick(t, 512)
    tn = _pick(d, 512, LANES)
    gate_spec =(pl.BlockSpec((1, tm, tn), lambda i, r, j: (i, r, j)) if per_row_gate else
                 pl.BlockSpec((1, 1, tn), lambda i, r, j: (i, 0, j)))
    out = pl.pallas_call(
        functools.partial(_out_kernel, tn=tn),
        grid=(b, t // tm, d // tn),
        in_specs=[pl.BlockSpec((1, tm, d), lambda i, r, j: (i, r, 0)),
                  pl.BlockSpec((d, tn), lambda i, r, j: (0, j)),
                  pl.BlockSpec((1, tm, tn), lambda i, r, j: (i, r, j)),
                  gate_spec,
                  pl.BlockSpec((1, d), lambda i, r, j: (0, 0))],
        out_specs=pl.BlockSpec((1, tm, d), lambda i, r, j: (i, r, 0)),
        out_shape=jax.ShapeDtypeStruct((b, t, d), F32),
        compiler_params=_params("arbitrary", "arbitrary", "arbitrary"),
        name="out_proj",
    )(merged.reshape(b, t, d), w_out, x, gate, final_w.reshape(1, d))
    return out.reshape(shape)


def _layer_path(x, mod, k_past, v_past, conv_state, ssm_state, pos0, w, lam_init, final_w):
    b, t, d = x.shape
    d_inner = 2 * d
    groups = d_inner // GROUP_W
    bcw = groups * M_STATE
    conv_dim = d_inner + 2 * bcw
    kvw = (d // QH_W) * KV_W
    z_off, xbc_off = 0, d_inner
    k_off, v_off, za_off, gm_off, ga_off = d, d + kvw, d + 2 * kvw, 2 * d + 2 * kvw, 3 * d + 2 * kvw
    assert k_off % kvw == 0 and za_off % QH_W == 0, "consumer column blocks must be block-aligned"
    shift, scale, gate = mod[:, :d], mod[:, d:2 * d], mod[:, 2 * d:]

    h = _prenorm(x, w['norm_w'], scale, shift).reshape(b * t, d)
    proj_m = _matmul_f32w(h, w['w_in'], w['layer'], 0, xbc_off + conv_dim, "in_proj_m").reshape(b, t, -1)
    proj_a = _matmul_f32w(h, w['w_in'], w['layer'], w['a_col0'], ga_off + d, "in_proj_a").reshape(b, t, -1)
    dt = _dt_proj(h, w['w_in'], w['layer'], w['dt_col0'], w['dt_bias']).reshape(b, t, -1)

    pos = pos0 + jnp.arange(t, dtype=jnp.int32)
    qr, k_new, v_new, kb, vb = _rope_kv(proj_a, pos, d, kvw, k_off // kvw, v_off // kvw)

    y_m, ssm_new = _ssd(proj_m, dt, conv_state, ssm_state, w['conv_w'], w['conv_b'], w['a_log'], w['d_skip'],
                        w['mamba_norm_w'], z_off, xbc_off, d_inner)
    halo = M_CONV - 1
    conv_new = proj_m[:, t - halo:, xbc_off:xbc_off + conv_dim]

    y_a = _attention(qr, kb, vb, proj_a, za_off, k_past, v_past,
                     (w['lam_q1'], w['lam_k1'], w['lam_q2'], w['lam_k2']), w['attn_norm_w'], lam_init, pos0)

    merged = _merge(y_m.reshape(b * t, d_inner), y_a.reshape(b * t, d), w['w_proj_m'], w['w_proj_a'],
                    proj_a.reshape(b * t, -1), gm_off, ga_off)
    y = _out_proj(merged, w['w_out'], x, gate, final_w)
    kvh = d // QH_W
    return (y, k_new.reshape(b, t, kvh, 2, A_HEAD_DIM), v_new.reshape(b, t, kvh, KV_W), conv_new, ssm_new)


def kernel(x_prompt, x_sample, cache_k, cache_v, state_conv, state_ssm, c_prompt, c_sample,
           w_ada, b_ada, norm_w, w_in, conv_w, conv_b, dt_bias, a_log, d_skip, mamba_norm_w,
           lam_q1, lam_k1, lam_q2, lam_k2, attn_norm_w, w_proj_m, w_proj_a, w_out, final_norm_w):
    depth = w_in.shape[0]
    assert depth == 1, "the final norm is fused into the single layer's output projection"
    bp, d = c_prompt.shape
    bs = c_sample.shape[0]
    past = cache_k.shape[2]
    d_inner = 2 * d
    groups = d_inner // GROUP_W
    heads = groups * M_HPG
    conv_dim = d_inner + 2 * groups * M_STATE
    kvw = (d // QH_W) * KV_W
    sizes = (d_inner, conv_dim, heads, d, kvw, kvw, d, d, d)
    offs = [0]
    for s in sizes:
        offs.append(offs[-1] + s)

    i = 0
    wi = w_in[i]
    w = {
        'w_in': w_in, 'layer': i, 'dt_col0': offs[2], 'a_col0': offs[3],
        'norm_w': norm_w[i], 'conv_w': conv_w[i], 'conv_b': conv_b[i], 'dt_bias': dt_bias[i], 'a_log': a_log[i],
        'd_skip': d_skip[i], 'mamba_norm_w': mamba_norm_w[i], 'lam_q1': lam_q1[i], 'lam_k1': lam_k1[i],
        'lam_q2': lam_q2[i], 'lam_k2': lam_k2[i], 'attn_norm_w': attn_norm_w[i],
        'w_proj_m': w_proj_m[i].astype(BF16), 'w_proj_a': w_proj_a[i].astype(BF16), 'w_out': w_out[i].astype(BF16),
    }
    lam_init = 0.8 - 0.6 * math.exp(-0.3 * i)
    pad_rows = -(bp + bs) % 16
    c_all = jnp.concatenate([c_prompt, c_sample, jnp.zeros((pad_rows, d), F32)], axis=0)
    mod = _ada_mod(c_all, w_ada[i], b_ada[i])

    yp, kp, vp, cp, sp = _layer_path(x_prompt, mod[:bp], None, None, None, None, 0, w, lam_init, final_norm_w)
    ck = cache_k[i].reshape(bs, past, kvw)
    cv = cache_v[i].reshape(bs, past, kvw)
    ys, kq, vq, cq, sq = _layer_path(x_sample, mod[bp:bp + bs], ck, cv, state_conv[i], state_ssm[i], past, w, lam_init,
                                     final_norm_w)
    st = lambda a: a[None]
    return (yp, ys, st(kp), st(vp), st(cp), st(sp), st(kq), st(vq), st(cq), st(sq))
```

```python
import functools
import math

import jax
import jax.numpy as jnp
from jax import lax
from jax.experimental import pallas as pl
from jax.experimental.pallas import tpu as pltpu

F32 = jnp.float32
BF16 = jnp.bfloat16

CHUNK = 64
EPS = 1e-6
M_HEAD_DIM = 64
M_HPG = 16
M_STATE = 128
M_CONV = 4
GROUP_W = M_HPG * M_HEAD_DIM
A_HEAD_DIM = 64
A_REP = 4
KV_W = 2 * A_HEAD_DIM
QH_W = A_REP * KV_W
ROPE_DIM = 16
ROPE_THETA = 500000.0
LANES = 128
VMEM_LIMIT_BYTES = 56 * 1024 * 1024


def _pick(n, target, mult=8):
    if n <= target:
        return n
    for t in range(target, 0, -1):
        if n % t == 0 and t % mult == 0:
            return t
    return n


def _params(*sem):
    return pltpu.CompilerParams(dimension_semantics=sem, vmem_limit_bytes=VMEM_LIMIT_BYTES)


def _silu(x):
    return x * jax.nn.sigmoid(x)


def _split_hi_lo(x):
    hi = x.astype(BF16)
    lo = (x - hi.astype(F32)).astype(BF16)
    return hi, lo


def _dot(a, b):
    return jnp.dot(a, b, preferred_element_type=F32)


def _dot_nt(a, b):
    return lax.dot_general(a, b, (((1,), (1,)), ((), ())), preferred_element_type=F32)


def _dot_tn(a, b):
    return lax.dot_general(a, b, (((0,), (0,)), ((), ())), preferred_element_type=F32)


def _ada_kernel(c_ref, w_ref, b_ref, o_ref):
    a = _silu(c_ref[...]).astype(BF16)
    o_ref[...] = _dot(a, w_ref[...].astype(BF16)) + b_ref[...]


def _ada_mod(c, w_ada, b_ada):
    m, d = c.shape
    n = w_ada.shape[1]
    tn = _pick(n, 512, LANES)
    return pl.pallas_call(
        _ada_kernel,
        grid=(n // tn,),
        in_specs=[pl.BlockSpec((m, d), lambda j: (0, 0)),
                  pl.BlockSpec((d, tn), lambda j: (0, j)),
                  pl.BlockSpec((1, tn), lambda j: (0, j))],
        out_specs=pl.BlockSpec((m, tn), lambda j: (0, j)),
        out_shape=jax.ShapeDtypeStruct((m, n), F32),
        compiler_params=_params("arbitrary"),
        name="ada_mod",
    )(c, w_ada, b_ada.reshape(1, n))


def _prenorm_kernel(x_ref, nw_ref, sc_ref, sh_ref, o_ref):
    x = x_ref[0]
    ms = jnp.mean(x * x, axis=-1, keepdims=True)
    y = x * lax.rsqrt(ms + EPS) * nw_ref[...]
    o_ref[0] = (y * (1.0 + sc_ref[0]) + sh_ref[0]).astype(o_ref.dtype)


def _prenorm(x, norm_w, scale, shift):
    b, t, d = x.shape
    tr = _pick(t, 256)
    return pl.pallas_call(
        _prenorm_kernel,
        grid=(b, t // tr),
        in_specs=[pl.BlockSpec((1, tr, d), lambda i, j: (i, j, 0)),
                  pl.BlockSpec((1, d), lambda i, j: (0, 0)),
                  pl.BlockSpec((1, 1, d), lambda i, j: (i, 0, 0)),
                  pl.BlockSpec((1, 1, d), lambda i, j: (i, 0, 0))],
        out_specs=pl.BlockSpec((1, tr, d), lambda i, j: (i, j, 0)),
        out_shape=jax.ShapeDtypeStruct((b, t, d), BF16),
        compiler_params=_params("arbitrary", "arbitrary"),
        name="prenorm",
    )(x, norm_w.reshape(1, d), scale.reshape(b, 1, d), shift.reshape(b, 1, d))


def _mm_kernel(a_ref, w_ref, o_ref):
    o_ref[...] = _dot(a_ref[...], w_ref[...]).astype(o_ref.dtype)


def _mm_castw_kernel(a_ref, w_ref, o_ref):
    o_ref[...] = _dot(a_ref[...], w_ref[0].astype(BF16))


def _matmul_f32w(a, w3, layer, col0, n, name):
    m, k = a.shape
    tm = _pick(m, 1024)
    tn = _pick(n, 512, LANES)
    if col0 % LANES:
        w3, col0 = w3[:, :, col0:col0 + n], 0
    return pl.pallas_call(
        _mm_castw_kernel,
        grid=(m // tm, n // tn),
        in_specs=[pl.BlockSpec((tm, k), lambda i, j: (i, 0)),
                  pl.BlockSpec((pl.Element(1), pl.Element(k), pl.Element(tn)),
                               lambda i, j: (layer, 0, pl.multiple_of(col0 + j * tn, LANES)))],
        out_specs=pl.BlockSpec((tm, tn), lambda i, j: (i, j)),
        out_shape=jax.ShapeDtypeStruct((m, n), F32),
        compiler_params=_params("arbitrary", "arbitrary"),
        name=name,
    )(a, w3)


def _matmul(a, w, name):
    m, k = a.shape
    n = w.shape[1]
    tm = _pick(m, 1024)
    tn = _pick(n, 1024, LANES)
    return pl.pallas_call(
        _mm_kernel,
        grid=(n // tn, m // tm),
        in_specs=[pl.BlockSpec((tm, k), lambda j, i: (i, 0)),
                  pl.BlockSpec((k, tn), lambda j, i: (0, j))],
        out_specs=pl.BlockSpec((tm, tn), lambda j, i: (i, j)),
        out_shape=jax.ShapeDtypeStruct((m, n), F32),
        compiler_params=_params("arbitrary", "arbitrary"),
        name=name,
    )(a, w)


def _dt_kernel(a_ref, w_ref, b_ref, o_ref, wb_scr):
    @pl.when(pl.program_id(0) == 0)
    def _():
        wb_scr[...] = w_ref[0].astype(BF16)

    x = _dot(a_ref[...], wb_scr[...]) + b_ref[...]
    o_ref[...] = jnp.maximum(x, 0.0) + jnp.log1p(jnp.exp(-jnp.abs(x)))


def _dt_proj(a, w3, layer, col0, dt_bias):
    m, k = a.shape
    n = dt_bias.shape[0]
    tm = _pick(m, 512)
    if col0 % LANES or n % LANES:
        w3, col0 = w3[:, :, col0:col0 + n], 0
    return pl.pallas_call(
        _dt_kernel,
        grid=(m // tm,),
        in_specs=[pl.BlockSpec((tm, k), lambda i: (i, 0)),
                  pl.BlockSpec((pl.Element(1), pl.Element(k), pl.Element(n)),
                               lambda i: (layer, 0, col0)),
                  pl.BlockSpec((1, n), lambda i: (0, 0))],
        out_specs=pl.BlockSpec((tm, n), lambda i: (i, 0)),
        out_shape=jax.ShapeDtypeStruct((m, n), F32),
        scratch_shapes=[pltpu.VMEM((k, n), BF16)],
        compiler_params=_params("arbitrary"),
        name="dt_proj",
    )(a, w3, dt_bias.reshape(1, n))


def _rope_slab(x, cos, sin_up, sin_dn):
    return x * cos + pltpu.roll(x, LANES - ROPE_DIM // 2, 1) * sin_up + pltpu.roll(x, ROPE_DIM // 2, 1) * sin_dn


def _rope_kernel(q_ref, k_ref, v_ref, cos_ref, sup_ref, sdn_ref, qo_ref, ko_ref, vo_ref, kb_ref, vb_ref, *, q_scale):
    cos, sup, sdn = cos_ref[...], sup_ref[...], sdn_ref[...]
    for s in range(q_ref.shape[2] // LANES):
        sl = slice(s * LANES, (s + 1) * LANES)
        qo_ref[0, :, sl] = (_rope_slab(q_ref[0, :, sl], cos, sup, sdn) * q_scale).astype(qo_ref.dtype)
    for s in range(k_ref.shape[2] // LANES):
        sl = slice(s * LANES, (s + 1) * LANES)
        k = _rope_slab(k_ref[0, :, sl], cos, sup, sdn)
        ko_ref[0, :, sl] = k
        kb_ref[0, :, sl] = k.astype(kb_ref.dtype)
    v = v_ref[...]
    vo_ref[...] = v
    vb_ref[...] = v.astype(vb_ref.dtype)


def _rope_tables(pos):
    half = ROPE_DIM // 2
    inv = ROPE_THETA ** (-jnp.arange(half, dtype=F32) / half)
    ang = pos.astype(F32)[:, None] * inv[None, :]
    cos, sin = jnp.cos(ang), jnp.sin(ang)
    t = pos.shape[0]
    ones = jnp.ones((t, A_HEAD_DIM - ROPE_DIM), F32)
    zeros = jnp.zeros((t, A_HEAD_DIM - ROPE_DIM), F32)
    zh = jnp.zeros((t, half), F32)
    cos64 = jnp.concatenate([cos, cos, ones], axis=1)
    sup64 = jnp.concatenate([-sin, zh, zeros], axis=1)
    sdn64 = jnp.concatenate([zh, sin, zeros], axis=1)
    rep = lambda a: jnp.concatenate([a, a], axis=1)
    return rep(cos64), rep(sup64), rep(sdn64)


def _rope_kv(proj, pos, d, kvw, k_blk, v_blk):
    b, t, _ = proj.shape
    tr = _pick(t, 256)
    cos, sup, sdn = _rope_tables(pos)
    tab = pl.BlockSpec((tr, LANES), lambda i, j: (j, 0))
    kv_out = pl.BlockSpec((1, tr, kvw), lambda i, j: (i, j, 0))
    return pl.pallas_call(
        functools.partial(_rope_kernel, q_scale=A_HEAD_DIM ** -0.5 * math.log2(math.e)),
        grid=(b, t // tr),
        in_specs=[pl.BlockSpec((1, tr, d), lambda i, j: (i, j, 0)),
                  pl.BlockSpec((1, tr, kvw), lambda i, j: (i, j, k_blk)),
                  pl.BlockSpec((1, tr, kvw), lambda i, j: (i, j, v_blk)),
                  tab, tab, tab],
        out_specs=[pl.BlockSpec((1, tr, d), lambda i, j: (i, j, 0))] + [kv_out] * 4,
        out_shape=[jax.ShapeDtypeStruct((b, t, d), BF16),
                   jax.ShapeDtypeStruct((b, t, kvw), F32),
                   jax.ShapeDtypeStruct((b, t, kvw), F32),
                   jax.ShapeDtypeStruct((b, t, kvw), BF16),
                   jax.ShapeDtypeStruct((b, t, kvw), BF16)],
        compiler_params=_params("arbitrary", "arbitrary"),
        name="rope_kv",
    )(proj, proj, proj, cos, sup, sdn)


def _ssd_kernel(*refs, tl, has_state):
    (alog_ref, alr_ref, dsk_ref, mnw_ref, cwx_ref, cwb_ref, cwc_ref, cbx_ref, cbb_ref, cbc_ref,
     z_ref, x_ref, bm_ref, cm_ref, dt_ref, dtr_ref) = refs[:16]
    if has_state:
        csx_ref, csb_ref, csc_ref, s0_ref = refs[16:20]
        rest = refs[20:]
    else:
        rest = refs[16:]
    y_ref, sout_ref, xbuf, bbuf, cbuf, st_scr = rest
    g = pl.program_id(1)
    c = pl.program_id(2)
    L = CHUNK
    nsub = tl // L
    halo = M_CONV - 1
    base = 8

    @pl.when(c == 0)
    def _():
        if has_state:
            xbuf[base - halo:base, :] = csx_ref[0]
            bbuf[base - halo:base, :] = csb_ref[0]
            cbuf[base - halo:base, :] = csc_ref[0]
            st_scr[...] = s0_ref[0].reshape(GROUP_W, M_STATE).T
        else:
            xbuf[0:base, :] = jnp.zeros((base, GROUP_W), F32)
            bbuf[0:base, :] = jnp.zeros((base, M_STATE), F32)
            cbuf[0:base, :] = jnp.zeros((base, M_STATE), F32)
            st_scr[...] = jnp.zeros_like(st_scr)

    xbuf[base:base + tl, :] = x_ref[0]
    bbuf[base:base + tl, :] = bm_ref[0]
    cbuf[base:base + tl, :] = cm_ref[0]

    r64 = lax.broadcasted_iota(jnp.int32, (L, L), 0)
    c64 = lax.broadcasted_iota(jnp.int32, (L, L), 1)
    tril = (c64 <= r64).astype(BF16)
    heads = dt_ref.shape[2]
    hrow = lax.broadcasted_iota(jnp.int32, (heads, GROUP_W), 0)
    hcol = lax.broadcasted_iota(jnp.int32, (heads, GROUP_W), 1)
    expand = (hrow == g * M_HPG + hcol // M_HEAD_DIM).astype(BF16)
    br = lax.broadcasted_iota(jnp.int32, (LANES, LANES), 0)
    bc = lax.broadcasted_iota(jnp.int32, (LANES, LANES), 1)
    same_half = (br // L) == (bc // L)
    triu2 = (same_half & ((br % L) <= (bc % L))).astype(BF16)
    pr = lax.broadcasted_iota(jnp.int32, (L, LANES), 0)
    pc = lax.broadcasted_iota(jnp.int32, (L, LANES), 1)
    causal2 = (pc % L) <= pr

    a_col = -jnp.exp(alog_ref[...])
    a_row = -jnp.exp(alr_ref[0])
    n_rows = dtr_ref.shape[2]
    da_r = dtr_ref[0, 0] * jnp.concatenate([a_row] * (n_rows // 8), axis=0)
    hi, lo = _split_hi_lo(da_r)
    acum_r_all = _dot(hi, triu2) + _dot(lo, triu2)

    def conv(buf, w_ref, b_ref, i):
        acc = b_ref[...] + w_ref[M_CONV - 1:M_CONV, :] * buf[base + i * L:base + (i + 1) * L, :]
        for k in range(M_CONV - 1):
            off = base - halo + k + i * L
            acc = acc + w_ref[k:k + 1, :] * buf[off:off + L, :]
        return _silu(acc)

    for i in range(nsub):
        rows = slice(i * L, (i + 1) * L)
        x = conv(xbuf, cwx_ref, cbx_ref, i)
        bm = conv(bbuf, cwb_ref, cbb_ref, i).astype(BF16)
        cm = conv(cbuf, cwc_ref, cbc_ref, i).astype(BF16)
        dt = dt_ref[0, rows, :]
        da = dt * a_col
        hi, lo = _split_hi_lo(da)
        acum = _dot(tril, hi) + _dot(tril, lo)
        hi, lo = _split_hi_lo(acum)
        acum_x = _dot(hi, expand) + _dot(lo, expand)
        hi, lo = _split_hi_lo(dt)
        dt_x = _dot(hi, expand) + _dot(lo, expand)
        xdt = x * dt_x
        xdt_b = xdt.astype(BF16)
        alast_x = acum_x[L - 1:L, :]
        cb2 = _dot_nt(cm, jnp.concatenate([bm, bm], axis=0))
        st_b = st_scr[...].astype(BF16)
        y_off = _dot(cm, st_b) * jnp.exp(acum_x)
        y_parts = []
        for jj in range(GROUP_W // LANES):
            ls = slice(jj * LANES, (jj + 1) * LANES)
            seg = acum_x[:, ls] - acum_r_all[i * 8 + jj:i * 8 + jj + 1, :]
            m2 = (jnp.where(causal2, jnp.exp(seg), 0.0) * cb2).astype(BF16)
            x2 = xdt_b[:, ls]
            rhs = jnp.where(same_half, jnp.concatenate([x2, x2], axis=0), jnp.zeros((), BF16))
            y_parts.append(_dot(m2, rhs))
        y = jnp.concatenate(y_parts, axis=1) + y_off + dsk_ref[0] * x
        y = y * _silu(z_ref[0, rows, :])
        ms = jnp.mean(y * y, axis=-1, keepdims=True)
        y_ref[0, rows, :] = (y * lax.rsqrt(ms + EPS) * mnw_ref[0]).astype(y_ref.dtype)
        xw = (xdt * jnp.exp(alast_x - acum_x)).astype(BF16)
        st_scr[...] = st_scr[...] * jnp.exp(alast_x) + _dot_tn(bm, xw)

    tx = xbuf[base + tl - halo:base + tl, :]
    tb = bbuf[base + tl - halo:base + tl, :]
    tc = cbuf[base + tl - halo:base + tl, :]
    xbuf[base - halo:base, :] = tx
    bbuf[base - halo:base, :] = tb
    cbuf[base - halo:base, :] = tc

    @pl.when(c == pl.num_programs(2) - 1)
    def _():
        sout_ref[0] = st_scr[...].T.reshape(M_HPG, M_HEAD_DIM, M_STATE)


def _ssd(proj, dt, conv_state, ssm_state, conv_w, conv_b, a_log, d_skip, mnorm_w, z_off, xbc_off, d_inner):
    b, t, _ = proj.shape
    groups = d_inner // GROUP_W
    heads = groups * M_HPG
    bcw = groups * M_STATE
    has_state = conv_state is not None
    tl = _pick(t, 1024, CHUNK)
    nsub = tl // CHUNK
    nc = t // tl
    dtr = dt.reshape(b, t // CHUNK, CHUNK, groups, M_HPG // 2, 2).transpose(0, 3, 1, 4, 5, 2)
    dtr = dtr.reshape(b, groups, (t // CHUNK) * 8, LANES)
    n_rows = nsub * 8
    if n_rows < 16:
        dtr = jnp.concatenate([dtr, jnp.zeros_like(dtr)], axis=2)
        n_rows = 16
    alr = jnp.repeat(a_log.reshape(groups, M_HPG // 2, 2), CHUNK, axis=2)
    dsk = jnp.repeat(d_skip.reshape(groups, 1, M_HPG), M_HEAD_DIM, axis=2)
    mnw = mnorm_w.reshape(groups, 1, GROUP_W)
    xb, bb, cb = xbc_off // GROUP_W, (xbc_off + d_inner) // M_STATE, (xbc_off + d_inner + bcw) // M_STATE
    cxb, cbb, ccb = 0, d_inner // M_STATE, (d_inner + bcw) // M_STATE
    zb = z_off // GROUP_W
    conv_b2 = conv_b.reshape(1, -1)
    gmap = lambda blk: (lambda i, g, c: (0, blk + g))
    in_specs = [
        pl.BlockSpec((1, heads), lambda i, g, c: (0, 0)),
        pl.BlockSpec((1, 8, LANES), lambda i, g, c: (g, 0, 0)),
        pl.BlockSpec((1, 1, GROUP_W), lambda i, g, c: (g, 0, 0)),
        pl.BlockSpec((1, 1, GROUP_W), lambda i, g, c: (g, 0, 0)),
        pl.BlockSpec((M_CONV, GROUP_W), gmap(cxb)),
        pl.BlockSpec((M_CONV, M_STATE), gmap(cbb)),
        pl.BlockSpec((M_CONV, M_STATE), gmap(ccb)),
        pl.BlockSpec((1, GROUP_W), gmap(cxb)),
        pl.BlockSpec((1, M_STATE), gmap(cbb)),
        pl.BlockSpec((1, M_STATE), gmap(ccb)),
        pl.BlockSpec((1, tl, GROUP_W), lambda i, g, c: (i, c, zb + g)),
        pl.BlockSpec((1, tl, GROUP_W), lambda i, g, c: (i, c, xb + g)),
        pl.BlockSpec((1, tl, M_STATE), lambda i, g, c: (i, c, bb + g)),
        pl.BlockSpec((1, tl, M_STATE), lambda i, g, c: (i, c, cb + g)),
        pl.BlockSpec((1, tl, heads), lambda i, g, c: (i, c, 0)),
        pl.BlockSpec((1, 1, n_rows, LANES), lambda i, g, c: (i, g, c, 0)),
    ]
    args = [a_log.reshape(1, heads), alr, dsk, mnw, conv_w, conv_w, conv_w, conv_b2, conv_b2, conv_b2,
            proj, proj, proj, proj, dt.reshape(b, t, heads), dtr]
    if has_state:
        halo = M_CONV - 1
        in_specs += [
            pl.BlockSpec((1, halo, GROUP_W), lambda i, g, c: (i, 0, cxb + g)),
            pl.BlockSpec((1, halo, M_STATE), lambda i, g, c: (i, 0, cbb + g)),
            pl.BlockSpec((1, halo, M_STATE), lambda i, g, c: (i, 0, ccb + g)),
            pl.BlockSpec((1, M_HPG, M_HEAD_DIM, M_STATE), lambda i, g, c: (i, g, 0, 0)),
        ]
        args += [conv_state, conv_state, conv_state, ssm_state]
    return pl.pallas_call(
        functools.partial(_ssd_kernel, tl=tl, has_state=has_state),
        grid=(b, groups, nc),
        in_specs=in_specs,
        out_specs=[pl.BlockSpec((1, tl, GROUP_W), lambda i, g, c: (i, c, g)),
                   pl.BlockSpec((1, M_HPG, M_HEAD_DIM, M_STATE), lambda i, g, c: (i, g, 0, 0))],
        out_shape=[jax.ShapeDtypeStruct((b, t, d_inner), BF16),
                   jax.ShapeDtypeStruct((b, heads, M_HEAD_DIM, M_STATE), F32)],
        scratch_shapes=[pltpu.VMEM((tl + 8, GROUP_W), F32),
                        pltpu.VMEM((tl + 8, M_STATE), F32),
                        pltpu.VMEM((tl + 8, M_STATE), F32),
                        pltpu.VMEM((M_STATE, GROUP_W), F32)],
        compiler_params=_params("arbitrary", "arbitrary", "arbitrary"),
        name="ssd",
    )(*args)


def _attn_kernel(*refs, tq, tk, rc, nh, n_qt, past, tkp, pos0, lam_init):
    lq1_ref, lk1_ref, lq2_ref, lk2_ref, nw_ref, q_ref, k_ref, v_ref, za_ref = refs[:9]
    if past:
        kp_ref, vp_ref = refs[9:11]
        rest = refs[11:]
    else:
        rest = refs[9:]
    o_ref, qz_scr, s_scr, p_scr, m_scr, l_scr, acc_scr = rest
    qi = pl.program_id(2)
    rows = A_REP * tq

    hq = tq // nh
    hrows = rows // nh
    row_blocks = [(half, r, half * hrows + r * hq) for half in range(nh) for r in range(A_REP)]
    lane = lax.broadcasted_iota(jnp.int32, (hq, KV_W), 1)
    zero = jnp.zeros((), BF16)
    for half, r, row0 in row_blocks:
        q = q_ref[0, half * hq:(half + 1) * hq, r * KV_W:(r + 1) * KV_W]
        qz_scr[0, row0:row0 + hq, :] = jnp.where(lane < A_HEAD_DIM, q, zero)
        qz_scr[1, row0:row0 + hq, :] = jnp.where(lane >= A_HEAD_DIM, q, zero)
    m_scr[...] = jnp.full(m_scr.shape, -jnp.inf, F32)
    l_scr[...] = jnp.zeros(l_scr.shape, F32)
    acc_scr[...] = jnp.zeros(acc_scr.shape, F32)

    def tile(k, v, nkeys, mask_fn, row_lo=0, row_hi=rows):
        reps = nkeys // LANES
        rr = slice(row_lo, row_hi)
        for comp in range(2):
            s_scr[comp, rr, :nkeys] = _dot_nt(qz_scr[comp, rr, :], k)
            for i in range(row_lo // rc, row_hi // rc):
                rs = slice(i * rc, (i + 1) * rc)
                s = s_scr[comp, rs, :nkeys]
                mask = None if mask_fn is None else mask_fn(i)
                if mask is not None:
                    s = jnp.where(mask, s, -jnp.inf)
                m_old = m_scr[comp, rs, :]
                m_new = jnp.maximum(m_old, jnp.max(s, axis=-1, keepdims=True))
                p = jnp.exp2(s - jnp.concatenate([m_new] * reps, axis=1))
                alpha = jnp.exp2(m_old - m_new)
                psum = p[:, :LANES]
                for u in range(1, reps):
                    psum = psum + p[:, u * LANES:(u + 1) * LANES]
                l_scr[comp, rs, :] = alpha * l_scr[comp, rs, :] + psum
                acc_scr[comp, rs, :] = alpha * acc_scr[comp, rs, :]
                m_scr[comp, rs, :] = m_new
                p_scr[comp, rs, :nkeys] = p.astype(BF16)
            acc_scr[comp, rr, :] = acc_scr[comp, rr, :] + _dot(p_scr[comp, rr, :nkeys], v)

    def diag_mask(i, nkeys, n_valid, k_pos0, masked_halves=(0, 1)):
        half, rem = divmod(i * rc, hrows)
        if half not in masked_halves:
            return None
        t0 = half * hq + rem % hq
        qpos = pos0 + qi * tq + t0 + lax.broadcasted_iota(jnp.int32, (rc, nkeys), 0)
        kidx = lax.broadcasted_iota(jnp.int32, (rc, nkeys), 1)
        ok = ((k_pos0 + kidx) // CHUNK) <= (qpos // CHUNK)
        if n_valid < nkeys:
            ok = ok & (kidx < n_valid)
        return ok

    if past:
        def pbody(j, carry):
            sl = pl.ds(pl.multiple_of(j * tkp, tkp), tkp)
            tile(kp_ref[0, sl, :].astype(BF16), vp_ref[0, sl, :].astype(BF16), tkp, None)
            return carry

        lax.fori_loop(0, past // tkp, pbody, 0)

    def full_tile(j):
        sl = pl.ds(pl.multiple_of(j * tk, tk), tk)
        tile(k_ref[0, sl, :], v_ref[0, sl, :], tk, None)

    def pair_body(j2, carry):
        full_tile(2 * j2)
        full_tile(2 * j2 + 1)
        return carry

    if n_qt > 1:
        n_full = (qi * tq) // tk
        lax.fori_loop(0, n_full // 2, pair_body, 0)
        pl.when(n_full % 2 == 1)(lambda: full_tile(n_full - 1))
    k0 = qi * tq
    if nh == 2:
        sl_a = pl.ds(pl.multiple_of(k0, hq), hq)
        sl_b = pl.ds(pl.multiple_of(k0 + hq, hq), hq)
        tile(k_ref[0, sl_a, :], v_ref[0, sl_a, :], hq,
             functools.partial(diag_mask, nkeys=hq, n_valid=hq, k_pos0=pos0 + k0, masked_halves=(0,)))
        tile(k_ref[0, sl_b, :], v_ref[0, sl_b, :], hq,
             functools.partial(diag_mask, nkeys=hq, n_valid=hq, k_pos0=pos0 + k0 + hq), row_lo=hrows)
    else:
        tkd = max(tq, LANES)
        sl = pl.ds(pl.multiple_of(k0, tq), tq)
        k, v = k_ref[0, sl, :], v_ref[0, sl, :]
        if tkd > tq:
            pad = jnp.zeros((tkd - tq, KV_W), BF16)
            k, v = jnp.concatenate([k, pad], axis=0), jnp.concatenate([v, pad], axis=0)
        tile(k, v, tkd, functools.partial(diag_mask, nkeys=tkd, n_valid=tq, k_pos0=pos0 + k0))

    lam = (jnp.exp(jnp.sum(lq1_ref[...] * lk1_ref[...], axis=-1, keepdims=True))
           - jnp.exp(jnp.sum(lq2_ref[...] * lk2_ref[...], axis=-1, keepdims=True)) + lam_init)
    l0 = jnp.sum(l_scr[0], axis=-1, keepdims=True)
    l1 = jnp.sum(l_scr[1], axis=-1, keepdims=True)
    o = acc_scr[0] / l0 - lam * (acc_scr[1] / l1)
    ms = jnp.mean(o * o, axis=-1, keepdims=True)
    on = o * lax.rsqrt(ms + EPS) * nw_ref[...] * (1.0 - lam_init)
    for half, r, row0 in row_blocks:
        ts, sl = slice(half * hq, (half + 1) * hq), slice(r * KV_W, (r + 1) * KV_W)
        o_ref[0, ts, sl] = (on[row0:row0 + hq, :] * _silu(za_ref[0, ts, sl])).astype(o_ref.dtype)


def _attention(qr, kb, vb, proj, za_off, k_past, v_past, lam_vecs, norm_w, lam_init, pos0):
    b, t, d = qr.shape
    kvh = d // QH_W
    past = 0 if k_past is None else k_past.shape[1]
    tq = _pick(t, 512, CHUNK)
    tk = tq
    tkp = _pick(past, 2048, LANES) if past else 0
    nh = 2 if tq % (2 * LANES) == 0 else 1
    rc = min(64, tq // nh)
    zb = za_off // QH_W
    rows = A_REP * tq
    smax = max(tk, LANES, tkp)
    vec = pl.BlockSpec((1, A_HEAD_DIM), lambda i, h, j: (0, 0))
    in_specs = [vec, vec, vec, vec,
                pl.BlockSpec((1, KV_W), lambda i, h, j: (0, 0)),
                pl.BlockSpec((1, tq, QH_W), lambda i, h, j: (i, j, h)),
                pl.BlockSpec((1, t, KV_W), lambda i, h, j: (i, 0, h)),
                pl.BlockSpec((1, t, KV_W), lambda i, h, j: (i, 0, h)),
                pl.BlockSpec((1, tq, QH_W), lambda i, h, j: (i, j, zb + h))]
    args = [v.reshape(1, A_HEAD_DIM) for v in lam_vecs] + [norm_w.reshape(1, KV_W), qr, kb, vb, proj]
    if past:
        in_specs += [pl.BlockSpec((1, past, KV_W), lambda i, h, j: (i, 0, h)),
                     pl.BlockSpec((1, past, KV_W), lambda i, h, j: (i, 0, h))]
        args += [k_past, v_past]
    return pl.pallas_call(
        functools.partial(_attn_kernel, tq=tq, tk=tk, rc=rc, nh=nh, n_qt=t // tq, past=past, tkp=tkp, pos0=pos0,
                          lam_init=lam_init),
        grid=(b, kvh, t // tq),
        in_specs=in_specs,
        out_specs=pl.BlockSpec((1, tq, QH_W), lambda i, h, j: (i, j, h)),
        out_shape=jax.ShapeDtypeStruct((b, t, d), BF16),
        scratch_shapes=[pltpu.VMEM((2, rows, KV_W), BF16),
                        pltpu.VMEM((2, rows, smax), F32),
                        pltpu.VMEM((2, rows, smax), BF16),
                        pltpu.VMEM((2, rows, LANES), F32),
                        pltpu.VMEM((2, rows, LANES), F32),
                        pltpu.VMEM((2, rows, KV_W), F32)],
        compiler_params=_params("arbitrary", "arbitrary", "arbitrary"),
        name="diff_attn",
    )(*args)


def _merge_kernel(ym_ref, ya_ref, wm_ref, wa_ref, gm_ref, ga_ref, o_ref):
    pm = _dot(ym_ref[...], wm_ref[...])
    pa = _dot(ya_ref[...], wa_ref[...])
    o_ref[...] = (jax.nn.sigmoid(gm_ref[...]) * pm + jax.nn.sigmoid(ga_ref[...]) * pa).astype(o_ref.dtype)


def _merge(y_m, y_a, w_pm, w_pa, proj2d, gm_off, ga_off):
    m, km = y_m.shape
    ka = y_a.shape[1]
    d = w_pm.shape[1]
    tn = _pick(d, 512, LANES)
    gmb, gab = gm_off // tn, ga_off // tn
    many_rows = m >= 16 * 512
    tm = _pick(m, 512 if many_rows else 256)
    once = pl.Buffered(1) if many_rows else None
    return pl.pallas_call(
        _merge_kernel,
        grid=(d // tn, m // tm),
        in_specs=[pl.BlockSpec((tm, km), lambda j, i: (i, 0)),
                  pl.BlockSpec((tm, ka), lambda j, i: (i, 0)),
                  pl.BlockSpec((km, tn), lambda j, i: (0, j), pipeline_mode=once),
                  pl.BlockSpec((ka, tn), lambda j, i: (0, j), pipeline_mode=once),
                  pl.BlockSpec((tm, tn), lambda j, i: (i, gmb + j)),
                  pl.BlockSpec((tm, tn), lambda j, i: (i, gab + j))],
        out_specs=pl.BlockSpec((tm, tn), lambda j, i: (i, j)),
        out_shape=jax.ShapeDtypeStruct((m, d), BF16),
        compiler_params=_params("arbitrary", "arbitrary"),
        name="merge",
    )(y_m, y_a, w_pm, w_pa, proj2d, proj2d)


def _out_kernel(mg_ref, w_ref, x_ref, gate_ref, fw_ref, o_ref, *, tn):
    j = pl.program_id(2)
    cols = pl.ds(pl.multiple_of(j * tn, tn), tn)
    o_ref[0, :, cols] = x_ref[0] + gate_ref[0] * _dot(mg_ref[0], w_ref[...])

    @pl.when(j == pl.num_programs(2) - 1)
    def _():
        r = o_ref[0]
        ms = jnp.mean(r * r, axis=-1, keepdims=True)
        o_ref[0] = r * lax.rsqrt(ms + EPS) * fw_ref[...]


def _out_proj(merged, w_out, x, gate, final_w):
    shape = x.shape
    b, t, d = shape
    gate = gate.reshape(b, 1, d)
    per_row_gate = t < 512 and b > 1
    if per_row_gate:
        gate = jnp.broadcast_to(gate, (b, t, d)).reshape(1, b * t, d)
        x = x.reshape(1, b * t, d)
        b, t = 1, b * t
    tm = _pick(t, 512)
    tn = _pick(d, 512 if per_row_gate else 1024, LANES)
    gate_spec = (pl.BlockSpec((1, tm, tn), lambda i, r, j: (i, r, j)) if per_row_gate else
                 pl.BlockSpec((1, 1, tn), lambda i, r, j: (i, 0, j)))
    out = pl.pallas_call(
        functools.partial(_out_kernel, tn=tn),
        grid=(b, t // tm, d // tn),
        in_specs=[pl.BlockSpec((1, tm, d), lambda i, r, j: (i, r, 0)),
                  pl.BlockSpec((d, tn), lambda i, r, j: (0, j)),
                  pl.BlockSpec((1, tm, tn), lambda i, r, j: (i, r, j)),
                  gate_spec,
                  pl.BlockSpec((1, d), lambda i, r, j: (0, 0))],
        out_specs=pl.BlockSpec((1, tm, d), lambda i, r, j: (i, r, 0)),
        out_shape=jax.ShapeDtypeStruct((b, t, d), F32),
        compiler_params=_params("arbitrary", "arbitrary", "arbitrary"),
        name="out_proj",
    )(merged.reshape(b, t, d), w_out, x, gate, final_w.reshape(1, d))
    return out.reshape(shape)


def _layer_path(x, mod, k_past, v_past, conv_state, ssm_state, pos0, w, lam_init, final_w):
    b, t, d = x.shape
    d_inner = 2 * d
    groups = d_inner // GROUP_W
    bcw = groups * M_STATE
    conv_dim = d_inner + 2 * bcw
    kvw = (d // QH_W) * KV_W
    z_off, xbc_off = 0, d_inner
    k_off, v_off, za_off, gm_off, ga_off = d, d + kvw, d + 2 * kvw, 2 * d + 2 * kvw, 3 * d + 2 * kvw
    assert k_off % kvw == 0 and za_off % QH_W == 0, "consumer column blocks must be block-aligned"
    shift, scale, gate = mod[:, :d], mod[:, d:2 * d], mod[:, 2 * d:]

    h = _prenorm(x, w['norm_w'], scale, shift).reshape(b * t, d)
    proj_m = _matmul_f32w(h, w['w_in'], w['layer'], 0, xbc_off + conv_dim, "in_proj_m").reshape(b, t, -1)
    proj_a = _matmul_f32w(h, w['w_in'], w['layer'], w['a_col0'], ga_off + d, "in_proj_a").reshape(b, t, -1)
    dt = _dt_proj(h, w['w_in'], w['layer'], w['dt_col0'], w['dt_bias']).reshape(b, t, -1)

    pos = pos0 + jnp.arange(t, dtype=jnp.int32)
    qr, k_new, v_new, kb, vb = _rope_kv(proj_a, pos, d, kvw, k_off // kvw, v_off // kvw)

    y_m, ssm_new = _ssd(proj_m, dt, conv_state, ssm_state, w['conv_w'], w['conv_b'], w['a_log'], w['d_skip'],
                        w['mamba_norm_w'], z_off, xbc_off, d_inner)
    halo = M_CONV - 1
    conv_new = proj_m[:, t - halo:, xbc_off:xbc_off + conv_dim]

    y_a = _attention(qr, kb, vb, proj_a, za_off, k_past, v_past,
                     (w['lam_q1'], w['lam_k1'], w['lam_q2'], w['lam_k2']), w['attn_norm_w'], lam_init, pos0)

    merged = _merge(y_m.reshape(b * t, d_inner), y_a.reshape(b * t, d), w['w_proj_m'], w['w_proj_a'],
                    proj_a.reshape(b * t, -1), gm_off, ga_off)
    y = _out_proj(merged, w['w_out'], x, gate, final_w)
    kvh = d // QH_W
    return (y, k_new.reshape(b, t, kvh, 2, A_HEAD_DIM), v_new.reshape(b, t, kvh, KV_W), conv_new, ssm_new)


def kernel(x_prompt, x_sample, cache_k, cache_v, state_conv, state_ssm, c_prompt, c_sample,
           w_ada, b_ada, norm_w, w_in, conv_w, conv_b, dt_bias, a_log, d_skip, mamba_norm_w,
           lam_q1, lam_k1, lam_q2, lam_k2, attn_norm_w, w_proj_m, w_proj_a, w_out, final_norm_w):
    depth = w_in.shape[0]
    assert depth == 1, "the final norm is fused into the single layer's output projection"
    bp, d = c_prompt.shape
    bs = c_sample.shape[0]
    past = cache_k.shape[2]
    d_inner = 2 * d
    groups = d_inner // GROUP_W
    heads = groups * M_HPG
    conv_dim = d_inner + 2 * groups * M_STATE
    kvw = (d // QH_W) * KV_W
    sizes = (d_inner, conv_dim, heads, d, kvw, kvw, d, d, d)
    offs = [0]
    for s in sizes:
        offs.append(offs[-1] + s)

    i = 0
    wi = w_in[i]
    w = {
        'w_in': w_in, 'layer': i, 'dt_col0': offs[2], 'a_col0': offs[3],
        'norm_w': norm_w[i], 'conv_w': conv_w[i], 'conv_b': conv_b[i], 'dt_bias': dt_bias[i], 'a_log': a_log[i],
        'd_skip': d_skip[i], 'mamba_norm_w': mamba_norm_w[i], 'lam_q1': lam_q1[i], 'lam_k1': lam_k1[i],
        'lam_q2': lam_q2[i], 'lam_k2': lam_k2[i], 'attn_norm_w': attn_norm_w[i],
        'w_proj_m': w_proj_m[i].astype(BF16), 'w_proj_a': w_proj_a[i].astype(BF16), 'w_out': w_out[i].astype(BF16),
    }
    lam_init = 0.8 - 0.6 * math.exp(-0.3 * i)
    pad_rows = -(bp + bs) % 16
    c_all = jnp.concatenate([c_prompt, c_sample, jnp.zeros((pad_rows, d), F32)], axis=0)
    mod = _ada_mod(c_all, w_ada[i], b_ada[i])

    yp, kp, vp, cp, sp = _layer_path(x_prompt, mod[:bp], None, None, None, None, 0, w, lam_init, final_norm_w)
    ck = cache_k[i].reshape(bs, past, kvw)
    cv = cache_v[i].reshape(bs, past, kvw)
    ys, kq, vq, cq, sq = _layer_path(x_sample, mod[bp:bp + bs], ck, cv, state_conv[i], state_ssm[i], past, w, lam_init,
                                     final_norm_w)
    st = lambda a: a[None]
    return (yp, ys, st(kp), st(vp), st(cp), st(sp), st(kq), st(vq), st(cq), st(sq))
```

```python
import functools
import math

import jax
import jax.numpy as jnp
from jax import lax
from jax.experimental import pallas as pl
from jax.experimental.pallas import tpu as pltpu

F32 = jnp.float32
BF16 = jnp.bfloat16

CHUNK = 64
EPS = 1e-6
M_HEAD_DIM = 64
M_HPG = 16
M_STATE = 128
M_CONV = 4
GROUP_W = M_HPG * M_HEAD_DIM
A_HEAD_DIM = 64
A_REP = 4
KV_W = 2 * A_HEAD_DIM
QH_W = A_REP * KV_W
ROPE_DIM = 16
ROPE_THETA = 500000.0
LANES = 128
VMEM_LIMIT_BYTES = 56 * 1024 * 1024


def _pick(n, target, mult=8):
    if n <= target:
        return n
    for t in range(target, 0, -1):
        if n % t == 0 and t % mult == 0:
            return t
    return n


def _params(*sem):
    return pltpu.CompilerParams(dimension_semantics=sem, vmem_limit_bytes=VMEM_LIMIT_BYTES)


def _silu(x):
    return x * jax.nn.sigmoid(x)


def _split_hi_lo(x):
    hi = x.astype(BF16)
    lo = (x - hi.astype(F32)).astype(BF16)
    return hi, lo


def _dot(a, b):
    return jnp.dot(a, b, preferred_element_type=F32)


def _dot_nt(a, b):
    return lax.dot_general(a, b, (((1,), (1,)), ((), ())), preferred_element_type=F32)


def _dot_tn(a, b):
    return lax.dot_general(a, b, (((0,), (0,)), ((), ())), preferred_element_type=F32)


def _ada_kernel(c_ref, w_ref, b_ref, o_ref):
    a = _silu(c_ref[...]).astype(BF16)
    o_ref[...] = _dot(a, w_ref[...].astype(BF16)) + b_ref[...]


def _ada_mod(c, w_ada, b_ada):
    m, d = c.shape
    n = w_ada.shape[1]
    tn = _pick(n, 512, LANES)
    return pl.pallas_call(
        _ada_kernel,
        grid=(n // tn,),
        in_specs=[pl.BlockSpec((m, d), lambda j: (0, 0)),
                  pl.BlockSpec((d, tn), lambda j: (0, j)),
                  pl.BlockSpec((1, tn), lambda j: (0, j))],
        out_specs=pl.BlockSpec((m, tn), lambda j: (0, j)),
        out_shape=jax.ShapeDtypeStruct((m, n), F32),
        compiler_params=_params("arbitrary"),
        name="ada_mod",
    )(c, w_ada, b_ada.reshape(1, n))


def _prenorm_kernel(x_ref, nw_ref, sc_ref, sh_ref, o_ref):
    x = x_ref[0]
    ms = jnp.mean(x * x, axis=-1, keepdims=True)
    y = x * lax.rsqrt(ms + EPS) * nw_ref[...]
    o_ref[0] = (y * (1.0 + sc_ref[0]) + sh_ref[0]).astype(o_ref.dtype)


def _prenorm(x, norm_w, scale, shift):
    b, t, d = x.shape
    tr = _pick(t, 256)
    return pl.pallas_call(
        _prenorm_kernel,
        grid=(b, t // tr),
        in_specs=[pl.BlockSpec((1, tr, d), lambda i, j: (i, j, 0)),
                  pl.BlockSpec((1, d), lambda i, j: (0, 0)),
                  pl.BlockSpec((1, 1, d), lambda i, j: (i, 0, 0)),
                  pl.BlockSpec((1, 1, d), lambda i, j: (i, 0, 0))],
        out_specs=pl.BlockSpec((1, tr, d), lambda i, j: (i, j, 0)),
        out_shape=jax.ShapeDtypeStruct((b, t, d), BF16),
        compiler_params=_params("arbitrary", "arbitrary"),
        name="prenorm",
    )(x, norm_w.reshape(1, d), scale.reshape(b, 1, d), shift.reshape(b, 1, d))


def _mm_kernel(a_ref, w_ref, o_ref):
    o_ref[...] = _dot(a_ref[...], w_ref[...]).astype(o_ref.dtype)


def _round_kernel(w_ref, o_ref):
    o_ref[...] = w_ref[0].astype(o_ref.dtype)


def _round_cols(w3, layer, col0, n):
    k = w3.shape[1]
    tn = _pick(n, 1024, LANES)
    if col0 % LANES:
        w3, col0 = w3[:, :, col0:col0 + n], 0
    return pl.pallas_call(
        _round_kernel,
        grid=(n // tn,),
        in_specs=[pl.BlockSpec((pl.Element(1), pl.Element(k), pl.Element(tn)),
                               lambda j: (layer, 0, pl.multiple_of(col0 + j * tn, LANES)))],
        out_specs=pl.BlockSpec((k, tn), lambda j: (0, j)),
        out_shape=jax.ShapeDtypeStruct((k, n), BF16),
        compiler_params=_params("arbitrary"),
        name="round_w_in",
    )(w3)


def _matmul(a, w, name):
    m, k = a.shape
    n = w.shape[1]
    tm = _pick(m, 1024)
    tn = _pick(n, 1024, LANES)
    return pl.pallas_call(
        _mm_kernel,
        grid=(n // tn, m // tm),
        in_specs=[pl.BlockSpec((tm, k), lambda j, i: (i, 0)),
                  pl.BlockSpec((k, tn), lambda j, i: (0, j))],
        out_specs=pl.BlockSpec((tm, tn), lambda j, i: (i, j)),
        out_shape=jax.ShapeDtypeStruct((m, n), F32),
        compiler_params=_params("arbitrary", "arbitrary"),
        name=name,
    )(a, w)


def _dt_kernel(a_ref, w_ref, b_ref, o_ref, wb_scr):
    @pl.when(pl.program_id(0) == 0)
    def _():
        wb_scr[...] = w_ref[0].astype(BF16)

    x = _dot(a_ref[...], wb_scr[...]) + b_ref[...]
    o_ref[...] = jnp.maximum(x, 0.0) + jnp.log1p(jnp.exp(-jnp.abs(x)))


def _dt_proj(a, w3, layer, col0, dt_bias):
    m, k = a.shape
    n = dt_bias.shape[0]
    tm = _pick(m, 512)
    if col0 % LANES or n % LANES:
        w3, col0 = w3[:, :, col0:col0 + n], 0
    return pl.pallas_call(
        _dt_kernel,
        grid=(m // tm,),
        in_specs=[pl.BlockSpec((tm, k), lambda i: (i, 0)),
                  pl.BlockSpec((pl.Element(1), pl.Element(k), pl.Element(n)),
                               lambda i: (layer, 0, col0)),
                  pl.BlockSpec((1, n), lambda i: (0, 0))],
        out_specs=pl.BlockSpec((tm, n), lambda i: (i, 0)),
        out_shape=jax.ShapeDtypeStruct((m, n), F32),
        scratch_shapes=[pltpu.VMEM((k, n), BF16)],
        compiler_params=_params("arbitrary"),
        name="dt_proj",
    )(a, w3, dt_bias.reshape(1, n))


def _rope_slab(x, cos, sin_up, sin_dn):
    return x * cos + pltpu.roll(x, LANES - ROPE_DIM // 2, 1) * sin_up + pltpu.roll(x, ROPE_DIM // 2, 1) * sin_dn


def _rope_kernel(q_ref, k_ref, v_ref, cos_ref, sup_ref, sdn_ref, qo_ref, ko_ref, vo_ref, kb_ref, vb_ref, *, q_scale):
    cos, sup, sdn = cos_ref[...], sup_ref[...], sdn_ref[...]
    for s in range(q_ref.shape[2] // LANES):
        sl = slice(s * LANES, (s + 1) * LANES)
        qo_ref[0, :, sl] = (_rope_slab(q_ref[0, :, sl], cos, sup, sdn) * q_scale).astype(qo_ref.dtype)
    for s in range(k_ref.shape[2] // LANES):
        sl = slice(s * LANES, (s + 1) * LANES)
        k = _rope_slab(k_ref[0, :, sl], cos, sup, sdn)
        ko_ref[0, :, sl] = k
        kb_ref[0, :, sl] = k.astype(kb_ref.dtype)
    v = v_ref[...]
    vo_ref[...] = v
    vb_ref[...] = v.astype(vb_ref.dtype)


def _rope_tables(pos):
    half = ROPE_DIM // 2
    inv = ROPE_THETA ** (-jnp.arange(half, dtype=F32) / half)
    ang = pos.astype(F32)[:, None] * inv[None, :]
    cos, sin = jnp.cos(ang), jnp.sin(ang)
    t = pos.shape[0]
    ones = jnp.ones((t, A_HEAD_DIM - ROPE_DIM), F32)
    zeros = jnp.zeros((t, A_HEAD_DIM - ROPE_DIM), F32)
    zh = jnp.zeros((t, half), F32)
    cos64 = jnp.concatenate([cos, cos, ones], axis=1)
    sup64 = jnp.concatenate([-sin, zh, zeros], axis=1)
    sdn64 = jnp.concatenate([zh, sin, zeros], axis=1)
    rep = lambda a: jnp.concatenate([a, a], axis=1)
    return rep(cos64), rep(sup64), rep(sdn64)


def _rope_kv(proj, pos, d, kvw, k_blk, v_blk):
    b, t, _ = proj.shape
    tr = _pick(t, 256)
    cos, sup, sdn = _rope_tables(pos)
    tab = pl.BlockSpec((tr, LANES), lambda i, j: (j, 0))
    kv_out = pl.BlockSpec((1, tr, kvw), lambda i, j: (i, j, 0))
    return pl.pallas_call(
        functools.partial(_rope_kernel, q_scale=A_HEAD_DIM ** -0.5 * math.log2(math.e)),
        grid=(b, t // tr),
        in_specs=[pl.BlockSpec((1, tr, d), lambda i, j: (i, j, 0)),
                  pl.BlockSpec((1, tr, kvw), lambda i, j: (i, j, k_blk)),
                  pl.BlockSpec((1, tr, kvw), lambda i, j: (i, j, v_blk)),
                  tab, tab, tab],
        out_specs=[pl.BlockSpec((1, tr, d), lambda i, j: (i, j, 0))] + [kv_out] * 4,
        out_shape=[jax.ShapeDtypeStruct((b, t, d), BF16),
                   jax.ShapeDtypeStruct((b, t, kvw), F32),
                   jax.ShapeDtypeStruct((b, t, kvw), F32),
                   jax.ShapeDtypeStruct((b, t, kvw), BF16),
                   jax.ShapeDtypeStruct((b, t, kvw), BF16)],
        compiler_params=_params("arbitrary", "arbitrary"),
        name="rope_kv",
    )(proj, proj, proj, cos, sup, sdn)


def _ssd_kernel(*refs, tl, has_state):
    (alog_ref, alr_ref, dsk_ref, mnw_ref, cwx_ref, cwb_ref, cwc_ref, cbx_ref, cbb_ref, cbc_ref,
     z_ref, x_ref, bm_ref, cm_ref, dt_ref, dtr_ref) = refs[:16]
    if has_state:
        csx_ref, csb_ref, csc_ref, s0_ref = refs[16:20]
        rest = refs[20:]
    else:
        rest = refs[16:]
    y_ref, sout_ref, xbuf, bbuf, cbuf, st_scr = rest
    g = pl.program_id(1)
    c = pl.program_id(2)
    L = CHUNK
    nsub = tl // L
    halo = M_CONV - 1
    base = 8

    @pl.when(c == 0)
    def _():
        if has_state:
            xbuf[base - halo:base, :] = csx_ref[0]
            bbuf[base - halo:base, :] = csb_ref[0]
            cbuf[base - halo:base, :] = csc_ref[0]
            st_scr[...] = s0_ref[0].reshape(GROUP_W, M_STATE).T
        else:
            xbuf[0:base, :] = jnp.zeros((base, GROUP_W), F32)
            bbuf[0:base, :] = jnp.zeros((base, M_STATE), F32)
            cbuf[0:base, :] = jnp.zeros((base, M_STATE), F32)
            st_scr[...] = jnp.zeros_like(st_scr)

    xbuf[base:base + tl, :] = x_ref[0]
    bbuf[base:base + tl, :] = bm_ref[0]
    cbuf[base:base + tl, :] = cm_ref[0]

    r64 = lax.broadcasted_iota(jnp.int32, (L, L), 0)
    c64 = lax.broadcasted_iota(jnp.int32, (L, L), 1)
    tril = (c64 <= r64).astype(BF16)
    heads = dt_ref.shape[2]
    hrow = lax.broadcasted_iota(jnp.int32, (heads, GROUP_W), 0)
    hcol = lax.broadcasted_iota(jnp.int32, (heads, GROUP_W), 1)
    expand = (hrow == g * M_HPG + hcol // M_HEAD_DIM).astype(BF16)
    br = lax.broadcasted_iota(jnp.int32, (LANES, LANES), 0)
    bc = lax.broadcasted_iota(jnp.int32, (LANES, LANES), 1)
    same_half = (br // L) == (bc // L)
    triu2 = (same_half & ((br % L) <= (bc % L))).astype(BF16)
    pr = lax.broadcasted_iota(jnp.int32, (L, LANES), 0)
    pc = lax.broadcasted_iota(jnp.int32, (L, LANES), 1)
    causal2 = (pc % L) <= pr

    a_col = -jnp.exp(alog_ref[...])
    a_row = -jnp.exp(alr_ref[0])
    n_rows = dtr_ref.shape[2]
    da_r = dtr_ref[0, 0] * jnp.concatenate([a_row] * (n_rows // 8), axis=0)
    hi, lo = _split_hi_lo(da_r)
    acum_r_all = _dot(hi, triu2) + _dot(lo, triu2)

    def conv(buf, w_ref, b_ref, i):
        acc = b_ref[...] + w_ref[M_CONV - 1:M_CONV, :] * buf[base + i * L:base + (i + 1) * L, :]
        for k in range(M_CONV - 1):
            off = base - halo + k + i * L
            acc = acc + w_ref[k:k + 1, :] * buf[off:off + L, :]
        return _silu(acc)

    for i in range(nsub):
        rows = slice(i * L, (i + 1) * L)
        x = conv(xbuf, cwx_ref, cbx_ref, i)
        bm = conv(bbuf, cwb_ref, cbb_ref, i).astype(BF16)
        cm = conv(cbuf, cwc_ref, cbc_ref, i).astype(BF16)
        dt = dt_ref[0, rows, :]
        da = dt * a_col
        hi, lo = _split_hi_lo(da)
        acum = _dot(tril, hi) + _dot(tril, lo)
        hi, lo = _split_hi_lo(acum)
        acum_x = _dot(hi, expand) + _dot(lo, expand)
        hi, lo = _split_hi_lo(dt)
        dt_x = _dot(hi, expand) + _dot(lo, expand)
        xdt = x * dt_x
        xdt_b = xdt.astype(BF16)
        alast_x = acum_x[L - 1:L, :]
        cb2 = _dot_nt(cm, jnp.concatenate([bm, bm], axis=0))
        st_b = st_scr[...].astype(BF16)
        y_off = _dot(cm, st_b) * jnp.exp(acum_x)
        y_parts = []
        for jj in range(GROUP_W // LANES):
            ls = slice(jj * LANES, (jj + 1) * LANES)
            seg = acum_x[:, ls] - acum_r_all[i * 8 + jj:i * 8 + jj + 1, :]
            m2 = (jnp.where(causal2, jnp.exp(seg), 0.0) * cb2).astype(BF16)
            x2 = xdt_b[:, ls]
            rhs = jnp.where(same_half, jnp.concatenate([x2, x2], axis=0), jnp.zeros((), BF16))
            y_parts.append(_dot(m2, rhs))
        y = jnp.concatenate(y_parts, axis=1) + y_off + dsk_ref[0] * x
        y = y * _silu(z_ref[0, rows, :])
        ms = jnp.mean(y * y, axis=-1, keepdims=True)
        y_ref[0, rows, :] = (y * lax.rsqrt(ms + EPS) * mnw_ref[0]).astype(y_ref.dtype)
        xw = (xdt * jnp.exp(alast_x - acum_x)).astype(BF16)
        st_scr[...] = st_scr[...] * jnp.exp(alast_x) + _dot_tn(bm, xw)

    tx = xbuf[base + tl - halo:base + tl, :]
    tb = bbuf[base + tl - halo:base + tl, :]
    tc = cbuf[base + tl - halo:base + tl, :]
    xbuf[base - halo:base, :] = tx
    bbuf[base - halo:base, :] = tb
    cbuf[base - halo:base, :] = tc

    @pl.when(c == pl.num_programs(2) - 1)
    def _():
        sout_ref[0] = st_scr[...].T.reshape(M_HPG, M_HEAD_DIM, M_STATE)


def _ssd(proj, dt, conv_state, ssm_state, conv_w, conv_b, a_log, d_skip, mnorm_w, z_off, xbc_off, d_inner):
    b, t, _ = proj.shape
    groups = d_inner // GROUP_W
    heads = groups * M_HPG
    bcw = groups * M_STATE
    has_state = conv_state is not None
    tl = _pick(t, 1024, CHUNK)
    nsub = tl // CHUNK
    nc = t // tl
    dtr = dt.reshape(b, t // CHUNK, CHUNK, groups, M_HPG // 2, 2).transpose(0, 3, 1, 4, 5, 2)
    dtr = dtr.reshape(b, groups, (t // CHUNK) * 8, LANES)
    n_rows = nsub * 8
    if n_rows < 16:
        dtr = jnp.concatenate([dtr, jnp.zeros_like(dtr)], axis=2)
        n_rows = 16
    alr = jnp.repeat(a_log.reshape(groups, M_HPG // 2, 2), CHUNK, axis=2)
    dsk = jnp.repeat(d_skip.reshape(groups, 1, M_HPG), M_HEAD_DIM, axis=2)
    mnw = mnorm_w.reshape(groups, 1, GROUP_W)
    xb, bb, cb = xbc_off // GROUP_W, (xbc_off + d_inner) // M_STATE, (xbc_off + d_inner + bcw) // M_STATE
    cxb, cbb, ccb = 0, d_inner // M_STATE, (d_inner + bcw) // M_STATE
    zb = z_off // GROUP_W
    conv_b2 = conv_b.reshape(1, -1)
    gmap = lambda blk: (lambda i, g, c: (0, blk + g))
    in_specs = [
        pl.BlockSpec((1, heads), lambda i, g, c: (0, 0)),
        pl.BlockSpec((1, 8, LANES), lambda i, g, c: (g, 0, 0)),
        pl.BlockSpec((1, 1, GROUP_W), lambda i, g, c: (g, 0, 0)),
        pl.BlockSpec((1, 1, GROUP_W), lambda i, g, c: (g, 0, 0)),
        pl.BlockSpec((M_CONV, GROUP_W), gmap(cxb)),
        pl.BlockSpec((M_CONV, M_STATE), gmap(cbb)),
        pl.BlockSpec((M_CONV, M_STATE), gmap(ccb)),
        pl.BlockSpec((1, GROUP_W), gmap(cxb)),
        pl.BlockSpec((1, M_STATE), gmap(cbb)),
        pl.BlockSpec((1, M_STATE), gmap(ccb)),
        pl.BlockSpec((1, tl, GROUP_W), lambda i, g, c: (i, c, zb + g)),
        pl.BlockSpec((1, tl, GROUP_W), lambda i, g, c: (i, c, xb + g)),
        pl.BlockSpec((1, tl, M_STATE), lambda i, g, c: (i, c, bb + g)),
        pl.BlockSpec((1, tl, M_STATE), lambda i, g, c: (i, c, cb + g)),
        pl.BlockSpec((1, tl, heads), lambda i, g, c: (i, c, 0)),
        pl.BlockSpec((1, 1, n_rows, LANES), lambda i, g, c: (i, g, c, 0)),
    ]
    args = [a_log.reshape(1, heads), alr, dsk, mnw, conv_w, conv_w, conv_w, conv_b2, conv_b2, conv_b2,
            proj, proj, proj, proj, dt.reshape(b, t, heads), dtr]
    if has_state:
        halo = M_CONV - 1
        in_specs += [
            pl.BlockSpec((1, halo, GROUP_W), lambda i, g, c: (i, 0, cxb + g)),
            pl.BlockSpec((1, halo, M_STATE), lambda i, g, c: (i, 0, cbb + g)),
            pl.BlockSpec((1, halo, M_STATE), lambda i, g, c: (i, 0, ccb + g)),
            pl.BlockSpec((1, M_HPG, M_HEAD_DIM, M_STATE), lambda i, g, c: (i, g, 0, 0)),
        ]
        args += [conv_state, conv_state, conv_state, ssm_state]
    return pl.pallas_call(
        functools.partial(_ssd_kernel, tl=tl, has_state=has_state),
        grid=(b, groups, nc),
        in_specs=in_specs,
        out_specs=[pl.BlockSpec((1, tl, GROUP_W), lambda i, g, c: (i, c, g)),
                   pl.BlockSpec((1, M_HPG, M_HEAD_DIM, M_STATE), lambda i, g, c: (i, g, 0, 0))],
        out_shape=[jax.ShapeDtypeStruct((b, t, d_inner), BF16),
                   jax.ShapeDtypeStruct((b, heads, M_HEAD_DIM, M_STATE), F32)],
        scratch_shapes=[pltpu.VMEM((tl + 8, GROUP_W), F32),
                        pltpu.VMEM((tl + 8, M_STATE), F32),
                        pltpu.VMEM((tl + 8, M_STATE), F32),
                        pltpu.VMEM((M_STATE, GROUP_W), F32)],
        compiler_params=_params("arbitrary", "arbitrary", "arbitrary"),
        name="ssd",
    )(*args)


def _attn_kernel(*refs, tq, tk, rc, nh, n_qt, past, tkp, pos0, lam_init):
    lq1_ref, lk1_ref, lq2_ref, lk2_ref, nw_ref, q_ref, k_ref, v_ref, za_ref = refs[:9]
    if past:
        kp_ref, vp_ref = refs[9:11]
        rest = refs[11:]
    else:
        rest = refs[9:]
    o_ref, qz_scr, s_scr, p_scr, m_scr, l_scr, acc_scr = rest
    qi = pl.program_id(2)
    rows = A_REP * tq

    hq = tq // nh
    hrows = rows // nh
    row_blocks = [(half, r, half * hrows + r * hq) for half in range(nh) for r in range(A_REP)]
    lane = lax.broadcasted_iota(jnp.int32, (hq, KV_W), 1)
    zero = jnp.zeros((), BF16)
    for half, r, row0 in row_blocks:
        q = q_ref[0, half * hq:(half + 1) * hq, r * KV_W:(r + 1) * KV_W]
        qz_scr[0, row0:row0 + hq, :] = jnp.where(lane < A_HEAD_DIM, q, zero)
        qz_scr[1, row0:row0 + hq, :] = jnp.where(lane >= A_HEAD_DIM, q, zero)
    m_scr[...] = jnp.full(m_scr.shape, -jnp.inf, F32)
    l_scr[...] = jnp.zeros(l_scr.shape, F32)
    acc_scr[...] = jnp.zeros(acc_scr.shape, F32)

    def tile(k, v, nkeys, mask_fn, row_lo=0, row_hi=rows):
        reps = nkeys // LANES
        rr = slice(row_lo, row_hi)
        for comp in range(2):
            s_scr[comp, rr, :nkeys] = _dot_nt(qz_scr[comp, rr, :], k)
            for i in range(row_lo // rc, row_hi // rc):
                rs = slice(i * rc, (i + 1) * rc)
                s = s_scr[comp, rs, :nkeys]
                mask = None if mask_fn is None else mask_fn(i)
                if mask is not None:
                    s = jnp.where(mask, s, -jnp.inf)
                m_old = m_scr[comp, rs, :]
                m_new = jnp.maximum(m_old, jnp.max(s, axis=-1, keepdims=True))
                p = jnp.exp2(s - jnp.concatenate([m_new] * reps, axis=1))
                alpha = jnp.exp2(m_old - m_new)
                psum = p[:, :LANES]
                for u in range(1, reps):
                    psum = psum + p[:, u * LANES:(u + 1) * LANES]
                l_scr[comp, rs, :] = alpha * l_scr[comp, rs, :] + psum
                acc_scr[comp, rs, :] = alpha * acc_scr[comp, rs, :]
                m_scr[comp, rs, :] = m_new
                p_scr[comp, rs, :nkeys] = p.astype(BF16)
            acc_scr[comp, rr, :] = acc_scr[comp, rr, :] + _dot(p_scr[comp, rr, :nkeys], v)

    def diag_mask(i, nkeys, n_valid, k_pos0, masked_halves=(0, 1)):
        half, rem = divmod(i * rc, hrows)
        if half not in masked_halves:
            return None
        t0 = half * hq + rem % hq
        qpos = pos0 + qi * tq + t0 + lax.broadcasted_iota(jnp.int32, (rc, nkeys), 0)
        kidx = lax.broadcasted_iota(jnp.int32, (rc, nkeys), 1)
        ok = ((k_pos0 + kidx) // CHUNK) <= (qpos // CHUNK)
        if n_valid < nkeys:
            ok = ok & (kidx < n_valid)
        return ok

    if past:
        def pbody(j, carry):
            sl = pl.ds(pl.multiple_of(j * tkp, tkp), tkp)
            tile(kp_ref[0, sl, :].astype(BF16), vp_ref[0, sl, :].astype(BF16), tkp, None)
            return carry

        lax.fori_loop(0, past // tkp, pbody, 0)

    def full_tile(j):
        sl = pl.ds(pl.multiple_of(j * tk, tk), tk)
        tile(k_ref[0, sl, :], v_ref[0, sl, :], tk, None)

    def pair_body(j2, carry):
        full_tile(2 * j2)
        full_tile(2 * j2 + 1)
        return carry

    if n_qt > 1:
        n_full = (qi * tq) // tk
        lax.fori_loop(0, n_full // 2, pair_body, 0)
        pl.when(n_full % 2 == 1)(lambda: full_tile(n_full - 1))
    k0 = qi * tq
    if nh == 2:
        sl_a = pl.ds(pl.multiple_of(k0, hq), hq)
        sl_b = pl.ds(pl.multiple_of(k0 + hq, hq), hq)
        tile(k_ref[0, sl_a, :], v_ref[0, sl_a, :], hq,
             functools.partial(diag_mask, nkeys=hq, n_valid=hq, k_pos0=pos0 + k0, masked_halves=(0,)))
        tile(k_ref[0, sl_b, :], v_ref[0, sl_b, :], hq,
             functools.partial(diag_mask, nkeys=hq, n_valid=hq, k_pos0=pos0 + k0 + hq), row_lo=hrows)
    else:
        tkd = max(tq, LANES)
        sl = pl.ds(pl.multiple_of(k0, tq), tq)
        k, v = k_ref[0, sl, :], v_ref[0, sl, :]
        if tkd > tq:
            pad = jnp.zeros((tkd - tq, KV_W), BF16)
            k, v = jnp.concatenate([k, pad], axis=0), jnp.concatenate([v, pad], axis=0)
        tile(k, v, tkd, functools.partial(diag_mask, nkeys=tkd, n_valid=tq, k_pos0=pos0 + k0))

    lam = (jnp.exp(jnp.sum(lq1_ref[...] * lk1_ref[...], axis=-1, keepdims=True))
           - jnp.exp(jnp.sum(lq2_ref[...] * lk2_ref[...], axis=-1, keepdims=True)) + lam_init)
    l0 = jnp.sum(l_scr[0], axis=-1, keepdims=True)
    l1 = jnp.sum(l_scr[1], axis=-1, keepdims=True)
    o = acc_scr[0] / l0 - lam * (acc_scr[1] / l1)
    ms = jnp.mean(o * o, axis=-1, keepdims=True)
    on = o * lax.rsqrt(ms + EPS) * nw_ref[...] * (1.0 - lam_init)
    for half, r, row0 in row_blocks:
        ts, sl = slice(half * hq, (half + 1) * hq), slice(r * KV_W, (r + 1) * KV_W)
        o_ref[0, ts, sl] = (on[row0:row0 + hq, :] * _silu(za_ref[0, ts, sl])).astype(o_ref.dtype)


def _attention(qr, kb, vb, proj, za_off, k_past, v_past, lam_vecs, norm_w, lam_init, pos0):
    b, t, d = qr.shape
    kvh = d // QH_W
    past = 0 if k_past is None else k_past.shape[1]
    tq = _pick(t, 512, CHUNK)
    tk = tq
    tkp = _pick(past, 2048, LANES) if past else 0
    nh = 2 if tq % (2 * LANES) == 0 else 1
    rc = min(64, tq // nh)
    zb = za_off // QH_W
    rows = A_REP * tq
    smax = max(tk, LANES, tkp)
    vec = pl.BlockSpec((1, A_HEAD_DIM), lambda i, h, j: (0, 0))
    in_specs = [vec, vec, vec, vec,
                pl.BlockSpec((1, KV_W), lambda i, h, j: (0, 0)),
                pl.BlockSpec((1, tq, QH_W), lambda i, h, j: (i, j, h)),
                pl.BlockSpec((1, t, KV_W), lambda i, h, j: (i, 0, h)),
                pl.BlockSpec((1, t, KV_W), lambda i, h, j: (i, 0, h)),
                pl.BlockSpec((1, tq, QH_W), lambda i, h, j: (i, j, zb + h))]
    args = [v.reshape(1, A_HEAD_DIM) for v in lam_vecs] + [norm_w.reshape(1, KV_W), qr, kb, vb, proj]
    if past:
        in_specs += [pl.BlockSpec((1, past, KV_W), lambda i, h, j: (i, 0, h)),
                     pl.BlockSpec((1, past, KV_W), lambda i, h, j: (i, 0, h))]
        args += [k_past, v_past]
    return pl.pallas_call(
        functools.partial(_attn_kernel, tq=tq, tk=tk, rc=rc, nh=nh, n_qt=t // tq, past=past, tkp=tkp, pos0=pos0,
                          lam_init=lam_init),
        grid=(b, kvh, t // tq),
        in_specs=in_specs,
        out_specs=pl.BlockSpec((1, tq, QH_W), lambda i, h, j: (i, j, h)),
        out_shape=jax.ShapeDtypeStruct((b, t, d), BF16),
        scratch_shapes=[pltpu.VMEM((2, rows, KV_W), BF16),
                        pltpu.VMEM((2, rows, smax), F32),
                        pltpu.VMEM((2, rows, smax), BF16),
                        pltpu.VMEM((2, rows, LANES), F32),
                        pltpu.VMEM((2, rows, LANES), F32),
                        pltpu.VMEM((2, rows, KV_W), F32)],
        compiler_params=_params("arbitrary", "arbitrary", "arbitrary"),
        name="diff_attn",
    )(*args)


def _merge_kernel(ym_ref, ya_ref, wm_ref, wa_ref, gm_ref, ga_ref, o_ref):
    pm = _dot(ym_ref[...], wm_ref[...])
    pa = _dot(ya_ref[...], wa_ref[...])
    o_ref[...] = (jax.nn.sigmoid(gm_ref[...]) * pm + jax.nn.sigmoid(ga_ref[...]) * pa).astype(o_ref.dtype)


def _merge(y_m, y_a, w_pm, w_pa, proj2d, gm_off, ga_off):
    m, km = y_m.shape
    ka = y_a.shape[1]
    d = w_pm.shape[1]
    tn = _pick(d, 512, LANES)
    gmb, gab = gm_off // tn, ga_off // tn
    many_rows = m >= 16 * 512
    tm = _pick(m, 512 if many_rows else 256)
    once = pl.Buffered(1) if many_rows else None
    return pl.pallas_call(
        _merge_kernel,
        grid=(d // tn, m // tm),
        in_specs=[pl.BlockSpec((tm, km), lambda j, i: (i, 0)),
                  pl.BlockSpec((tm, ka), lambda j, i: (i, 0)),
                  pl.BlockSpec((km, tn), lambda j, i: (0, j), pipeline_mode=once),
                  pl.BlockSpec((ka, tn), lambda j, i: (0, j), pipeline_mode=once),
                  pl.BlockSpec((tm, tn), lambda j, i: (i, gmb + j)),
                  pl.BlockSpec((tm, tn), lambda j, i: (i, gab + j))],
        out_specs=pl.BlockSpec((tm, tn), lambda j, i: (i, j)),
        out_shape=jax.ShapeDtypeStruct((m, d), BF16),
        compiler_params=_params("arbitrary", "arbitrary"),
        name="merge",
    )(y_m, y_a, w_pm, w_pa, proj2d, proj2d)


def _out_kernel(mg_ref, w_ref, x_ref, gate_ref, fw_ref, o_ref, *, tn):
    j = pl.program_id(2)
    cols = pl.ds(pl.multiple_of(j * tn, tn), tn)
    o_ref[0, :, cols] = x_ref[0] + gate_ref[0] * _dot(mg_ref[0], w_ref[...])

    @pl.when(j == pl.num_programs(2) - 1)
    def _():
        r = o_ref[0]
        ms = jnp.mean(r * r, axis=-1, keepdims=True)
        o_ref[0] = r * lax.rsqrt(ms + EPS) * fw_ref[...]


def _out_proj(merged, w_out, x, gate, final_w):
    shape = x.shape
    b, t, d = shape
    gate = gate.reshape(b, 1, d)
    per_row_gate = t < 512 and b > 1
    if per_row_gate:
        gate = jnp.broadcast_to(gate, (b, t, d)).reshape(1, b * t, d)
        x = x.reshape(1, b * t, d)
        b, t = 1, b * t
    tm = _pick(t, 512)
    tn = _pick(d, 512 if per_row_gate else 1024, LANES)
    gate_spec = (pl.BlockSpec((1, tm, tn), lambda i, r, j: (i, r, j)) if per_row_gate else
                 pl.BlockSpec((1, 1, tn), lambda i, r, j: (i, 0, j)))
    out = pl.pallas_call(
        functools.partial(_out_kernel, tn=tn),
        grid=(b, t // tm, d // tn),
        in_specs=[pl.BlockSpec((1, tm, d), lambda i, r, j: (i, r, 0)),
                  pl.BlockSpec((d, tn), lambda i, r, j: (0, j)),
                  pl.BlockSpec((1, tm, tn), lambda i, r, j: (i, r, j)),
                  gate_spec,
                  pl.BlockSpec((1, d), lambda i, r, j: (0, 0))],
        out_specs=pl.BlockSpec((1, tm, d), lambda i, r, j: (i, r, 0)),
        out_shape=jax.ShapeDtypeStruct((b, t, d), F32),
        compiler_params=_params("arbitrary", "arbitrary", "arbitrary"),
        name="out_proj",
    )(merged.reshape(b, t, d), w_out, x, gate, final_w.reshape(1, d))
    return out.reshape(shape)


def _layer_path(x, mod, k_past, v_past, conv_state, ssm_state, pos0, w, lam_init, final_w):
    b, t, d = x.shape
    d_inner = 2 * d
    groups = d_inner // GROUP_W
    bcw = groups * M_STATE
    conv_dim = d_inner + 2 * bcw
    kvw = (d // QH_W) * KV_W
    z_off, xbc_off = 0, d_inner
    k_off, v_off, za_off, gm_off, ga_off = d, d + kvw, d + 2 * kvw, 2 * d + 2 * kvw, 3 * d + 2 * kvw
    assert k_off % kvw == 0 and za_off % QH_W == 0, "consumer column blocks must be block-aligned"
    shift, scale, gate = mod[:, :d], mod[:, d:2 * d], mod[:, 2 * d:]

    h = _prenorm(x, w['norm_w'], scale, shift).reshape(b * t, d)
    proj_m = _matmul(h, w['w_in_m'], "in_proj_m").reshape(b, t, -1)
    proj_a = _matmul(h, w['w_in_a'], "in_proj_a").reshape(b, t, -1)
    dt = _dt_proj(h, w['w_in'], w['layer'], w['dt_col0'], w['dt_bias']).reshape(b, t, -1)

    pos = pos0 + jnp.arange(t, dtype=jnp.int32)
    qr, k_new, v_new, kb, vb = _rope_kv(proj_a, pos, d, kvw, k_off // kvw, v_off // kvw)

    y_m, ssm_new = _ssd(proj_m, dt, conv_state, ssm_state, w['conv_w'], w['conv_b'], w['a_log'], w['d_skip'],
                        w['mamba_norm_w'], z_off, xbc_off, d_inner)
    halo = M_CONV - 1
    conv_new = proj_m[:, t - halo:, xbc_off:xbc_off + conv_dim]

    y_a = _attention(qr, kb, vb, proj_a, za_off, k_past, v_past,
                     (w['lam_q1'], w['lam_k1'], w['lam_q2'], w['lam_k2']), w['attn_norm_w'], lam_init, pos0)

    merged = _merge(y_m.reshape(b * t, d_inner), y_a.reshape(b * t, d), w['w_proj_m'], w['w_proj_a'],
                    proj_a.reshape(b * t, -1), gm_off, ga_off)
    y = _out_proj(merged, w['w_out'], x, gate, final_w)
    kvh = d // QH_W
    return (y, k_new.reshape(b, t, kvh, 2, A_HEAD_DIM), v_new.reshape(b, t, kvh, KV_W), conv_new, ssm_new)


def kernel(x_prompt, x_sample, cache_k, cache_v, state_conv, state_ssm, c_prompt, c_sample,
           w_ada, b_ada, norm_w, w_in, conv_w, conv_b, dt_bias, a_log, d_skip, mamba_norm_w,
           lam_q1, lam_k1, lam_q2, lam_k2, attn_norm_w, w_proj_m, w_proj_a, w_out, final_norm_w):
    depth = w_in.shape[0]
    assert depth == 1, "the final norm is fused into the single layer's output projection"
    bp, d = c_prompt.shape
    bs = c_sample.shape[0]
    past = cache_k.shape[2]
    d_inner = 2 * d
    groups = d_inner // GROUP_W
    heads = groups * M_HPG
    conv_dim = d_inner + 2 * groups * M_STATE
    kvw = (d // QH_W) * KV_W
    sizes = (d_inner, conv_dim, heads, d, kvw, kvw, d, d, d)
    offs = [0]
    for s in sizes:
        offs.append(offs[-1] + s)

    i = 0
    w = {
        'w_in': w_in, 'layer': i, 'dt_col0': offs[2],
        'w_in_m': _round_cols(w_in, i, 0, offs[2]),
        'w_in_a': _round_cols(w_in, i, offs[3], offs[9] - offs[3]),
        'norm_w': norm_w[i], 'conv_w': conv_w[i], 'conv_b': conv_b[i], 'dt_bias': dt_bias[i], 'a_log': a_log[i],
        'd_skip': d_skip[i], 'mamba_norm_w': mamba_norm_w[i], 'lam_q1': lam_q1[i], 'lam_k1': lam_k1[i],
        'lam_q2': lam_q2[i], 'lam_k2': lam_k2[i], 'attn_norm_w': attn_norm_w[i],
        'w_proj_m': w_proj_m[i].astype(BF16), 'w_proj_a': w_proj_a[i].astype(BF16), 'w_out': w_out[i].astype(BF16),
    }
    lam_init = 0.8 - 0.6 * math.exp(-0.3 * i)
    pad_rows = -(bp + bs) % 16
    c_all = jnp.concatenate([c_prompt, c_sample, jnp.zeros((pad_rows, d), F32)], axis=0)
    mod = _ada_mod(c_all, w_ada[i], b_ada[i])

    yp, kp, vp, cp, sp = _layer_path(x_prompt, mod[:bp], None, None, None, None, 0, w, lam_init, final_norm_w)
    ck = cache_k[i].reshape(bs, past, kvw)
    cv = cache_v[i].reshape(bs, past, kvw)
    ys, kq, vq, cq, sq = _layer_path(x_sample, mod[bp:bp + bs], ck, cv, state_conv[i], state_ssm[i], past, w, lam_init,
                                     final_norm_w)
    st = lambda a: a[None]
    return (yp, ys, st(kp), st(vp), st(cp), st(sp), st(kq), st(vq), st(cq), st(sq))
```

```python
import functools
import math

import jax
import jax.numpy as jnp
from jax import lax
from jax.experimental import pallas as pl
from jax.experimental.pallas import tpu as pltpu

F32 = jnp.float32
BF16 = jnp.bfloat16

CHUNK = 64
EPS = 1e-6
M_HEAD_DIM = 64
M_HPG = 16
M_STATE = 128
M_CONV = 4
GROUP_W = M_HPG * M_HEAD_DIM
A_HEAD_DIM = 64
A_REP = 4
KV_W = 2 * A_HEAD_DIM
QH_W = A_REP * KV_W
ROPE_DIM = 16
ROPE_THETA = 500000.0
LANES = 128
VMEM_LIMIT_BYTES = 56 * 1024 * 1024


def _pick(n, target, mult=8):
    if n <= target:
        return n
    for t in range(target, 0, -1):
        if n % t == 0 and t % mult == 0:
            return t
    return n


def _params(*sem):
    return pltpu.CompilerParams(dimension_semantics=sem, vmem_limit_bytes=VMEM_LIMIT_BYTES)


def _silu(x):
    h = 0.5 * x
    return h + h * jnp.tanh(h)


def _split_hi_lo(x):
    hi = x.astype(BF16)
    lo = (x - hi.astype(F32)).astype(BF16)
    return hi, lo


def _dot(a, b):
    return jnp.dot(a, b, preferred_element_type=F32)


def _dot_nt(a, b):
    return lax.dot_general(a, b, (((1,), (1,)), ((), ())), preferred_element_type=F32)


def _dot_tn(a, b):
    return lax.dot_general(a, b, (((0,), (0,)), ((), ())), preferred_element_type=F32)


def _ada_kernel(c_ref, w_ref, b_ref, o_ref):
    a = _silu(c_ref[...]).astype(BF16)
    o_ref[...] = _dot(a, w_ref[...].astype(BF16)) + b_ref[...]


def _ada_mod(c, w_ada, b_ada):
    m, d = c.shape
    n = w_ada.shape[1]
    tn = _pick(n, 512, LANES)
    return pl.pallas_call(
        _ada_kernel,
        grid=(n // tn,),
        in_specs=[pl.BlockSpec((m, d), lambda j: (0, 0)),
                  pl.BlockSpec((d, tn), lambda j: (0, j)),
                  pl.BlockSpec((1, tn), lambda j: (0, j))],
        out_specs=pl.BlockSpec((m, tn), lambda j: (0, j)),
        out_shape=jax.ShapeDtypeStruct((m, n), F32),
        compiler_params=_params("arbitrary"),
        name="ada_mod",
    )(c, w_ada, b_ada.reshape(1, n))


def _prenorm_kernel(x_ref, nw_ref, sc_ref, sh_ref, o_ref):
    x = x_ref[0]
    ms = jnp.mean(x * x, axis=-1, keepdims=True)
    y = x * lax.rsqrt(ms + EPS) * nw_ref[...]
    o_ref[0] = (y * (1.0 + sc_ref[0]) + sh_ref[0]).astype(o_ref.dtype)


def _prenorm(x, norm_w, scale, shift):
    b, t, d = x.shape
    tr = _pick(t, 256)
    return pl.pallas_call(
        _prenorm_kernel,
        grid=(b, t // tr),
        in_specs=[pl.BlockSpec((1, tr, d), lambda i, j: (i, j, 0)),
                  pl.BlockSpec((1, d), lambda i, j: (0, 0)),
                  pl.BlockSpec((1, 1, d), lambda i, j: (i, 0, 0)),
                  pl.BlockSpec((1, 1, d), lambda i, j: (i, 0, 0))],
        out_specs=pl.BlockSpec((1, tr, d), lambda i, j: (i, j, 0)),
        out_shape=jax.ShapeDtypeStruct((b, t, d), BF16),
        compiler_params=_params("arbitrary", "arbitrary"),
        name="prenorm",
    )(x, norm_w.reshape(1, d), scale.reshape(b, 1, d), shift.reshape(b, 1, d))


def _mm_kernel(a_ref, w_ref, o_ref):
    o_ref[...] = _dot(a_ref[...], w_ref[...]).astype(o_ref.dtype)


def _round_kernel(w_ref, o_ref):
    o_ref[...] = w_ref[0].astype(o_ref.dtype)


def _round_cols(w3, layer, col0, n):
    k = w3.shape[1]
    tn = _pick(n, 1024, LANES)
    if col0 % LANES:
        w3, col0 = w3[:, :, col0:col0 + n], 0
    return pl.pallas_call(
        _round_kernel,
        grid=(n // tn,),
        in_specs=[pl.BlockSpec((pl.Element(1), pl.Element(k), pl.Element(tn)),
                               lambda j: (layer, 0, pl.multiple_of(col0 + j * tn, LANES)))],
        out_specs=pl.BlockSpec((k, tn), lambda j: (0, j)),
        out_shape=jax.ShapeDtypeStruct((k, n), BF16),
        compiler_params=_params("arbitrary"),
        name="round_w_in",
    )(w3)


def _matmul(a, w, name):
    m, k = a.shape
    n = w.shape[1]
    tm = _pick(m, 1024)
    tn = _pick(n, 1024, LANES)
    return pl.pallas_call(
        _mm_kernel,
        grid=(n // tn, m // tm),
        in_specs=[pl.BlockSpec((tm, k), lambda j, i: (i, 0)),
                  pl.BlockSpec((k, tn), lambda j, i: (0, j))],
        out_specs=pl.BlockSpec((tm, tn), lambda j, i: (i, j)),
        out_shape=jax.ShapeDtypeStruct((m, n), F32),
        compiler_params=_params("arbitrary", "arbitrary"),
        name=name,
    )(a, w)


def _dt_kernel(a_ref, w_ref, b_ref, o_ref, wb_scr):
    @pl.when(pl.program_id(0) == 0)
    def _():
        wb_scr[...] = w_ref[0].astype(BF16)

    x = _dot(a_ref[...], wb_scr[...]) + b_ref[...]
    o_ref[...] = jnp.maximum(x, 0.0) + jnp.log1p(jnp.exp(-jnp.abs(x)))


def _dt_proj(a, w3, layer, col0, dt_bias):
    m, k = a.shape
    n = dt_bias.shape[0]
    tm = _pick(m, 512)
    if col0 % LANES or n % LANES:
        w3, col0 = w3[:, :, col0:col0 + n], 0
    return pl.pallas_call(
        _dt_kernel,
        grid=(m // tm,),
        in_specs=[pl.BlockSpec((tm, k), lambda i: (i, 0)),
                  pl.BlockSpec((pl.Element(1), pl.Element(k), pl.Element(n)),
                               lambda i: (layer, 0, col0)),
                  pl.BlockSpec((1, n), lambda i: (0, 0))],
        out_specs=pl.BlockSpec((tm, n), lambda i: (i, 0)),
        out_shape=jax.ShapeDtypeStruct((m, n), F32),
        scratch_shapes=[pltpu.VMEM((k, n), BF16)],
        compiler_params=_params("arbitrary"),
        name="dt_proj",
    )(a, w3, dt_bias.reshape(1, n))


def _rope_slab(x, cos, sin_up, sin_dn):
    return x * cos + pltpu.roll(x, LANES - ROPE_DIM // 2, 1) * sin_up + pltpu.roll(x, ROPE_DIM // 2, 1) * sin_dn


def _rope_kernel(q_ref, k_ref, v_ref, cos_ref, sup_ref, sdn_ref, qo_ref, ko_ref, vo_ref, kb_ref, vb_ref, *, q_scale):
    cos, sup, sdn = cos_ref[...], sup_ref[...], sdn_ref[...]
    for s in range(q_ref.shape[2] // LANES):
        sl = slice(s * LANES, (s + 1) * LANES)
        qo_ref[0, :, sl] = (_rope_slab(q_ref[0, :, sl], cos, sup, sdn) * q_scale).astype(qo_ref.dtype)
    for s in range(k_ref.shape[2] // LANES):
        sl = slice(s * LANES, (s + 1) * LANES)
        k = _rope_slab(k_ref[0, :, sl], cos, sup, sdn)
        ko_ref[0, :, sl] = k
        kb_ref[0, :, sl] = k.astype(kb_ref.dtype)
    v = v_ref[...]
    vo_ref[...] = v
    vb_ref[...] = v.astype(vb_ref.dtype)


def _rope_tables(pos):
    half = ROPE_DIM // 2
    inv = ROPE_THETA ** (-jnp.arange(half, dtype=F32) / half)
    ang = pos.astype(F32)[:, None] * inv[None, :]
    cos, sin = jnp.cos(ang), jnp.sin(ang)
    t = pos.shape[0]
    ones = jnp.ones((t, A_HEAD_DIM - ROPE_DIM), F32)
    zeros = jnp.zeros((t, A_HEAD_DIM - ROPE_DIM), F32)
    zh = jnp.zeros((t, half), F32)
    cos64 = jnp.concatenate([cos, cos, ones], axis=1)
    sup64 = jnp.concatenate([-sin, zh, zeros], axis=1)
    sdn64 = jnp.concatenate([zh, sin, zeros], axis=1)
    rep = lambda a: jnp.concatenate([a, a], axis=1)
    return rep(cos64), rep(sup64), rep(sdn64)


def _rope_kv(proj, pos, d, kvw, k_blk, v_blk):
    b, t, _ = proj.shape
    tr = _pick(t, 256)
    cos, sup, sdn = _rope_tables(pos)
    tab = pl.BlockSpec((tr, LANES), lambda i, j: (j, 0))
    kv_out = pl.BlockSpec((1, tr, kvw), lambda i, j: (i, j, 0))
    return pl.pallas_call(
        functools.partial(_rope_kernel, q_scale=A_HEAD_DIM ** -0.5 * math.log2(math.e)),
        grid=(b, t // tr),
        in_specs=[pl.BlockSpec((1, tr, d), lambda i, j: (i, j, 0)),
                  pl.BlockSpec((1, tr, kvw), lambda i, j: (i, j, k_blk)),
                  pl.BlockSpec((1, tr, kvw), lambda i, j: (i, j, v_blk)),
                  tab, tab, tab],
        out_specs=[pl.BlockSpec((1, tr, d), lambda i, j: (i, j, 0))] + [kv_out] * 4,
        out_shape=[jax.ShapeDtypeStruct((b, t, d), BF16),
                   jax.ShapeDtypeStruct((b, t, kvw), F32),
                   jax.ShapeDtypeStruct((b, t, kvw), F32),
                   jax.ShapeDtypeStruct((b, t, kvw), BF16),
                   jax.ShapeDtypeStruct((b, t, kvw), BF16)],
        compiler_params=_params("arbitrary", "arbitrary"),
        name="rope_kv",
    )(proj, proj, proj, cos, sup, sdn)


def _ssd_kernel(*refs, tl, has_state):
    (alog_ref, alr_ref, dsk_ref, mnw_ref, cwx_ref, cwb_ref, cwc_ref, cbx_ref, cbb_ref, cbc_ref,
     z_ref, x_ref, bm_ref, cm_ref, dt_ref, dtr_ref) = refs[:16]
    if has_state:
        csx_ref, csb_ref, csc_ref, s0_ref = refs[16:20]
        rest = refs[20:]
    else:
        rest = refs[16:]
    y_ref, sout_ref, xbuf, bbuf, cbuf, st_scr = rest
    g = pl.program_id(1)
    c = pl.program_id(2)
    L = CHUNK
    nsub = tl // L
    halo = M_CONV - 1
    base = 8

    @pl.when(c == 0)
    def _():
        if has_state:
            xbuf[base - halo:base, :] = csx_ref[0]
            bbuf[base - halo:base, :] = csb_ref[0]
            cbuf[base - halo:base, :] = csc_ref[0]
            st_scr[...] = s0_ref[0].reshape(GROUP_W, M_STATE).T
        else:
            xbuf[0:base, :] = jnp.zeros((base, GROUP_W), F32)
            bbuf[0:base, :] = jnp.zeros((base, M_STATE), F32)
            cbuf[0:base, :] = jnp.zeros((base, M_STATE), F32)
            st_scr[...] = jnp.zeros_like(st_scr)

    xbuf[base:base + tl, :] = x_ref[0]
    bbuf[base:base + tl, :] = bm_ref[0]
    cbuf[base:base + tl, :] = cm_ref[0]

    r64 = lax.broadcasted_iota(jnp.int32, (L, L), 0)
    c64 = lax.broadcasted_iota(jnp.int32, (L, L), 1)
    tril = (c64 <= r64).astype(BF16)
    heads = dt_ref.shape[2]
    hrow = lax.broadcasted_iota(jnp.int32, (heads, GROUP_W), 0)
    hcol = lax.broadcasted_iota(jnp.int32, (heads, GROUP_W), 1)
    expand = (hrow == g * M_HPG + hcol // M_HEAD_DIM).astype(BF16)
    br = lax.broadcasted_iota(jnp.int32, (LANES, LANES), 0)
    bc = lax.broadcasted_iota(jnp.int32, (LANES, LANES), 1)
    same_half = (br // L) == (bc // L)
    triu2 = (same_half & ((br % L) <= (bc % L))).astype(BF16)
    pr = lax.broadcasted_iota(jnp.int32, (L, LANES), 0)
    pc = lax.broadcasted_iota(jnp.int32, (L, LANES), 1)
    causal2 = (pc % L) <= pr

    a_col = -jnp.exp(alog_ref[...])
    a_row = -jnp.exp(alr_ref[0])
    n_rows = dtr_ref.shape[2]
    da_r = dtr_ref[0, 0] * jnp.concatenate([a_row] * (n_rows // 8), axis=0)
    hi, lo = _split_hi_lo(da_r)
    acum_r_all = _dot(hi, triu2) + _dot(lo, triu2)

    def conv(buf, w_ref, b_ref, i):
        acc = b_ref[...] + w_ref[M_CONV - 1:M_CONV, :] * buf[base + i * L:base + (i + 1) * L, :]
        for k in range(M_CONV - 1):
            off = base - halo + k + i * L
            acc = acc + w_ref[k:k + 1, :] * buf[off:off + L, :]
        return _silu(acc)

    for i in range(nsub):
        rows = slice(i * L, (i + 1) * L)
        x = conv(xbuf, cwx_ref, cbx_ref, i)
        bm = conv(bbuf, cwb_ref, cbb_ref, i).astype(BF16)
        cm = conv(cbuf, cwc_ref, cbc_ref, i).astype(BF16)
        dt = dt_ref[0, rows, :]
        da = dt * a_col
        hi, lo = _split_hi_lo(da)
        acum = _dot(tril, hi) + _dot(tril, lo)
        hi, lo = _split_hi_lo(acum)
        acum_x = _dot(hi, expand) + _dot(lo, expand)
        hi, lo = _split_hi_lo(dt)
        dt_x = _dot(hi, expand) + _dot(lo, expand)
        xdt = x * dt_x
        xdt_b = xdt.astype(BF16)
        alast_x = acum_x[L - 1:L, :]
        cb2 = _dot_nt(cm, jnp.concatenate([bm, bm], axis=0))
        st_b = st_scr[...].astype(BF16)
        y_off = _dot(cm, st_b) * jnp.exp(acum_x)
        y_parts = []
        for jj in range(GROUP_W // LANES):
            ls = slice(jj * LANES, (jj + 1) * LANES)
            seg = acum_x[:, ls] - acum_r_all[i * 8 + jj:i * 8 + jj + 1, :]
            m2 = (jnp.where(causal2, jnp.exp(seg), 0.0) * cb2).astype(BF16)
            x2 = xdt_b[:, ls]
            rhs = jnp.where(same_half, jnp.concatenate([x2, x2], axis=0), jnp.zeros((), BF16))
            y_parts.append(_dot(m2, rhs))
        y = jnp.concatenate(y_parts, axis=1) + y_off + dsk_ref[0] * x
        y = y * _silu(z_ref[0, rows, :])
        ms = jnp.mean(y * y, axis=-1, keepdims=True)
        y_ref[0, rows, :] = (y * lax.rsqrt(ms + EPS) * mnw_ref[0]).astype(y_ref.dtype)
        xw = (xdt * jnp.exp(alast_x - acum_x)).astype(BF16)
        st_scr[...] = st_scr[...] * jnp.exp(alast_x) + _dot_tn(bm, xw)

    tx = xbuf[base + tl - halo:base + tl, :]
    tb = bbuf[base + tl - halo:base + tl, :]
    tc = cbuf[base + tl - halo:base + tl, :]
    xbuf[base - halo:base, :] = tx
    bbuf[base - halo:base, :] = tb
    cbuf[base - halo:base, :] = tc

    @pl.when(c == pl.num_programs(2) - 1)
    def _():
        sout_ref[0] = st_scr[...].T.reshape(M_HPG, M_HEAD_DIM, M_STATE)


def _ssd(proj, dt, conv_state, ssm_state, conv_w, conv_b, a_log, d_skip, mnorm_w, z_off, xbc_off, d_inner):
    b, t, _ = proj.shape
    groups = d_inner // GROUP_W
    heads = groups * M_HPG
    bcw = groups * M_STATE
    has_state = conv_state is not None
    tl = _pick(t, 1024, CHUNK)
    nsub = tl // CHUNK
    nc = t // tl
    dtr = dt.reshape(b, t // CHUNK, CHUNK, groups, M_HPG // 2, 2).transpose(0, 3, 1, 4, 5, 2)
    dtr = dtr.reshape(b, groups, (t // CHUNK) * 8, LANES)
    n_rows = nsub * 8
    if n_rows < 16:
        dtr = jnp.concatenate([dtr, jnp.zeros_like(dtr)], axis=2)
        n_rows = 16
    alr = jnp.repeat(a_log.reshape(groups, M_HPG // 2, 2), CHUNK, axis=2)
    dsk = jnp.repeat(d_skip.reshape(groups, 1, M_HPG), M_HEAD_DIM, axis=2)
    mnw = mnorm_w.reshape(groups, 1, GROUP_W)
    xb, bb, cb = xbc_off // GROUP_W, (xbc_off + d_inner) // M_STATE, (xbc_off + d_inner + bcw) // M_STATE
    cxb, cbb, ccb = 0, d_inner // M_STATE, (d_inner + bcw) // M_STATE
    zb = z_off // GROUP_W
    conv_b2 = conv_b.reshape(1, -1)
    gmap = lambda blk: (lambda i, g, c: (0, blk + g))
    in_specs = [
        pl.BlockSpec((1, heads), lambda i, g, c: (0, 0)),
        pl.BlockSpec((1, 8, LANES), lambda i, g, c: (g, 0, 0)),
        pl.BlockSpec((1, 1, GROUP_W), lambda i, g, c: (g, 0, 0)),
        pl.BlockSpec((1, 1, GROUP_W), lambda i, g, c: (g, 0, 0)),
        pl.BlockSpec((M_CONV, GROUP_W), gmap(cxb)),
        pl.BlockSpec((M_CONV, M_STATE), gmap(cbb)),
        pl.BlockSpec((M_CONV, M_STATE), gmap(ccb)),
        pl.BlockSpec((1, GROUP_W), gmap(cxb)),
        pl.BlockSpec((1, M_STATE), gmap(cbb)),
        pl.BlockSpec((1, M_STATE), gmap(ccb)),
        pl.BlockSpec((1, tl, GROUP_W), lambda i, g, c: (i, c, zb + g)),
        pl.BlockSpec((1, tl, GROUP_W), lambda i, g, c: (i, c, xb + g)),
        pl.BlockSpec((1, tl, M_STATE), lambda i, g, c: (i, c, bb + g)),
        pl.BlockSpec((1, tl, M_STATE), lambda i, g, c: (i, c, cb + g)),
        pl.BlockSpec((1, tl, heads), lambda i, g, c: (i, c, 0)),
        pl.BlockSpec((1, 1, n_rows, LANES), lambda i, g, c: (i, g, c, 0)),
    ]
    args = [a_log.reshape(1, heads), alr, dsk, mnw, conv_w, conv_w, conv_w, conv_b2, conv_b2, conv_b2,
            proj, proj, proj, proj, dt.reshape(b, t, heads), dtr]
    if has_state:
        halo = M_CONV - 1
        in_specs += [
            pl.BlockSpec((1, halo, GROUP_W), lambda i, g, c: (i, 0, cxb + g)),
            pl.BlockSpec((1, halo, M_STATE), lambda i, g, c: (i, 0, cbb + g)),
            pl.BlockSpec((1, halo, M_STATE), lambda i, g, c: (i, 0, ccb + g)),
            pl.BlockSpec((1, M_HPG, M_HEAD_DIM, M_STATE), lambda i, g, c: (i, g, 0, 0)),
        ]
        args += [conv_state, conv_state, conv_state, ssm_state]
    return pl.pallas_call(
        functools.partial(_ssd_kernel, tl=tl, has_state=has_state),
        grid=(b, groups, nc),
        in_specs=in_specs,
        out_specs=[pl.BlockSpec((1, tl, GROUP_W), lambda i, g, c: (i, c, g)),
                   pl.BlockSpec((1, M_HPG, M_HEAD_DIM, M_STATE), lambda i, g, c: (i, g, 0, 0))],
        out_shape=[jax.ShapeDtypeStruct((b, t, d_inner), BF16),
                   jax.ShapeDtypeStruct((b, heads, M_HEAD_DIM, M_STATE), F32)],
        scratch_shapes=[pltpu.VMEM((tl + 8, GROUP_W), F32),
                        pltpu.VMEM((tl + 8, M_STATE), F32),
                        pltpu.VMEM((tl + 8, M_STATE), F32),
                        pltpu.VMEM((M_STATE, GROUP_W), F32)],
        compiler_params=_params("arbitrary", "arbitrary", "arbitrary"),
        name="ssd",
    )(*args)


def _attn_kernel(*refs, tq, tk, rc, nh, n_qt, past, tkp, pos0, lam_init):
    lq1_ref, lk1_ref, lq2_ref, lk2_ref, nw_ref, q_ref, k_ref, v_ref, za_ref = refs[:9]
    if past:
        kp_ref, vp_ref = refs[9:11]
        rest = refs[11:]
    else:
        rest = refs[9:]
    o_ref, qz_scr, s_scr, p_scr, m_scr, l_scr, acc_scr = rest
    qi = pl.program_id(2)
    rows = A_REP * tq

    hq = tq // nh
    hrows = rows // nh
    row_blocks = [(half, r, half * hrows + r * hq) for half in range(nh) for r in range(A_REP)]
    lane = lax.broadcasted_iota(jnp.int32, (hq, KV_W), 1)
    zero = jnp.zeros((), BF16)
    for half, r, row0 in row_blocks:
        q = q_ref[0, half * hq:(half + 1) * hq, r * KV_W:(r + 1) * KV_W]
        qz_scr[0, row0:row0 + hq, :] = jnp.where(lane < A_HEAD_DIM, q, zero)
        qz_scr[1, row0:row0 + hq, :] = jnp.where(lane >= A_HEAD_DIM, q, zero)
    m_scr[...] = jnp.full(m_scr.shape, -jnp.inf, F32)
    l_scr[...] = jnp.zeros(l_scr.shape, F32)
    acc_scr[...] = jnp.zeros(acc_scr.shape, F32)

    def tile(k, v, nkeys, mask_fn, row_lo=0, row_hi=rows):
        reps = nkeys // LANES
        rr = slice(row_lo, row_hi)
        for comp in range(2):
            s_scr[comp, rr, :nkeys] = _dot_nt(qz_scr[comp, rr, :], k)
            for i in range(row_lo // rc, row_hi // rc):
                rs = slice(i * rc, (i + 1) * rc)
                s = s_scr[comp, rs, :nkeys]
                mask = None if mask_fn is None else mask_fn(i)
                if mask is not None:
                    s = jnp.where(mask, s, -jnp.inf)
                m_old = m_scr[comp, rs, :]
                m_new = jnp.maximum(m_old, jnp.max(s, axis=-1, keepdims=True))
                p = jnp.exp2(s - jnp.concatenate([m_new] * reps, axis=1))
                alpha = jnp.exp2(m_old - m_new)
                psum = p[:, :LANES]
                for u in range(1, reps):
                    psum = psum + p[:, u * LANES:(u + 1) * LANES]
                l_scr[comp, rs, :] = alpha * l_scr[comp, rs, :] + psum
                acc_scr[comp, rs, :] = alpha * acc_scr[comp, rs, :]
                m_scr[comp, rs, :] = m_new
                p_scr[comp, rs, :nkeys] = p.astype(BF16)
            acc_scr[comp, rr, :] = acc_scr[comp, rr, :] + _dot(p_scr[comp, rr, :nkeys], v)

    def diag_mask(i, nkeys, n_valid, k_pos0, masked_halves=(0, 1)):
        half, rem = divmod(i * rc, hrows)
        if half not in masked_halves:
            return None
        t0 = half * hq + rem % hq
        qpos = pos0 + qi * tq + t0 + lax.broadcasted_iota(jnp.int32, (rc, nkeys), 0)
        kidx = lax.broadcasted_iota(jnp.int32, (rc, nkeys), 1)
        ok = ((k_pos0 + kidx) // CHUNK) <= (qpos // CHUNK)
        if n_valid < nkeys:
            ok = ok & (kidx < n_valid)
        return ok

    if past:
        def pbody(j, carry):
            sl = pl.ds(pl.multiple_of(j * tkp, tkp), tkp)
            tile(kp_ref[0, sl, :].astype(BF16), vp_ref[0, sl, :].astype(BF16), tkp, None)
            return carry

        lax.fori_loop(0, past // tkp, pbody, 0)

    def full_tile(j):
        sl = pl.ds(pl.multiple_of(j * tk, tk), tk)
        tile(k_ref[0, sl, :], v_ref[0, sl, :], tk, None)

    def pair_body(j2, carry):
        full_tile(2 * j2)
        full_tile(2 * j2 + 1)
        return carry

    if n_qt > 1:
        n_full = (qi * tq) // tk
        lax.fori_loop(0, n_full // 2, pair_body, 0)
        pl.when(n_full % 2 == 1)(lambda: full_tile(n_full - 1))
    k0 = qi * tq
    if nh == 2:
        sl_a = pl.ds(pl.multiple_of(k0, hq), hq)
        sl_b = pl.ds(pl.multiple_of(k0 + hq, hq), hq)
        tile(k_ref[0, sl_a, :], v_ref[0, sl_a, :], hq,
             functools.partial(diag_mask, nkeys=hq, n_valid=hq, k_pos0=pos0 + k0, masked_halves=(0,)))
        tile(k_ref[0, sl_b, :], v_ref[0, sl_b, :], hq,
             functools.partial(diag_mask, nkeys=hq, n_valid=hq, k_pos0=pos0 + k0 + hq), row_lo=hrows)
    else:
        tkd = max(tq, LANES)
        sl = pl.ds(pl.multiple_of(k0, tq), tq)
        k, v = k_ref[0, sl, :], v_ref[0, sl, :]
        if tkd > tq:
            pad = jnp.zeros((tkd - tq, KV_W), BF16)
            k, v = jnp.concatenate([k, pad], axis=0), jnp.concatenate([v, pad], axis=0)
        tile(k, v, tkd, functools.partial(diag_mask, nkeys=tkd, n_valid=tq, k_pos0=pos0 + k0))

    lam = (jnp.exp(jnp.sum(lq1_ref[...] * lk1_ref[...], axis=-1, keepdims=True))
           - jnp.exp(jnp.sum(lq2_ref[...] * lk2_ref[...], axis=-1, keepdims=True)) + lam_init)
    l0 = jnp.sum(l_scr[0], axis=-1, keepdims=True)
    l1 = jnp.sum(l_scr[1], axis=-1, keepdims=True)
    o = acc_scr[0] / l0 - lam * (acc_scr[1] / l1)
    ms = jnp.mean(o * o, axis=-1, keepdims=True)
    on = o * lax.rsqrt(ms + EPS) * nw_ref[...] * (1.0 - lam_init)
    for half, r, row0 in row_blocks:
        ts, sl = slice(half * hq, (half + 1) * hq), slice(r * KV_W, (r + 1) * KV_W)
        o_ref[0, ts, sl] = (on[row0:row0 + hq, :] * _silu(za_ref[0, ts, sl])).astype(o_ref.dtype)


def _attention(qr, kb, vb, proj, za_off, k_past, v_past, lam_vecs, norm_w, lam_init, pos0):
    b, t, d = qr.shape
    kvh = d // QH_W
    past = 0 if k_past is None else k_past.shape[1]
    tq = _pick(t, 512, CHUNK)
    tk = tq
    tkp = _pick(past, 2048, LANES) if past else 0
    nh = 2 if tq % (2 * LANES) == 0 else 1
    rc = min(64, tq // nh)
    zb = za_off // QH_W
    rows = A_REP * tq
    smax = max(tk, LANES, tkp)
    vec = pl.BlockSpec((1, A_HEAD_DIM), lambda i, h, j: (0, 0))
    in_specs = [vec, vec, vec, vec,
                pl.BlockSpec((1, KV_W), lambda i, h, j: (0, 0)),
                pl.BlockSpec((1, tq, QH_W), lambda i, h, j: (i, j, h)),
                pl.BlockSpec((1, t, KV_W), lambda i, h, j: (i, 0, h)),
                pl.BlockSpec((1, t, KV_W), lambda i, h, j: (i, 0, h)),
                pl.BlockSpec((1, tq, QH_W), lambda i, h, j: (i, j, zb + h))]
    args = [v.reshape(1, A_HEAD_DIM) for v in lam_vecs] + [norm_w.reshape(1, KV_W), qr, kb, vb, proj]
    if past:
        in_specs += [pl.BlockSpec((1, past, KV_W), lambda i, h, j: (i, 0, h)),
                     pl.BlockSpec((1, past, KV_W), lambda i, h, j: (i, 0, h))]
        args += [k_past, v_past]
    return pl.pallas_call(
        functools.partial(_attn_kernel, tq=tq, tk=tk, rc=rc, nh=nh, n_qt=t // tq, past=past, tkp=tkp, pos0=pos0,
                          lam_init=lam_init),
        grid=(b, kvh, t // tq),
        in_specs=in_specs,
        out_specs=pl.BlockSpec((1, tq, QH_W), lambda i, h, j: (i, j, h)),
        out_shape=jax.ShapeDtypeStruct((b, t, d), BF16),
        scratch_shapes=[pltpu.VMEM((2, rows, KV_W), BF16),
                        pltpu.VMEM((2, rows, smax), F32),
                        pltpu.VMEM((2, rows, smax), BF16),
                        pltpu.VMEM((2, rows, LANES), F32),
                        pltpu.VMEM((2, rows, LANES), F32),
                        pltpu.VMEM((2, rows, KV_W), F32)],
        compiler_params=_params("arbitrary", "arbitrary", "arbitrary"),
        name="diff_attn",
    )(*args)


def _merge_kernel(ym_ref, ya_ref, wm_ref, wa_ref, gm_ref, ga_ref, o_ref):
    pm = _dot(ym_ref[...], wm_ref[...])
    pa = _dot(ya_ref[...], wa_ref[...])
    o_ref[...] = (jax.nn.sigmoid(gm_ref[...]) * pm + jax.nn.sigmoid(ga_ref[...]) * pa).astype(o_ref.dtype)


def _merge(y_m, y_a, w_pm, w_pa, proj2d, gm_off, ga_off):
    m, km = y_m.shape
    ka = y_a.shape[1]
    d = w_pm.shape[1]
    tn = _pick(d, 512, LANES)
    gmb, gab = gm_off // tn, ga_off // tn
    many_rows = m >= 16 * 512
    tm = _pick(m, 512 if many_rows else 256)
    once = pl.Buffered(1) if many_rows else None
    return pl.pallas_call(
        _merge_kernel,
        grid=(d // tn, m // tm),
        in_specs=[pl.BlockSpec((tm, km), lambda j, i: (i, 0)),
                  pl.BlockSpec((tm, ka), lambda j, i: (i, 0)),
                  pl.BlockSpec((km, tn), lambda j, i: (0, j), pipeline_mode=once),
                  pl.BlockSpec((ka, tn), lambda j, i: (0, j), pipeline_mode=once),
                  pl.BlockSpec((tm, tn), lambda j, i: (i, gmb + j)),
                  pl.BlockSpec((tm, tn), lambda j, i: (i, gab + j))],
        out_specs=pl.BlockSpec((tm, tn), lambda j, i: (i, j)),
        out_shape=jax.ShapeDtypeStruct((m, d), BF16),
        compiler_params=_params("arbitrary", "arbitrary"),
        name="merge",
    )(y_m, y_a, w_pm, w_pa, proj2d, proj2d)


def _out_kernel(mg_ref, w_ref, x_ref, gate_ref, fw_ref, o_ref, *, tn):
    j = pl.program_id(2)
    cols = pl.ds(pl.multiple_of(j * tn, tn), tn)
    o_ref[0, :, cols] = x_ref[0] + gate_ref[0] * _dot(mg_ref[0], w_ref[...])

    @pl.when(j == pl.num_programs(2) - 1)
    def _():
        r = o_ref[0]
        ms = jnp.mean(r * r, axis=-1, keepdims=True)
        o_ref[0] = r * lax.rsqrt(ms + EPS) * fw_ref[...]


def _out_proj(merged, w_out, x, gate, final_w):
    shape = x.shape
    b, t, d = shape
    gate = gate.reshape(b, 1, d)
    per_row_gate = t < 512 and b > 1
    if per_row_gate:
        gate = jnp.broadcast_to(gate, (b, t, d)).reshape(1, b * t, d)
        x = x.reshape(1, b * t, d)
        b, t = 1, b * t
    tm = _pick(t, 512)
    tn = _pick(d, 512 if per_row_gate else 1024, LANES)
    gate_spec = (pl.BlockSpec((1, tm, tn), lambda i, r, j: (i, r, j)) if per_row_gate else
                 pl.BlockSpec((1, 1, tn), lambda i, r, j: (i, 0, j)))
    out = pl.pallas_call(
        functools.partial(_out_kernel, tn=tn),
        grid=(b, t // tm, d // tn),
        in_specs=[pl.BlockSpec((1, tm, d), lambda i, r, j: (i, r, 0)),
                  pl.BlockSpec((d, tn), lambda i, r, j: (0, j)),
                  pl.BlockSpec((1, tm, tn), lambda i, r, j: (i, r, j)),
                  gate_spec,
                  pl.BlockSpec((1, d), lambda i, r, j: (0, 0))],
        out_specs=pl.BlockSpec((1, tm, d), lambda i, r, j: (i, r, 0)),
        out_shape=jax.ShapeDtypeStruct((b, t, d), F32),
        compiler_params=_params("arbitrary", "arbitrary", "arbitrary"),
        name="out_proj",
    )(merged.reshape(b, t, d), w_out, x, gate, final_w.reshape(1, d))
    return out.reshape(shape)


def _layer_path(x, mod, k_past, v_past, conv_state, ssm_state, pos0, w, lam_init, final_w):
    b, t, d = x.shape
    d_inner = 2 * d
    groups = d_inner // GROUP_W
    bcw = groups * M_STATE
    conv_dim = d_inner + 2 * bcw
    kvw = (d // QH_W) * KV_W
    z_off, xbc_off = 0, d_inner
    k_off, v_off, za_off, gm_off, ga_off = d, d + kvw, d + 2 * kvw, 2 * d + 2 * kvw, 3 * d + 2 * kvw
    assert k_off % kvw == 0 and za_off % QH_W == 0, "consumer column blocks must be block-aligned"
    shift, scale, gate = mod[:, :d], mod[:, d:2 * d], mod[:, 2 * d:]

    h = _prenorm(x, w['norm_w'], scale, shift).reshape(b * t, d)
    proj_m = _matmul(h, w['w_in_m'], "in_proj_m").reshape(b, t, -1)
    proj_a = _matmul(h, w['w_in_a'], "in_proj_a").reshape(b, t, -1)
    dt = _dt_proj(h, w['w_in'], w['layer'], w['dt_col0'], w['dt_bias']).reshape(b, t, -1)

    pos = pos0 + jnp.arange(t, dtype=jnp.int32)
    qr, k_new, v_new, kb, vb = _rope_kv(proj_a, pos, d, kvw, k_off // kvw, v_off // kvw)

    y_m, ssm_new = _ssd(proj_m, dt, conv_state, ssm_state, w['conv_w'], w['conv_b'], w['a_log'], w['d_skip'],
                        w['mamba_norm_w'], z_off, xbc_off, d_inner)
    halo = M_CONV - 1
    conv_new = proj_m[:, t - halo:, xbc_off:xbc_off + conv_dim]

    y_a = _attention(qr, kb, vb, proj_a, za_off, k_past, v_past,
                     (w['lam_q1'], w['lam_k1'], w['lam_q2'], w['lam_k2']), w['attn_norm_w'], lam_init, pos0)

    merged = _merge(y_m.reshape(b * t, d_inner), y_a.reshape(b * t, d), w['w_proj_m'], w['w_proj_a'],
                    proj_a.reshape(b * t, -1), gm_off, ga_off)
    y = _out_proj(merged, w['w_out'], x, gate, final_w)
    kvh = d // QH_W
    return (y, k_new.reshape(b, t, kvh, 2, A_HEAD_DIM), v_new.reshape(b, t, kvh, KV_W), conv_new, ssm_new)


def kernel(x_prompt, x_sample, cache_k, cache_v, state_conv, state_ssm, c_prompt, c_sample,
           w_ada, b_ada, norm_w, w_in, conv_w, conv_b, dt_bias, a_log, d_skip, mamba_norm_w,
           lam_q1, lam_k1, lam_q2, lam_k2, attn_norm_w, w_proj_m, w_proj_a, w_out, final_norm_w):
    depth = w_in.shape[0]
    assert depth == 1, "the final norm is fused into the single layer's output projection"
    bp, d = c_prompt.shape
    bs = c_sample.shape[0]
    past = cache_k.shape[2]
    d_inner = 2 * d
    groups = d_inner // GROUP_W
    heads = groups * M_HPG
    conv_dim = d_inner + 2 * groups * M_STATE
    kvw = (d // QH_W) * KV_W
    sizes = (d_inner, conv_dim, heads, d, kvw, kvw, d, d, d)
    offs = [0]
    for s in sizes:
        offs.append(offs[-1] + s)

    i = 0
    w = {
        'w_in': w_in, 'layer': i, 'dt_col0': offs[2],
        'w_in_m': _round_cols(w_in, i, 0, offs[2]),
        'w_in_a': _round_cols(w_in, i, offs[3], offs[9] - offs[3]),
        'norm_w': norm_w[i], 'conv_w': conv_w[i], 'conv_b': conv_b[i], 'dt_bias': dt_bias[i], 'a_log': a_log[i],
        'd_skip': d_skip[i], 'mamba_norm_w': mamba_norm_w[i], 'lam_q1': lam_q1[i], 'lam_k1': lam_k1[i],
        'lam_q2': lam_q2[i], 'lam_k2': lam_k2[i], 'attn_norm_w': attn_norm_w[i],
        'w_proj_m': w_proj_m[i].astype(BF16), 'w_proj_a': w_proj_a[i].astype(BF16), 'w_out': w_out[i].astype(BF16),
    }
    lam_init = 0.8 - 0.6 * math.exp(-0.3 * i)
    pad_rows = -(bp + bs) % 16
    c_all = jnp.concatenate([c_prompt, c_sample, jnp.zeros((pad_rows, d), F32)], axis=0)
    mod = _ada_mod(c_all, w_ada[i], b_ada[i])

    yp, kp, vp, cp, sp = _layer_path(x_prompt, mod[:bp], None, None, None, None, 0, w, lam_init, final_norm_w)
    ck = cache_k[i].reshape(bs, past, kvw)
    cv = cache_v[i].reshape(bs, past, kvw)
    ys, kq, vq, cq, sq = _layer_path(x_sample, mod[bp:bp + bs], ck, cv, state_conv[i], state_ssm[i], past, w, lam_init,
                                     final_norm_w)
    st = lambda a: a[None]
    return (yp, ys, st(kp), st(vp), st(cp), st(sp), st(kq), st(vq), st(cq), st(sq))
```
